```python
import math
import jax, jax.numpy as jnp
from jax import lax
import numpy as np

D_MODEL = 2048
BATCH = 4
SEQ = 8192
DEPTH = 2
DEC_BATCH = 16
DEC_SEQ = 2048
PAST_LEN = 128

GRID_W = 64
HEAD_DIM = 64
MIX_WIDTH = 3072
Q_BLOCK = 128
EPS = 1e-6
ROPE_THETA = 500000.0
ROPE_DIMS = HEAD_DIM // 4
AXIAL_THETA = 10000.0
A_HEADS = 6
A_QK = 2 * HEAD_DIM
A_V = 2 * HEAD_DIM
B_HEADS = 12
B_PATTERNS = ((128, 1), (512, 4), (2048, 16))
C_HEADS = 6
C_DK = 64
C_DV = 128
C_RANK = 16
C_CHUNK = 64
C_GATE_NORM = 16.0
D_Q_HEADS = 12
D_KV_HEADS = 4
D_FF = 5632
IN_SPLITS = (A_HEADS * A_QK, A_HEADS * A_QK, A_HEADS * A_V,
             B_HEADS * HEAD_DIM, B_HEADS * HEAD_DIM, B_HEADS * HEAD_DIM,
             C_HEADS * C_DK, C_HEADS * C_DK, C_HEADS * C_DV, C_HEADS * C_DV, 2 * C_RANK,
             D_Q_HEADS * HEAD_DIM, D_KV_HEADS * HEAD_DIM, D_KV_HEADS * HEAD_DIM)
N_IN = 8224

kernel_name = 'hybrid_parallel_heads_bidir_encoder'


def _rms_norm(x, g):
    xf = x.astype(jnp.float32)
    y = xf * lax.rsqrt(jnp.mean(xf * xf, axis=-1, keepdims=True) + EPS)
    return (y * g.astype(jnp.float32)).astype(x.dtype)


def _swiglu(x, wg, wu, wd):
    return (jax.nn.silu(x @ wg) * (x @ wu)) @ wd


def _heads(t, n):
    b, l, _ = t.shape
    return t.reshape(b, l, n, -1).transpose(0, 2, 1, 3)


def _merge(t):
    b, h, l, d = t.shape
    return t.transpose(0, 2, 1, 3).reshape(b, l, h * d)


def _rope_cos_sin(pos, dims, theta):
    inv = theta ** (-jnp.arange(0, dims, 2, dtype=jnp.float32) / dims)
    ang = pos.astype(jnp.float32)[:, None] * inv[None, :]
    return jnp.cos(ang), jnp.sin(ang)


def _rotate(x, cos, sin):
    half = x.shape[-1] // 2
    x1 = x[..., :half].astype(jnp.float32)
    x2 = x[..., half:].astype(jnp.float32)
    return jnp.concatenate([x1 * cos - x2 * sin, x2 * cos + x1 * sin], axis=-1).astype(x.dtype)


def _partial_rope(x, cos, sin):
    return jnp.concatenate([_rotate(x[..., :ROPE_DIMS], cos, sin), x[..., ROPE_DIMS:]], axis=-1)


def _axial_rope(x, r_cos, r_sin, c_cos, c_sin):
    half = HEAD_DIM // 2
    return jnp.concatenate([_rotate(x[..., :half], r_cos, r_sin),
                            _rotate(x[..., half:], c_cos, c_sin)], axis=-1)


def _diff_attention(q, k, v, lam):
    b, h, L, _ = q.shape
    nb = L // Q_BLOCK
    scale = HEAD_DIM ** -0.5
    qs = jnp.stack([q[..., :HEAD_DIM], q[..., HEAD_DIM:]], 0)
    ks = jnp.stack([k[..., :HEAD_DIM], k[..., HEAD_DIM:]], 0)
    qs = qs.reshape(2, b, h, nb, Q_BLOCK, HEAD_DIM).transpose(3, 0, 1, 2, 4, 5)

    def block(qb):
        s = jnp.einsum('nbhqd,nbhkd->nbhqk', qb, ks).astype(jnp.float32) * scale
        p = jax.nn.softmax(s, axis=-1)
        a = p[0] - lam * p[1]
        return jnp.einsum('bhqk,bhkv->bhqv', a.astype(v.dtype), v)

    o = lax.map(block, qs)
    return o.transpose(1, 2, 0, 3, 4).reshape(b, h, L, v.shape[-1])


def _gqa_attention(q, k, v):
    b, hq, L, d = q.shape
    g = hq // D_KV_HEADS
    nb = L // Q_BLOCK
    scale = d ** -0.5
    qs = q.reshape(b, D_KV_HEADS, g, nb, Q_BLOCK, d).transpose(3, 0, 1, 2, 4, 5)

    def block(qb):
        s = jnp.einsum('bkgqd,bksd->bkgqs', qb, k).astype(jnp.float32) * scale
        p = jax.nn.softmax(s, axis=-1)
        return jnp.einsum('bkgqs,bksd->bkgqd', p.astype(v.dtype), v)

    o = lax.map(block, qs)
    return o.transpose(1, 2, 3, 0, 4, 5).reshape(b, hq, L, d)


def _banded_window_attention(q, k, v, n_side):
    lead = q.shape[:-2]
    N, d = q.shape[-2], q.shape[-1]
    blk = n_side
    nb = -(-N // blk)
    P = nb * blk
    nlead = len(lead)
    qp = jnp.pad(q, [(0, 0)] * nlead + [(0, P - N), (0, 0)]).reshape(*lead, nb, blk, d)
    kv_pad = [(0, 0)] * nlead + [(blk, P - N + blk), (0, 0)]
    kp = jnp.pad(k, kv_pad).reshape(*lead, nb + 2, blk, d)
    vp = jnp.pad(v, kv_pad).reshape(*lead, nb + 2, blk, d)
    kw = jnp.concatenate([kp[..., :-2, :, :], kp[..., 1:-1, :, :], kp[..., 2:, :, :]], axis=-2)
    vw = jnp.concatenate([vp[..., :-2, :, :], vp[..., 1:-1, :, :], vp[..., 2:, :, :]], axis=-2)
    s = jnp.einsum('...nqd,...nkd->...nqk', qp, kw).astype(jnp.float32) * (d ** -0.5)
    qpos = jnp.arange(nb)[:, None] * blk + jnp.arange(blk)[None, :]
    kpos = jnp.arange(nb)[:, None] * blk - blk + jnp.arange(3 * blk)[None, :]
    rel = kpos[:, None, :] - qpos[:, :, None]
    valid = (jnp.abs(rel) <= n_side) & (kpos[:, None, :] >= 0) & (kpos[:, None, :] < N)
    s = jnp.where(valid, s, -1e30)
    m = jnp.max(s, axis=-1, keepdims=True)
    e = jnp.exp(s - m)
    den = jnp.sum(e, axis=-1, keepdims=True)
    p = e / den
    lse = (m + jnp.log(den))[..., 0]
    o = jnp.einsum('...nqk,...nkd->...nqd', p.astype(v.dtype), vw)
    o = o.reshape(*lead, P, d)[..., :N, :]
    lse = lse.reshape(*lead, P)[..., :N]
    return o, lse


def _dilated_attention(q, k, v):
    b, h, L, d = q.shape
    outs, lses = [], []
    for window, dil in B_PATTERNS:
        n_side = window // (2 * dil)

        def to_sub(t):
            return t.reshape(b, h, L // dil, dil, d).swapaxes(2, 3)

        o, lse = _banded_window_attention(to_sub(q), to_sub(k), to_sub(v), n_side)
        outs.append(o.swapaxes(2, 3).reshape(b, h, L, d))
        lses.append(lse.swapaxes(2, 3).reshape(b, h, L))
    w = jax.nn.softmax(jnp.stack(lses, 0), axis=0)
    return jnp.einsum('gbhl,gbhld->bhld', w.astype(q.dtype), jnp.stack(outs, 0))


def _gla_scan(q, k, v, g):
    b, h, L, dk = q.shape
    dv = v.shape[-1]
    n = L // C_CHUNK
    q = q.reshape(b, h, n, C_CHUNK, dk)
    k = k.reshape(b, h, n, C_CHUNK, dk)
    v = v.reshape(b, h, n, C_CHUNK, dv)
    g = g.reshape(b, h, n, C_CHUNK, dk)
    cum = jnp.cumsum(g, axis=-2)
    last = cum[..., -1:, :]
    q_dec = q * jnp.exp(cum)
    att = jnp.einsum('bhncd,bhnsd->bhncs', q_dec, k * jnp.exp(-cum))
    att = jnp.where(jnp.tril(jnp.ones((C_CHUNK, C_CHUNK), dtype=bool)), att, 0.0)
    o_intra = jnp.einsum('bhncs,bhnse->bhnce', att, v)
    u = jnp.einsum('bhncd,bhnce->bhnde', k * jnp.exp(last - cum), v)
    decay = jnp.exp(last[..., 0, :])

    def step(state, inp):
        dec, uu = inp
        return dec[..., None] * state + uu, state

    _, s_in = lax.scan(step, jnp.zeros((b, h, dk, dv), jnp.float32),
                       (jnp.moveaxis(decay, 2, 0), jnp.moveaxis(u, 2, 0)))
    s_in = jnp.moveaxis(s_in, 0, 2)
    o_inter = jnp.einsum('bhncd,bhnde->bhnce', q_dec, s_in)
    return (o_intra + o_inter).reshape(b, h, L, dv)


def _bidir_gla(q, k, v, g_f, g_b):
    dt = v.dtype
    q = q.astype(jnp.float32) * (C_DK ** -0.5)
    k = k.astype(jnp.float32)
    v = v.astype(jnp.float32)
    fwd = _gla_scan(q, k, v, g_f)

    def flip(t):
        return jnp.flip(t, axis=2)

    bwd = flip(_gla_scan(flip(q), flip(k), flip(v), flip(g_b)))
    return (fwd + bwd).astype(dt)


def _trunk(x, ffn1_norm, ffn1_w_gate, ffn1_w_up, ffn1_w_down, mix_norm, w_in, w_out,
           diff_lambda_q1, diff_lambda_k1, diff_lambda_q2, diff_lambda_k2, diff_out_norm,
           gla_gate_up_f, gla_gate_bias_f, gla_gate_up_b, gla_gate_bias_b, gla_out_norm,
           gqa_q_norm, gqa_k_norm, ffn2_norm, ffn2_w_gate, ffn2_w_up, ffn2_w_down, final_norm):
    L = x.shape[1]
    rows = L // GRID_W
    t = jnp.arange(L, dtype=jnp.float32)
    p_cos, p_sin = _rope_cos_sin(t, ROPE_DIMS, ROPE_THETA)
    row_pos = jnp.repeat(jnp.arange(rows, dtype=jnp.float32), GRID_W)
    col_pos = jnp.tile(jnp.arange(GRID_W, dtype=jnp.float32), rows)
    r_cos, r_sin = _rope_cos_sin(row_pos, HEAD_DIM // 2, AXIAL_THETA)
    c_cos, c_sin = _rope_cos_sin(col_pos, HEAD_DIM // 2, AXIAL_THETA)
    split_points = np.cumsum(np.array(IN_SPLITS))[:-1].tolist()

    for l in range(DEPTH):
        x = x + 0.5 * _swiglu(_rms_norm(x, ffn1_norm[l]), ffn1_w_gate[l], ffn1_w_up[l], ffn1_w_down[l])

        h = _rms_norm(x, mix_norm[l])
        u = h @ w_in[l]
        (a_q, a_k, a_v, b_q, b_k, b_v, c_q, c_k, c_v, c_g, c_low,
         d_q, d_k, d_v) = jnp.split(u, split_points, axis=-1)

        aq = _heads(a_q, A_HEADS)
        ak = _heads(a_k, A_HEADS)
        av = _heads(a_v, A_HEADS)
        aq = jnp.concatenate([_partial_rope(aq[..., :HEAD_DIM], p_cos, p_sin),
                              _partial_rope(aq[..., HEAD_DIM:], p_cos, p_sin)], axis=-1)
        ak = jnp.concatenate([_partial_rope(ak[..., :HEAD_DIM], p_cos, p_sin),
                              _partial_rope(ak[..., HEAD_DIM:], p_cos, p_sin)], axis=-1)
        lam_init = 0.8 - 0.6 * math.exp(-0.3 * l)
        lam = (jnp.exp(jnp.sum(diff_lambda_q1[l].astype(jnp.float32) * diff_lambda_k1[l].astype(jnp.float32)))
               - jnp.exp(jnp.sum(diff_lambda_q2[l].astype(jnp.float32) * diff_lambda_k2[l].astype(jnp.float32)))
               + lam_init)
        o_a = _diff_attention(aq, ak, av, lam)
        o_a = _merge(_rms_norm(o_a, diff_out_norm[l]) * (1.0 - lam_init))

        bq = _partial_rope(_heads(b_q, B_HEADS), p_cos, p_sin)
        bk = _partial_rope(_heads(b_k, B_HEADS), p_cos, p_sin)
        bv = _heads(b_v, B_HEADS)
        o_b = _merge(_dilated_attention(bq, bk, bv))

        low_f = c_low[..., :C_RANK]
        low_b = c_low[..., C_RANK:]
        g_f = _heads(jax.nn.log_sigmoid((low_f @ gla_gate_up_f[l] + gla_gate_bias_f[l]).astype(jnp.float32)) / C_GATE_NORM, C_HEADS)
        g_b = _heads(jax.nn.log_sigmoid((low_b @ gla_gate_up_b[l] + gla_gate_bias_b[l]).astype(jnp.float32)) / C_GATE_NORM, C_HEADS)
        o_c = _bidir_gla(_heads(c_q, C_HEADS), _heads(c_k, C_HEADS), _heads(c_v, C_HEADS), g_f, g_b)
        o_c = _merge(_rms_norm(o_c, gla_out_norm[l]) * jax.nn.silu(_heads(c_g, C_HEADS)))

        dq = _axial_rope(_rms_norm(_heads(d_q, D_Q_HEADS), gqa_q_norm[l]), r_cos, r_sin, c_cos, c_sin)
        dk = _axial_rope(_rms_norm(_heads(d_k, D_KV_HEADS), gqa_k_norm[l]), r_cos, r_sin, c_cos, c_sin)
        dv = _heads(d_v, D_KV_HEADS)
        o_d = _merge(_gqa_attention(dq, dk, dv))

        x = x + jnp.concatenate([o_a, o_b, o_c, o_d], axis=-1) @ w_out[l]

        x = x + 0.5 * _swiglu(_rms_norm(x, ffn2_norm[l]), ffn2_w_gate[l], ffn2_w_up[l], ffn2_w_down[l])
    return _rms_norm(x, final_norm)


def setup_inputs(seed: int = 0) -> dict:
    key = jax.random.key(seed)
    ks = jax.random.split(key, 26)

    def nrm(k, shape, scale):
        return jax.random.normal(k, shape, jnp.float32) * scale

    def gain(k, shape):
        return 1.0 + 0.02 * jax.random.normal(k, shape, jnp.float32)

    return {
        'x_prompt': nrm(ks[0], (BATCH, SEQ, D_MODEL), 1.0),
        'x_sample': nrm(ks[1], (DEC_BATCH, DEC_SEQ, D_MODEL), 1.0),
        'ffn1_norm': gain(ks[2], (DEPTH, D_MODEL)),
        'ffn1_w_gate': nrm(ks[3], (DEPTH, D_MODEL, D_FF), D_MODEL ** -0.5),
        'ffn1_w_up': nrm(ks[4], (DEPTH, D_MODEL, D_FF), D_MODEL ** -0.5),
        'ffn1_w_down': nrm(ks[5], (DEPTH, D_FF, D_MODEL), D_FF ** -0.5),
        'mix_norm': gain(ks[6], (DEPTH, D_MODEL)),
        'w_in': nrm(ks[7], (DEPTH, D_MODEL, N_IN), D_MODEL ** -0.5),
        'w_out': nrm(ks[8], (DEPTH, MIX_WIDTH, D_MODEL), MIX_WIDTH ** -0.5),
        'diff_lambda_q1': nrm(ks[9], (DEPTH, HEAD_DIM), 0.1),
        'diff_lambda_k1': nrm(ks[10], (DEPTH, HEAD_DIM), 0.1),
        'diff_lambda_q2': nrm(ks[11], (DEPTH, HEAD_DIM), 0.1),
        'diff_lambda_k2': nrm(ks[12], (DEPTH, HEAD_DIM), 0.1),
        'diff_out_norm': gain(ks[13], (DEPTH, A_V)),
        'gla_gate_up_f': nrm(ks[14], (DEPTH, C_RANK, C_HEADS * C_DK), C_RANK ** -0.5),
        'gla_gate_bias_f': nrm(ks[15], (DEPTH, C_HEADS * C_DK), 0.1),
        'gla_gate_up_b': nrm(ks[16], (DEPTH, C_RANK, C_HEADS * C_DK), C_RANK ** -0.5),
        'gla_gate_bias_b': nrm(ks[17], (DEPTH, C_HEADS * C_DK), 0.1),
        'gla_out_norm': gain(ks[18], (DEPTH, C_DV)),
        'gqa_q_norm': gain(ks[19], (DEPTH, HEAD_DIM)),
        'gqa_k_norm': gain(ks[20], (DEPTH, HEAD_DIM)),
        'ffn2_norm': gain(ks[21], (DEPTH, D_MODEL)),
        'ffn2_w_gate': nrm(ks[22], (DEPTH, D_MODEL, D_FF), D_MODEL ** -0.5),
        'ffn2_w_up': nrm(ks[23], (DEPTH, D_MODEL, D_FF), D_MODEL ** -0.5),
        'ffn2_w_down': nrm(ks[24], (DEPTH, D_FF, D_MODEL), D_FF ** -0.5),
        'final_norm': gain(ks[25], (D_MODEL,)),
    }


def reference(x_prompt, x_sample, ffn1_norm, ffn1_w_gate, ffn1_w_up, ffn1_w_down, mix_norm, w_in, w_out,
              diff_lambda_q1, diff_lambda_k1, diff_lambda_q2, diff_lambda_k2, diff_out_norm,
              gla_gate_up_f, gla_gate_bias_f, gla_gate_up_b, gla_gate_bias_b, gla_out_norm,
              gqa_q_norm, gqa_k_norm, ffn2_norm, ffn2_w_gate, ffn2_w_up, ffn2_w_down, final_norm):
    y_prompt = _trunk(x_prompt, ffn1_norm, ffn1_w_gate, ffn1_w_up, ffn1_w_down, mix_norm, w_in, w_out,
                      diff_lambda_q1, diff_lambda_k1, diff_lambda_q2, diff_lambda_k2, diff_out_norm,
                      gla_gate_up_f, gla_gate_bias_f, gla_gate_up_b, gla_gate_bias_b, gla_out_norm,
                      gqa_q_norm, gqa_k_norm, ffn2_norm, ffn2_w_gate, ffn2_w_up, ffn2_w_down, final_norm)
    y_sample = _trunk(x_sample, ffn1_norm, ffn1_w_gate, ffn1_w_up, ffn1_w_down, mix_norm, w_in, w_out,
                      diff_lambda_q1, diff_lambda_k1, diff_lambda_q2, diff_lambda_k2, diff_out_norm,
                      gla_gate_up_f, gla_gate_bias_f, gla_gate_up_b, gla_gate_bias_b, gla_out_norm,
                      gqa_q_norm, gqa_k_norm, ffn2_norm, ffn2_w_gate, ffn2_w_up, ffn2_w_down, final_norm)
    return (y_prompt, y_sample)
```

```python
import functools
import math

import jax
import jax.numpy as jnp
from jax import lax
from jax.experimental import pallas as pl
from jax.experimental.pallas import tpu as pltpu

F32 = jnp.float32
BF16 = jnp.bfloat16

HEAD_DIM = 64
EPS = 1e-6
ROPE_THETA = 500000.0
ROPE_DIMS = HEAD_DIM // 4
AXIAL_THETA = 10000.0
GRID_W = 64
A_HEADS = 6
B_HEADS = 12
B_PATTERNS = ((128, 1), (512, 4), (2048, 16))
B_SIDE = 64
C_HEADS = 6
C_RANK = 16
C_CHUNK = 64
C_GATE_NORM = 16.0
D_Q_HEADS = 12
D_KV_HEADS = 4
D_GROUP = D_Q_HEADS // D_KV_HEADS
QK_SCALE = HEAD_DIM ** -0.5

LANE = 128
NEG = -1e30

U_AQ = 0
U_AK = 768
U_BQ = 1536
U_BK = 2304
U_DQ = 3072
U_DK = 4608
U_AV = 5120
U_BV = 5888
U_DV = 6656
U_CQ = 7168
U_CK = 7552
U_CV = 7936
U_CG = 8704
U_CLOW = 9472
NU = 9728
ROPE_COLS = U_DQ
AXIAL_COLS = U_AV - U_DQ
D_PAD_W = D_Q_HEADS * LANE


def _cparams(sem, vmem_mb=56):
    return pltpu.CompilerParams(dimension_semantics=sem, vmem_limit_bytes=vmem_mb * 1024 * 1024)


def _dot(a, b):
    return jnp.dot(a, b, preferred_element_type=F32)


def _dot_nt(a, b):
    return lax.dot_general(a, b, (((1,), (1,)), ((), ())), preferred_element_type=F32)


def _rms(x, g):
    ms = jnp.mean(x * x, axis=-1, keepdims=True)
    return x * lax.rsqrt(ms + EPS) * g


def _ffn_body(x_ref, g_ref, wg_ref, wu_ref, wd_ref, fg_ref, o_ref, h_ref, acc_ref, *, final):
    j = pl.program_id(1)
    nj = pl.num_programs(1)

    @pl.when(j == 0)
    def _():
        h_ref[...] = _rms(x_ref[...], g_ref[...]).astype(BF16)

    h = h_ref[...]
    gate = _dot(h, wg_ref[...])
    up = _dot(h, wu_ref[...])
    act = (gate * (1.0 / (1.0 + jnp.exp(-gate))) * up).astype(BF16)
    part = _dot(act, wd_ref[...])

    @pl.when(j == 0)
    def _():
        acc_ref[...] = part

    @pl.when(j > 0)
    def _():
        acc_ref[...] += part

    @pl.when(j == nj - 1)
    def _():
        y = x_ref[...] + 0.5 * acc_ref[...]
        if final:
            y = _rms(y, fg_ref[...])
        o_ref[...] = y


def _ffn(x, g, wg, wu, wd, fg, *, final, tm=512, tf=512):
    T, D = x.shape
    FF = wg.shape[1]
    tm = min(tm, T)
    tf = min(tf, FF)
    assert T % tm == 0 and FF % tf == 0
    return pl.pallas_call(
        functools.partial(_ffn_body, final=final),
        out_shape=jax.ShapeDtypeStruct((T, D), F32),
        grid=(T // tm, FF // tf),
        in_specs=[
            pl.BlockSpec((tm, D), lambda i, j: (i, 0)),
            pl.BlockSpec((1, D), lambda i, j: (0, 0)),
            pl.BlockSpec((D, tf), lambda i, j: (0, j)),
            pl.BlockSpec((D, tf), lambda i, j: (0, j)),
            pl.BlockSpec((tf, D), lambda i, j: (j, 0)),
            pl.BlockSpec((1, D), lambda i, j: (0, 0)),
        ],
        out_specs=pl.BlockSpec((tm, D), lambda i, j: (i, 0)),
        scratch_shapes=[pltpu.VMEM((tm, D), BF16), pltpu.VMEM((tm, D), F32)],
        compiler_params=_cparams(("parallel", "arbitrary")),
        name="ffn",
    )(x, g, wg, wu, wd, fg)


def _rot(x, c, sa, sb, shift):
    return x * c + pltpu.roll(x, shift, 1) * sa + pltpu.roll(x, LANE - shift, 1) * sb


def _inproj_body(x_ref, g_ref, w_ref, gd_ref, c8_ref, sa8_ref, sb8_ref, cx_ref, sax_ref, sbx_ref,
                 o_ref, h_ref, *, n_rope, n_axial, tn):
    j = pl.program_id(1)

    @pl.when(j == 0)
    def _():
        h_ref[...] = _rms(x_ref[...], g_ref[...]).astype(BF16)

    acc = _dot(h_ref[...], w_ref[...])
    chunks = tn // LANE

    @pl.when(j < n_rope)
    def _():
        for c in range(chunks):
            sl = slice(c * LANE, (c + 1) * LANE)
            o_ref[:, sl] = _rot(acc[:, sl], c8_ref[...], sa8_ref[...], sb8_ref[...], ROPE_DIMS // 2).astype(BF16)

    @pl.when((j >= n_rope) & (j < n_rope + n_axial))
    def _():
        for c in range(chunks):
            sl = slice(c * LANE, (c + 1) * LANE)
            xc = acc[:, sl]
            ms = jnp.sum(xc * xc, axis=-1, keepdims=True) * (1.0 / HEAD_DIM)
            y = xc * lax.rsqrt(ms + EPS) * gd_ref[:, sl]
            o_ref[:, sl] = _rot(y, cx_ref[...], sax_ref[...], sbx_ref[...], HEAD_DIM // 4).astype(BF16)

    @pl.when(j >= n_rope + n_axial)
    def _():
        o_ref[...] = acc.astype(BF16)


def _inproj(x, g, w, gd, tabs, L, *, tm=1024, tn=512):
    T, D = x.shape
    tm = min(tm, L)
    assert T % tm == 0 and L % tm == 0 and NU % tn == 0 and ROPE_COLS % tn == 0 and AXIAL_COLS % tn == 0
    n_rope = ROPE_COLS // tn
    n_axial = AXIAL_COLS // tn
    lt = L // tm
    tab_spec = pl.BlockSpec((tm, LANE), lambda i, j: (i % lt, 0))
    return pl.pallas_call(
        functools.partial(_inproj_body, n_rope=n_rope, n_axial=n_axial, tn=tn),
        out_shape=jax.ShapeDtypeStruct((T, NU), BF16),
        grid=(T // tm, NU // tn),
        in_specs=[
            pl.BlockSpec((tm, D), lambda i, j: (i, 0)),
            pl.BlockSpec((1, D), lambda i, j: (0, 0)),
            pl.BlockSpec((D, tn), lambda i, j: (0, j)),
            pl.BlockSpec((1, tn), lambda i, j: (0, jnp.clip(j - n_rope, 0, n_axial - 1))),
            tab_spec, tab_spec, tab_spec, tab_spec, tab_spec, tab_spec,
        ],
        out_specs=pl.BlockSpec((tm, tn), lambda i, j: (i, j)),
        scratch_shapes=[pltpu.VMEM((tm, D), BF16)],
        compiler_params=_cparams(("parallel", "arbitrary")),
        name="inproj",
    )(x, g, w, gd, *tabs)


def _flash_init(m_ref, l_ref, acc_ref):
    m_ref[...] = jnp.full(m_ref.shape, NEG, F32)
    l_ref[...] = jnp.zeros(l_ref.shape, F32)
    acc_ref[...] = jnp.zeros(acc_ref.shape, F32)


def _flash_step(qq_ref, k_ref, v_ref, m_ref, l_ref, acc_ref):
    s = _dot_nt(qq_ref[...], k_ref[...])
    m_prev = m_ref[...]
    m_new = jnp.maximum(m_prev, jnp.max(s, axis=-1, keepdims=True))
    alpha = jnp.exp(m_prev - m_new)
    p = jnp.exp(s - m_new)
    l_ref[...] = alpha * l_ref[...] + jnp.sum(p, axis=-1, keepdims=True)
    acc_ref[...] = alpha * acc_ref[...] + _dot(p.astype(BF16), v_ref[...])
    m_ref[...] = m_new


def _diff_body(q_ref, k_ref, v_ref, lam_ref, gn_ref, o_ref, qq_ref, m_ref, l_ref, acc_ref, *, tq, lam_init):
    ki = pl.program_id(3)

    @pl.when(ki == 0)
    def _():
        q = q_ref[...]
        lane = lax.broadcasted_iota(jnp.int32, q.shape, 1)
        zero = jnp.zeros_like(q)
        qq_ref[0:tq, :] = jnp.where(lane < HEAD_DIM, q, zero)
        qq_ref[tq:2 * tq, :] = jnp.where(lane >= HEAD_DIM, q, zero)
        _flash_init(m_ref, l_ref, acc_ref)

    _flash_step(qq_ref, k_ref, v_ref, m_ref, l_ref, acc_ref)

    @pl.when(ki == pl.num_programs(3) - 1)
    def _():
        o = acc_ref[...] * (1.0 / l_ref[...])
        lv = lam_ref[...]
        lam = (jnp.exp(jnp.sum(lv[0:1] * lv[1:2], axis=-1, keepdims=True))
               - jnp.exp(jnp.sum(lv[2:3] * lv[3:4], axis=-1, keepdims=True)) + lam_init)
        a = o[0:tq] - lam * o[tq:2 * tq]
        o_ref[...] = (_rms(a, gn_ref[...]) * (1.0 - lam_init)).astype(BF16)


def _diff_attention(u3, lam_vecs, gn, lam_init, *, tq=512, tk=512):
    B, L, _ = u3.shape
    tq = min(tq, L)
    tk = min(tk, L)
    qb, kb, vb = U_AQ // LANE, U_AK // LANE, U_AV // LANE
    return pl.pallas_call(
        functools.partial(_diff_body, tq=tq, lam_init=lam_init),
        out_shape=jax.ShapeDtypeStruct((B, L, A_HEADS * LANE), BF16),
        grid=(B, A_HEADS, L // tq, L // tk),
        in_specs=[
            pl.BlockSpec((None, tq, LANE), lambda b, h, qi, ki: (b, qi, qb + h)),
            pl.BlockSpec((None, tk, LANE), lambda b, h, qi, ki: (b, ki, kb + h)),
            pl.BlockSpec((None, tk, LANE), lambda b, h, qi, ki: (b, ki, vb + h)),
            pl.BlockSpec((4, HEAD_DIM), lambda b, h, qi, ki: (0, 0)),
            pl.BlockSpec((1, LANE), lambda b, h, qi, ki: (0, 0)),
        ],
        out_specs=pl.BlockSpec((None, tq, LANE), lambda b, h, qi, ki: (b, qi, h)),
        scratch_shapes=[
            pltpu.VMEM((2 * tq, LANE), BF16),
            pltpu.VMEM((2 * tq, 1), F32),
            pltpu.VMEM((2 * tq, 1), F32),
            pltpu.VMEM((2 * tq, LANE), F32),
        ],
        compiler_params=_cparams(("parallel", "parallel", "parallel", "arbitrary")),
        name="diff_attn",
    )(u3, u3, u3, lam_vecs, gn)


def _gqa_body(q_ref, k_ref, v_ref, o_ref, qq_ref, m_ref, l_ref, acc_ref, *, tq):
    ki = pl.program_id(3)

    @pl.when(ki == 0)
    def _():
        for j in range(D_GROUP):
            qq_ref[j * tq:(j + 1) * tq, :] = q_ref[:, j * LANE:(j + 1) * LANE]
        _flash_init(m_ref, l_ref, acc_ref)

    _flash_step(qq_ref, k_ref, v_ref, m_ref, l_ref, acc_ref)

    @pl.when(ki == pl.num_programs(3) - 1)
    def _():
        o = (acc_ref[...] * (1.0 / l_ref[...])).astype(BF16)
        for j in range(D_GROUP):
            o_ref[:, j * LANE:(j + 1) * LANE] = o[j * tq:(j + 1) * tq]


def _gqa_attention(u3, *, tq=512, tk=512):
    B, L, _ = u3.shape
    tq = min(tq, L)
    tk = min(tk, L)
    gw = D_GROUP * LANE
    qb, kb, vb = U_DQ // gw, U_DK // LANE, U_DV // LANE
    return pl.pallas_call(
        functools.partial(_gqa_body, tq=tq),
        out_shape=jax.ShapeDtypeStruct((B, L, D_PAD_W), BF16),
        grid=(B, D_KV_HEADS, L // tq, L // tk),
        in_specs=[
            pl.BlockSpec((None, tq, gw), lambda b, g, qi, ki: (b, qi, qb + g)),
            pl.BlockSpec((None, tk, LANE), lambda b, g, qi, ki: (b, ki, kb + g)),
            pl.BlockSpec((None, tk, LANE), lambda b, g, qi, ki: (b, ki, vb + g)),
        ],
        out_specs=pl.BlockSpec((None, tq, gw), lambda b, g, qi, ki: (b, qi, g)),
        scratch_shapes=[
            pltpu.VMEM((D_GROUP * tq, LANE), BF16),
            pltpu.VMEM((D_GROUP * tq, 1), F32),
            pltpu.VMEM((D_GROUP * tq, 1), F32),
            pltpu.VMEM((D_GROUP * tq, LANE), F32),
        ],
        compiler_params=_cparams(("parallel", "parallel", "parallel", "arbitrary")),
        name="gqa_attn",
    )(u3, u3, u3)


def _dil_body(q_ref, k_ref, kp_ref, kn_ref, v_ref, vp_ref, vn_ref, o_ref, lse_ref, kbuf, vbuf, *, S, N):
    i = pl.program_id(3)
    H = B_SIDE
    kbuf[0:H, :] = kp_ref[...]
    kbuf[H:H + S, :] = k_ref[...]
    kbuf[H + S:H + S + H, :] = kn_ref[...]
    vbuf[0:H, :] = vp_ref[...]
    vbuf[H:H + S, :] = v_ref[...]
    vbuf[H + S:H + S + H, :] = vn_ref[...]

    tq = LANE
    tw = tq + 2 * H
    lane_q = lax.broadcasted_iota(jnp.int32, (tq, LANE), 1)
    row = lax.broadcasted_iota(jnp.int32, (2 * tq, tw), 0)
    col = lax.broadcasted_iota(jnp.int32, (2 * tq, tw), 1)
    rel = col - H - (row % tq)
    band = jnp.abs(rel) <= H
    for j in range(S // tq):
        q = q_ref[j * tq:(j + 1) * tq, :]
        zero = jnp.zeros_like(q)
        qq = jnp.concatenate([jnp.where(lane_q < HEAD_DIM, q, zero),
                              jnp.where(lane_q >= HEAD_DIM, q, zero)], axis=0)
        kw = kbuf[j * tq:j * tq + tw, :]
        vw = vbuf[j * tq:j * tq + tw, :]
        s = _dot_nt(qq, kw)
        kpos = i * S + j * tq - H + col
        valid = band & (kpos >= 0) & (kpos < N)
        s = jnp.where(valid, s, NEG)
        m = jnp.max(s, axis=-1, keepdims=True)
        e = jnp.exp(s - m)
        l = jnp.sum(e, axis=-1, keepdims=True)
        pv = _dot(e.astype(BF16), vw) * (1.0 / l)
        lse = m + jnp.log(l)
        sel = lane_q < HEAD_DIM
        o_ref[j * tq:(j + 1) * tq, :] = jnp.where(sel, pv[0:tq], pv[tq:2 * tq])
        lse_ref[j * tq:(j + 1) * tq, :] = jnp.where(sel, jnp.broadcast_to(lse[0:tq], (tq, LANE)),
                                                    jnp.broadcast_to(lse[tq:2 * tq], (tq, LANE)))


def _dilated_one(u3, dil, *, smax=512):
    B, L, _ = u3.shape
    N = L // dil
    S = min(smax, N)
    H = B_SIDE
    uv = u3.reshape(B, N, dil * NU)
    nub = NU // LANE
    ob = (B_HEADS // 2)
    qb, kb, vb = U_BQ // LANE, U_BK // LANE, U_BV // LANE
    sh = S // H
    nh = N // H

    def main(base):
        return pl.BlockSpec((None, S, LANE), lambda b, hp, r, i: (b, i, r * nub + base + hp))

    def prev(base):
        return pl.BlockSpec((None, H, LANE), lambda b, hp, r, i: (b, jnp.maximum(i * sh - 1, 0), r * nub + base + hp))

    def nxt(base):
        return pl.BlockSpec((None, H, LANE), lambda b, hp, r, i: (b, jnp.minimum((i + 1) * sh, nh - 1), r * nub + base + hp))

    out_spec = pl.BlockSpec((None, S, LANE), lambda b, hp, r, i: (b, i, r * ob + hp))
    o, lse = pl.pallas_call(
        functools.partial(_dil_body, S=S, N=N),
        out_shape=[jax.ShapeDtypeStruct((B, N, dil * ob * LANE), F32)] * 2,
        grid=(B, ob, dil, N // S),
        in_specs=[main(qb), main(kb), prev(kb), nxt(kb), main(vb), prev(vb), nxt(vb)],
        out_specs=[out_spec, out_spec],
        scratch_shapes=[pltpu.VMEM((S + 2 * H, LANE), BF16), pltpu.VMEM((S + 2 * H, LANE), BF16)],
        compiler_params=_cparams(("parallel", "parallel", "parallel", "arbitrary")),
        name=f"dilated_attn_d{dil}",
    )(uv, uv, uv, uv, uv, uv, uv)
    W = ob * LANE
    return o.reshape(B * L, W), lse.reshape(B * L, W)


def _combine_body(o1, l1, o2, l2, o3, l3, out_ref):
    a, b, c = l1[...], l2[...], l3[...]
    m = jnp.maximum(jnp.maximum(a, b), c)
    wa, wb, wc = jnp.exp(a - m), jnp.exp(b - m), jnp.exp(c - m)
    num = wa * o1[...] + wb * o2[...] + wc * o3[...]
    out_ref[...] = (num * (1.0 / (wa + wb + wc))).astype(BF16)


def _dilated_attention(u3, *, tm=512):
    B, L, _ = u3.shape
    parts = []
    for _, dil in B_PATTERNS:
        parts.extend(_dilated_one(u3, dil))
    T = B * L
    W = parts[0].shape[1]
    tm = min(tm, T)
    spec = pl.BlockSpec((tm, W), lambda i: (i, 0))
    return pl.pallas_call(
        _combine_body,
        out_shape=jax.ShapeDtypeStruct((T, W), BF16),
        grid=(T // tm,),
        in_specs=[spec] * 6,
        out_specs=spec,
        compiler_params=_cparams(("parallel",)),
        name="dilated_combine",
    )(*parts)


def _gla_body(*refs, R, backward):
    if backward:
        (q_ref, k_ref, v_ref, low_ref, wup_ref, bias_ref, fwd_ref, cg_ref, gn_ref, o_ref, s_ref) = refs
    else:
        (q_ref, k_ref, v_ref, low_ref, wup_ref, bias_ref, o_ref, s_ref) = refs
    C = C_CHUNK
    n = R // C

    @pl.when(pl.program_id(2) == 0)
    def _():
        s_ref[...] = jnp.zeros(s_ref.shape, F32)

    x = _dot(low_ref[...], wup_ref[...]) + bias_ref[...]
    gl = (jnp.minimum(x, 0.0) - jnp.log(1.0 + jnp.exp(-jnp.abs(x)))) * (1.0 / C_GATE_NORM)

    r64 = lax.broadcasted_iota(jnp.int32, (C, C), 0)
    c64 = lax.broadcasted_iota(jnp.int32, (C, C), 1)
    tri = (c64 >= r64) if backward else (c64 <= r64)
    tri_f = tri.astype(F32)
    lane = lax.broadcasted_iota(jnp.int32, (C, LANE), 1)
    qrow = lax.broadcasted_iota(jnp.int32, (C, LANE), 0)
    key = lane % C
    tri_wide = (key >= qrow) if backward else (key <= qrow)
    head0 = lane < HEAD_DIM
    lane2 = lax.broadcasted_iota(jnp.int32, (C, 2 * LANE), 1)
    vhead0 = lane2 < LANE
    rr = lax.broadcasted_iota(jnp.int32, (LANE, 2 * LANE), 0)
    cc = lax.broadcasted_iota(jnp.int32, (LANE, 2 * LANE), 1)
    blockdiag = (rr < HEAD_DIM) == (cc < LANE)
    eye = (lax.broadcasted_iota(jnp.int32, (LANE, LANE), 0) == lax.broadcasted_iota(jnp.int32, (LANE, LANE), 1))
    zpad = jnp.zeros((C, LANE), F32)
    zpad_v = jnp.zeros((C, 2 * LANE), BF16)

    order = range(n - 1, -1, -1) if backward else range(n)
    for c in order:
        rows = slice(c * C, (c + 1) * C)
        gc = gl[rows]
        cum = jnp.dot(tri_f, gc, preferred_element_type=F32, precision=lax.Precision.HIGHEST)
        last = cum[0:1] if backward else cum[C - 1:C]
        qc = q_ref[rows, :].astype(F32)
        kc = k_ref[rows, :].astype(F32)
        vc = v_ref[rows, :]
        qd = (qc * jnp.exp(cum)).astype(BF16)
        kinv = kc * jnp.exp(-cum)
        kdec = kc * jnp.exp(last - cum)
        zero = jnp.zeros_like(kinv)
        kbd = jnp.concatenate([jnp.where(head0, kinv, zero), jnp.where(head0, zero, kinv)], axis=0).astype(BF16)
        att = jnp.where(tri_wide, _dot_nt(qd, kbd), 0.0).astype(BF16)
        zv = jnp.zeros_like(vc)
        vbd = jnp.concatenate([jnp.where(vhead0, vc, zv), jnp.where(vhead0, zv, vc)], axis=0)
        state = s_ref[...]
        o = _dot(att, vbd) + _dot(qd, state.astype(BF16))
        kdec_t = jnp.concatenate([kdec, zpad], axis=0).T.astype(BF16)
        upd = _dot(kdec_t, jnp.concatenate([vc, zpad_v], axis=0))
        dec_row = jnp.broadcast_to(jnp.exp(last), (LANE, LANE))
        dec_col = jnp.sum(jnp.where(eye, dec_row, 0.0), axis=-1, keepdims=True)
        s_ref[...] = dec_col * state + jnp.where(blockdiag, upd, 0.0)
        if backward:
            tot = o + fwd_ref[rows, :]
            gate = cg_ref[rows, :].astype(F32)
            gate = gate * (1.0 / (1.0 + jnp.exp(-gate)))
            for h in range(2):
                hs = slice(h * LANE, (h + 1) * LANE)
                o_ref[rows, hs] = (_rms(tot[:, hs], gn_ref[...]) * gate[:, hs]).astype(BF16)
        else:
            o_ref[rows, :] = o


def _gla(u3, wup_f, bias_f, wup_b, bias_b, gn, *, R=512):
    B, L, _ = u3.shape
    R = min(R, L)
    nb = L // R
    pairs = C_HEADS // 2
    W = C_HEADS * LANE
    qb, kb = U_CQ // LANE, U_CK // LANE
    vb, gb, lb = U_CV // (2 * LANE), U_CG // (2 * LANE), U_CLOW // (2 * LANE)

    def specs(rev):
        def ri(i):
            return nb - 1 - i if rev else i
        return dict(
            q=pl.BlockSpec((None, R, LANE), lambda b, p, i: (b, ri(i), qb + p)),
            k=pl.BlockSpec((None, R, LANE), lambda b, p, i: (b, ri(i), kb + p)),
            v=pl.BlockSpec((None, R, 2 * LANE), lambda b, p, i: (b, ri(i), vb + p)),
            low=pl.BlockSpec((None, R, 2 * LANE), lambda b, p, i: (b, ri(i), lb)),
            wup=pl.BlockSpec((2 * LANE, LANE), lambda b, p, i: (0, p)),
            bias=pl.BlockSpec((1, LANE), lambda b, p, i: (0, p)),
            out=pl.BlockSpec((None, R, 2 * LANE), lambda b, p, i: (b, ri(i), p)),
            cg=pl.BlockSpec((None, R, 2 * LANE), lambda b, p, i: (b, ri(i), gb + p)),
            gn=pl.BlockSpec((1, LANE), lambda b, p, i: (0, 0)),
        )

    sem = _cparams(("parallel", "parallel", "arbitrary"))
    sf = specs(False)
    fwd = pl.pallas_call(
        functools.partial(_gla_body, R=R, backward=False),
        out_shape=jax.ShapeDtypeStruct((B, L, W), F32),
        grid=(B, pairs, nb),
        in_specs=[sf["q"], sf["k"], sf["v"], sf["low"], sf["wup"], sf["bias"]],
        out_specs=sf["out"],
        scratch_shapes=[pltpu.VMEM((LANE, 2 * LANE), F32)],
        compiler_params=sem,
        name="gla_fwd",
    )(u3, u3, u3, u3, wup_f, bias_f)
    sb = specs(True)
    return pl.pallas_call(
        functools.partial(_gla_body, R=R, backward=True),
        out_shape=jax.ShapeDtypeStruct((B, L, W), BF16),
        grid=(B, pairs, nb),
        in_specs=[sb["q"], sb["k"], sb["v"], sb["low"], sb["wup"], sb["bias"], sb["out"], sb["cg"], sb["gn"]],
        out_specs=sb["out"],
        scratch_shapes=[pltpu.VMEM((LANE, 2 * LANE), F32)],
        compiler_params=sem,
        name="gla_bwd",
    )(u3, u3, u3, u3, wup_b, bias_b, fwd, u3, gn)


def _outproj_body(x_ref, a_ref, b_ref, c_ref, d_ref, wa_ref, wb_ref, wc_ref, wd_ref, o_ref):
    acc = _dot(a_ref[...], wa_ref[...]) + _dot(b_ref[...], wb_ref[...])
    acc = acc + _dot(c_ref[...], wc_ref[...]) + _dot(d_ref[...], wd_ref[...])
    o_ref[...] = x_ref[...] + acc


def _outproj(x, oa, ob, oc, od, wa, wb, wc, wd, *, tm=1024, tn=512):
    T, D = x.shape
    tm = min(tm, T)
    tn = min(tn, D)

    def act(w):
        return pl.BlockSpec((tm, w), lambda i, j: (i, 0))

    def wt(w):
        return pl.BlockSpec((w, tn), lambda i, j: (0, j))

    widths = [oa.shape[1], ob.shape[1], oc.shape[1], od.shape[1]]
    return pl.pallas_call(
        _outproj_body,
        out_shape=jax.ShapeDtypeStruct((T, D), F32),
        grid=(T // tm, D // tn),
        in_specs=[pl.BlockSpec((tm, tn), lambda i, j: (i, j))] + [act(w) for w in widths] + [wt(w) for w in widths],
        out_specs=pl.BlockSpec((tm, tn), lambda i, j: (i, j)),
        compiler_params=_cparams(("parallel", "arbitrary")),
        name="outproj",
    )(x, oa, ob, oc, od, wa, wb, wc, wd)


def _pad_heads(w, n):
    k = w.shape[0]
    return jnp.pad(w.reshape(k, n, HEAD_DIM), ((0, 0), (0, 0), (0, LANE - HEAD_DIM))).reshape(k, n * LANE)


def _prep_w_in(w):
    sizes = (768, 768, 768, 768, 768, 768, 384, 384, 768, 768, 32, 768, 256, 256)
    offs = [0]
    for s in sizes:
        offs.append(offs[-1] + s)
    (a_q, a_k, a_v, b_q, b_k, b_v, c_q, c_k, c_v, c_g, c_low, d_q, d_k, d_v) = [
        w[:, offs[i]:offs[i + 1]] for i in range(len(sizes))]
    low = jnp.pad(c_low, ((0, 0), (0, 2 * LANE - 2 * C_RANK)))
    cols = [a_q * QK_SCALE, a_k, b_q * QK_SCALE, b_k,
            _pad_heads(d_q, D_Q_HEADS), _pad_heads(d_k, D_KV_HEADS),
            a_v, b_v, _pad_heads(d_v, D_KV_HEADS),
            c_q * QK_SCALE, c_k, c_v, c_g, low]
    out = jnp.concatenate(cols, axis=1).astype(BF16)
    assert out.shape[1] == NU
    return out


def _prep_w_out(w):
    wa = w[0:768]
    wb = w[768:1536]
    wc = w[1536:2304]
    wd = w[2304:3072]
    n = w.shape[1]
    wd = jnp.pad(wd.reshape(D_Q_HEADS, HEAD_DIM, n), ((0, 0), (0, LANE - HEAD_DIM), (0, 0))).reshape(D_PAD_W, n)
    return [t.astype(BF16) for t in (wa, wb, wc, wd)]


def _rope_tables(L):
    t = jnp.arange(L, dtype=F32)
    lane = jnp.arange(LANE)
    l64 = lane % HEAD_DIM
    half = ROPE_DIMS // 2
    inv = ROPE_THETA ** (-jnp.arange(0, ROPE_DIMS, 2, dtype=F32) / ROPE_DIMS)
    ang = t[:, None] * inv[None, :]
    ang_l = ang[:, l64 % half]
    in_rot = (l64 < ROPE_DIMS)[None, :]
    c8 = jnp.where(in_rot, jnp.cos(ang_l), 1.0)
    sa8 = jnp.where(((l64 >= half) & (l64 < ROPE_DIMS))[None, :], jnp.sin(ang_l), 0.0)
    sb8 = jnp.where((l64 < half)[None, :], -jnp.sin(ang_l), 0.0)

    q = HEAD_DIM // 4
    inv2 = AXIAL_THETA ** (-jnp.arange(0, HEAD_DIM // 2, 2, dtype=F32) / (HEAD_DIM // 2))
    rows = L // GRID_W
    row_pos = jnp.repeat(jnp.arange(rows, dtype=F32), GRID_W)
    col_pos = jnp.tile(jnp.arange(GRID_W, dtype=F32), rows)
    ang_r = row_pos[:, None] * inv2[None, :]
    ang_c = col_pos[:, None] * inv2[None, :]
    ang_x = jnp.where((l64 < 2 * q)[None, :], ang_r[:, l64 % q], ang_c[:, l64 % q])
    real = (lane < HEAD_DIM)[None, :]
    cx = jnp.where(real, jnp.cos(ang_x), 0.0)
    upper = ((l64 % (2 * q)) >= q)[None, :]
    sax = jnp.where(real & upper, jnp.sin(ang_x), 0.0)
    sbx = jnp.where(real & ~upper, -jnp.sin(ang_x), 0.0)
    return [c8, sa8, sb8, cx, sax, sbx]


def _prep_layer(l, ffn1_norm, ffn1_w_gate, ffn1_w_up, ffn1_w_down, mix_norm, w_in, w_out,
                diff_lambda_q1, diff_lambda_k1, diff_lambda_q2, diff_lambda_k2, diff_out_norm,
                gla_gate_up_f, gla_gate_bias_f, gla_gate_up_b, gla_gate_bias_b, gla_out_norm,
                gqa_q_norm, gqa_k_norm, ffn2_norm, ffn2_w_gate, ffn2_w_up, ffn2_w_down):
    def row(v):
        return v.astype(F32).reshape(1, -1)

    zeros64 = jnp.zeros((HEAD_DIM,), F32)
    gq = jnp.tile(jnp.concatenate([gqa_q_norm[l].astype(F32) * QK_SCALE, zeros64]), D_Q_HEADS)
    gk = jnp.tile(jnp.concatenate([gqa_k_norm[l].astype(F32), zeros64]), D_KV_HEADS)
    wup_f = jnp.zeros((2 * LANE, C_HEADS * HEAD_DIM), F32).at[0:C_RANK].set(gla_gate_up_f[l])
    wup_b = jnp.zeros((2 * LANE, C_HEADS * HEAD_DIM), F32).at[C_RANK:2 * C_RANK].set(gla_gate_up_b[l])
    return dict(
        n1=row(ffn1_norm[l]), wg1=ffn1_w_gate[l].astype(BF16), wu1=ffn1_w_up[l].astype(BF16),
        wd1=ffn1_w_down[l].astype(BF16),
        nmix=row(mix_norm[l]), w_in=_prep_w_in(w_in[l]), w_out=_prep_w_out(w_out[l]),
        gd=jnp.concatenate([gq, gk]).reshape(1, -1),
        lam=jnp.stack([diff_lambda_q1[l], diff_lambda_k1[l], diff_lambda_q2[l], diff_lambda_k2[l]]).astype(F32),
        lam_init=0.8 - 0.6 * math.exp(-0.3 * l),
        gdiff=row(diff_out_norm[l]),
        wup_f=wup_f.astype(BF16), bias_f=row(gla_gate_bias_f[l]),
        wup_b=wup_b.astype(BF16), bias_b=row(gla_gate_bias_b[l]),
        ggla=row(gla_out_norm[l]),
        n2=row(ffn2_norm[l]), wg2=ffn2_w_gate[l].astype(BF16), wu2=ffn2_w_up[l].astype(BF16),
        wd2=ffn2_w_down[l].astype(BF16),
    )


def _trunk(x, layers, final_g):
    B, L, D = x.shape
    xt = x.reshape(B * L, D)
    tabs = _rope_tables(L)
    for l, p in enumerate(layers):
        xt = _ffn(xt, p["n1"], p["wg1"], p["wu1"], p["wd1"], final_g, final=False)
        u = _inproj(xt, p["nmix"], p["w_in"], p["gd"], tabs, L)
        u3 = u.reshape(B, L, NU)
        oa = _diff_attention(u3, p["lam"], p["gdiff"], p["lam_init"]).reshape(B * L, -1)
        ob = _dilated_attention(u3)
        oc = _gla(u3, p["wup_f"], p["bias_f"], p["wup_b"], p["bias_b"], p["ggla"]).reshape(B * L, -1)
        od = _gqa_attention(u3).reshape(B * L, -1)
        xt = _outproj(xt, oa, ob, oc, od, *p["w_out"])
        xt = _ffn(xt, p["n2"], p["wg2"], p["wu2"], p["wd2"], final_g, final=(l == len(layers) - 1))
    return xt.reshape(B, L, D)


def kernel(x_prompt, x_sample, ffn1_norm, ffn1_w_gate, ffn1_w_up, ffn1_w_down, mix_norm, w_in, w_out, diff_lambda_q1, diff_lambda_k1, diff_lambda_q2, diff_lambda_k2, diff_out_norm, gla_gate_up_f, gla_gate_bias_f, gla_gate_up_b, gla_gate_bias_b, gla_out_norm, gqa_q_norm, gqa_k_norm, ffn2_norm, ffn2_w_gate, ffn2_w_up, ffn2_w_down, final_norm):
    depth = w_in.shape[0]
    layers = [_prep_layer(l, ffn1_norm, ffn1_w_gate, ffn1_w_up, ffn1_w_down, mix_norm, w_in, w_out,
                          diff_lambda_q1, diff_lambda_k1, diff_lambda_q2, diff_lambda_k2, diff_out_norm,
                          gla_gate_up_f, gla_gate_bias_f, gla_gate_up_b, gla_gate_bias_b, gla_out_norm,
                          gqa_q_norm, gqa_k_norm, ffn2_norm, ffn2_w_gate, ffn2_w_up, ffn2_w_down)
              for l in range(depth)]
    final_g = final_norm.astype(F32).reshape(1, -1)
    return (_trunk(x_prompt, layers, final_g), _trunk(x_sample, layers, final_g))
```

```python
import functools
import math

import jax
import jax.numpy as jnp
from jax import lax
from jax.experimental import pallas as pl
from jax.experimental.pallas import tpu as pltpu

F32 = jnp.float32
BF16 = jnp.bfloat16

HEAD_DIM = 64
EPS = 1e-6
ROPE_THETA = 500000.0
ROPE_DIMS = HEAD_DIM // 4
AXIAL_THETA = 10000.0
GRID_W = 64
A_HEADS = 6
B_HEADS = 12
B_PATTERNS = ((128, 1), (512, 4), (2048, 16))
B_SIDE = 64
C_HEADS = 6
C_RANK = 16
C_CHUNK = 64
C_GATE_NORM = 16.0
D_Q_HEADS = 12
D_KV_HEADS = 4
D_GROUP = D_Q_HEADS // D_KV_HEADS
QK_SCALE = HEAD_DIM ** -0.5
LOG2E = math.log2(math.e)

LANE = 128
NEG = -1e30

U_AQ = 0
U_AK = 768
U_BQ = 1536
U_BK = 2304
U_DQ = 3072
U_DK = 4608
U_AV = 5120
U_BV = 5888
U_DV = 6656
U_CQ = 7168
U_CK = 7552
U_CV = 7936
U_CG = 8704
U_CLOW = 9472
NU = 9728
ROPE_COLS = U_DQ
AXIAL_COLS = U_AV - U_DQ
D_PAD_W = D_Q_HEADS * LANE


def _cparams(sem, vmem_mb=56):
    return pltpu.CompilerParams(dimension_semantics=sem, vmem_limit_bytes=vmem_mb * 1024 * 1024)


def _dot(a, b):
    return jnp.dot(a, b, preferred_element_type=F32)


def _dot_nt(a, b):
    return lax.dot_general(a, b, (((1,), (1,)), ((), ())), preferred_element_type=F32)


def _rms(x, g):
    ms = jnp.mean(x * x, axis=-1, keepdims=True)
    return x * lax.rsqrt(ms + EPS) * g


def _ffn_body(x_ref, g_ref, wg_ref, wu_ref, wd_ref, fg_ref, o_ref, h_ref, acc_ref, *, final):
    j = pl.program_id(1)
    nj = pl.num_programs(1)

    @pl.when(j == 0)
    def _():
        h_ref[...] = _rms(x_ref[...], g_ref[...]).astype(BF16)

    h = h_ref[...]
    gate = _dot(h, wg_ref[...])
    up = _dot(h, wu_ref[...])
    act = (gate * (1.0 / (1.0 + jnp.exp(-gate))) * up).astype(BF16)
    part = _dot(act, wd_ref[...])

    @pl.when(j == 0)
    def _():
        acc_ref[...] = part

    @pl.when(j > 0)
    def _():
        acc_ref[...] += part

    @pl.when(j == nj - 1)
    def _():
        y = x_ref[...] + 0.5 * acc_ref[...]
        if final:
            y = _rms(y, fg_ref[...])
        o_ref[...] = y


def _ffn(x, g, wg, wu, wd, fg, *, final, tm=512, tf=512):
    T, D = x.shape
    FF = wg.shape[1]
    tm = min(tm, T)
    tf = min(tf, FF)
    assert T % tm == 0 and FF % tf == 0
    return pl.pallas_call(
        functools.partial(_ffn_body, final=final),
        out_shape=jax.ShapeDtypeStruct((T, D), F32),
        grid=(T // tm, FF // tf),
        in_specs=[
            pl.BlockSpec((tm, D), lambda i, j: (i, 0)),
            pl.BlockSpec((1, D), lambda i, j: (0, 0)),
            pl.BlockSpec((D, tf), lambda i, j: (0, j)),
            pl.BlockSpec((D, tf), lambda i, j: (0, j)),
            pl.BlockSpec((tf, D), lambda i, j: (j, 0)),
            pl.BlockSpec((1, D), lambda i, j: (0, 0)),
        ],
        out_specs=pl.BlockSpec((tm, D), lambda i, j: (i, 0)),
        scratch_shapes=[pltpu.VMEM((tm, D), BF16), pltpu.VMEM((tm, D), F32)],
        compiler_params=_cparams(("parallel", "arbitrary")),
        name="ffn",
    )(x, g, wg, wu, wd, fg)


def _rot(x, c, sa, sb, shift):
    return x * c + pltpu.roll(x, shift, 1) * sa + pltpu.roll(x, LANE - shift, 1) * sb


def _inproj_body(x_ref, g_ref, w_ref, qs_ref, gd_ref, c8_ref, sa8_ref, sb8_ref, cx_ref, sax_ref, sbx_ref,
                 o_ref, h_ref, *, n_rope, n_axial, tn):
    j = pl.program_id(1)

    @pl.when(j == 0)
    def _():
        h_ref[...] = _rms(x_ref[...], g_ref[...]).astype(BF16)

    acc = _dot(h_ref[...], w_ref[...])
    chunks = tn // LANE

    @pl.when(j < n_rope)
    def _():
        for c in range(chunks):
            sl = slice(c * LANE, (c + 1) * LANE)
            y = _rot(acc[:, sl], c8_ref[...], sa8_ref[...], sb8_ref[...], ROPE_DIMS // 2)
            o_ref[:, sl] = (y * qs_ref[:, sl]).astype(BF16)

    @pl.when((j >= n_rope) & (j < n_rope + n_axial))
    def _():
        for c in range(chunks):
            sl = slice(c * LANE, (c + 1) * LANE)
            xc = acc[:, sl]
            ms = jnp.sum(xc * xc, axis=-1, keepdims=True) * (1.0 / HEAD_DIM)
            y = xc * lax.rsqrt(ms + EPS) * gd_ref[:, sl]
            o_ref[:, sl] = _rot(y, cx_ref[...], sax_ref[...], sbx_ref[...], HEAD_DIM // 4).astype(BF16)

    @pl.when(j >= n_rope + n_axial)
    def _():
        o_ref[...] = acc.astype(BF16)


def _inproj(x, g, w, qs, gd, tabs, L, *, tm=1024, tn=512):
    T, D = x.shape
    tm = min(tm, L)
    assert T % tm == 0 and L % tm == 0 and NU % tn == 0 and ROPE_COLS % tn == 0 and AXIAL_COLS % tn == 0
    n_rope = ROPE_COLS // tn
    n_axial = AXIAL_COLS // tn
    lt = L // tm
    tab_spec = pl.BlockSpec((tm, LANE), lambda i, j: (i % lt, 0))
    return pl.pallas_call(
        functools.partial(_inproj_body, n_rope=n_rope, n_axial=n_axial, tn=tn),
        out_shape=jax.ShapeDtypeStruct((T, NU), BF16),
        grid=(T // tm, NU // tn),
        in_specs=[
            pl.BlockSpec((tm, D), lambda i, j: (i, 0)),
            pl.BlockSpec((1, D), lambda i, j: (0, 0)),
            pl.BlockSpec((D, tn), lambda i, j: (0, j)),
            pl.BlockSpec((1, tn), lambda i, j: (0, jnp.minimum(j, n_rope - 1))),
            pl.BlockSpec((1, tn), lambda i, j: (0, jnp.clip(j - n_rope, 0, n_axial - 1))),
            tab_spec, tab_spec, tab_spec, tab_spec, tab_spec, tab_spec,
        ],
        out_specs=pl.BlockSpec((tm, tn), lambda i, j: (i, j)),
        scratch_shapes=[pltpu.VMEM((tm, D), BF16)],
        compiler_params=_cparams(("parallel", "arbitrary")),
        name="inproj",
    )(x, g, w, qs, gd, *tabs)


def _flash_init(m_ref, l_ref, acc_ref):
    m_ref[...] = jnp.full(m_ref.shape, NEG, F32)
    l_ref[...] = jnp.zeros(l_ref.shape, F32)
    acc_ref[...] = jnp.zeros(acc_ref.shape, F32)


def _flash_step(qq_ref, k_ref, v_ref, m_ref, l_ref, acc_ref):
    s = _dot_nt(qq_ref[...], k_ref[...])
    chunks = [s[:, c * LANE:(c + 1) * LANE] for c in range(s.shape[1] // LANE)]
    mx = chunks[0]
    for sc in chunks[1:]:
        mx = jnp.maximum(mx, sc)
    m_prev = m_ref[...]
    m_new = jnp.maximum(m_prev, jnp.max(mx, axis=-1, keepdims=True))
    alpha = jnp.exp2(m_prev - m_new)
    ps = [jnp.exp2(sc - m_new) for sc in chunks]
    lsum = ps[0]
    for pc in ps[1:]:
        lsum = lsum + pc
    p = jnp.concatenate([pc.astype(BF16) for pc in ps], axis=1)
    l_ref[...] = alpha * l_ref[...] + lsum
    acc_ref[...] = alpha * acc_ref[...] + _dot(p, v_ref[...])
    m_ref[...] = m_new


def _flash_out(l_ref, acc_ref):
    return acc_ref[...] * (1.0 / jnp.sum(l_ref[...], axis=-1, keepdims=True))


def _diff_body(q_ref, k_ref, v_ref, lam_ref, gn_ref, o_ref, qq_ref, m_ref, l_ref, acc_ref, *, tq, lam_init):
    ki = pl.program_id(3)

    @pl.when(ki == 0)
    def _():
        q = q_ref[...]
        lane = lax.broadcasted_iota(jnp.int32, q.shape, 1)
        zero = jnp.zeros_like(q)
        qq_ref[0:tq, :] = jnp.where(lane < HEAD_DIM, q, zero)
        qq_ref[tq:2 * tq, :] = jnp.where(lane >= HEAD_DIM, q, zero)
        _flash_init(m_ref, l_ref, acc_ref)

    _flash_step(qq_ref, k_ref, v_ref, m_ref, l_ref, acc_ref)

    @pl.when(ki == pl.num_programs(3) - 1)
    def _():
        o = _flash_out(l_ref, acc_ref)
        lv = lam_ref[...]
        lam = (jnp.exp(jnp.sum(lv[0:1] * lv[1:2], axis=-1, keepdims=True))
               - jnp.exp(jnp.sum(lv[2:3] * lv[3:4], axis=-1, keepdims=True)) + lam_init)
        a = o[0:tq] - lam * o[tq:2 * tq]
        o_ref[...] = (_rms(a, gn_ref[...]) * (1.0 - lam_init)).astype(BF16)


def _diff_attention(u3, lam_vecs, gn, lam_init, *, tq=512, tk=1024):
    B, L, _ = u3.shape
    tq = min(tq, L)
    tk = min(tk, L)
    qb, kb, vb = U_AQ // LANE, U_AK // LANE, U_AV // LANE
    return pl.pallas_call(
        functools.partial(_diff_body, tq=tq, lam_init=lam_init),
        out_shape=jax.ShapeDtypeStruct((B, L, A_HEADS * LANE), BF16),
        grid=(B, A_HEADS, L // tq, L // tk),
        in_specs=[
            pl.BlockSpec((None, tq, LANE), lambda b, h, qi, ki: (b, qi, qb + h)),
            pl.BlockSpec((None, tk, LANE), lambda b, h, qi, ki: (b, ki, kb + h)),
            pl.BlockSpec((None, tk, LANE), lambda b, h, qi, ki: (b, ki, vb + h)),
            pl.BlockSpec((4, HEAD_DIM), lambda b, h, qi, ki: (0, 0)),
            pl.BlockSpec((1, LANE), lambda b, h, qi, ki: (0, 0)),
        ],
        out_specs=pl.BlockSpec((None, tq, LANE), lambda b, h, qi, ki: (b, qi, h)),
        scratch_shapes=[
            pltpu.VMEM((2 * tq, LANE), BF16),
            pltpu.VMEM((2 * tq, LANE), F32),
            pltpu.VMEM((2 * tq, LANE), F32),
            pltpu.VMEM((2 * tq, LANE), F32),
        ],
        compiler_params=_cparams(("parallel", "parallel", "parallel", "arbitrary")),
        name="diff_attn",
    )(u3, u3, u3, lam_vecs, gn)


def _gqa_body(q_ref, k_ref, v_ref, o_ref, qq_ref, m_ref, l_ref, acc_ref, *, tq):
    ki = pl.program_id(3)

    @pl.when(ki == 0)
    def _():
        for j in range(D_GROUP):
            qq_ref[j * tq:(j + 1) * tq, :] = q_ref[:, j * LANE:(j + 1) * LANE]
        _flash_init(m_ref, l_ref, acc_ref)

    _flash_step(qq_ref, k_ref, v_ref, m_ref, l_ref, acc_ref)

    @pl.when(ki == pl.num_programs(3) - 1)
    def _():
        o = _flash_out(l_ref, acc_ref).astype(BF16)
        for j in range(D_GROUP):
            o_ref[:, j * LANE:(j + 1) * LANE] = o[j * tq:(j + 1) * tq]


def _gqa_attention(u3, *, tq=512, tk=1024):
    B, L, _ = u3.shape
    tq = min(tq, L)
    tk = min(tk, L)
    gw = D_GROUP * LANE
    qb, kb, vb = U_DQ // gw, U_DK // LANE, U_DV // LANE
    return pl.pallas_call(
        functools.partial(_gqa_body, tq=tq),
        out_shape=jax.ShapeDtypeStruct((B, L, D_PAD_W), BF16),
        grid=(B, D_KV_HEADS, L // tq, L // tk),
        in_specs=[
            pl.BlockSpec((None, tq, gw), lambda b, g, qi, ki: (b, qi, qb + g)),
            pl.BlockSpec((None, tk, LANE), lambda b, g, qi, ki: (b, ki, kb + g)),
            pl.BlockSpec((None, tk, LANE), lambda b, g, qi, ki: (b, ki, vb + g)),
        ],
        out_specs=pl.BlockSpec((None, tq, gw), lambda b, g, qi, ki: (b, qi, g)),
        scratch_shapes=[
            pltpu.VMEM((D_GROUP * tq, LANE), BF16),
            pltpu.VMEM((D_GROUP * tq, LANE), F32),
            pltpu.VMEM((D_GROUP * tq, LANE), F32),
            pltpu.VMEM((D_GROUP * tq, LANE), F32),
        ],
        compiler_params=_cparams(("parallel", "parallel", "parallel", "arbitrary")),
        name="gqa_attn",
    )(u3, u3, u3)


def _dil_body(q_ref, k_ref, kp_ref, kn_ref, v_ref, vp_ref, vn_ref, o_ref, lse_ref, kbuf, vbuf, *, S, N):
    i = pl.program_id(3)
    H = B_SIDE
    kbuf[0:H, :] = kp_ref[...]
    kbuf[H:H + S, :] = k_ref[...]
    kbuf[H + S:H + S + H, :] = kn_ref[...]
    vbuf[0:H, :] = vp_ref[...]
    vbuf[H:H + S, :] = v_ref[...]
    vbuf[H + S:H + S + H, :] = vn_ref[...]

    tq = LANE
    tw = tq + 2 * H
    lane_q = lax.broadcasted_iota(jnp.int32, (tq, LANE), 1)
    row = lax.broadcasted_iota(jnp.int32, (2 * tq, tw), 0)
    col = lax.broadcasted_iota(jnp.int32, (2 * tq, tw), 1)
    rel = col - H - (row % tq)
    band = jnp.abs(rel) <= H
    for j in range(S // tq):
        q = q_ref[j * tq:(j + 1) * tq, :]
        zero = jnp.zeros_like(q)
        qq = jnp.concatenate([jnp.where(lane_q < HEAD_DIM, q, zero),
                              jnp.where(lane_q >= HEAD_DIM, q, zero)], axis=0)
        kw = kbuf[j * tq:j * tq + tw, :]
        vw = vbuf[j * tq:j * tq + tw, :]
        s = _dot_nt(qq, kw)
        kpos = i * S + j * tq - H + col
        valid = band & (kpos >= 0) & (kpos < N)
        s = jnp.where(valid, s, NEG)
        m = jnp.max(s, axis=-1, keepdims=True)
        e = jnp.exp2(s - m)
        l = jnp.sum(e, axis=-1, keepdims=True)
        pv = _dot(e.astype(BF16), vw) * (1.0 / l)
        lse = m + jnp.log2(l)
        sel = lane_q < HEAD_DIM
        o_ref[j * tq:(j + 1) * tq, :] = jnp.where(sel, pv[0:tq], pv[tq:2 * tq])
        lse_ref[j * tq:(j + 1) * tq, :] = jnp.where(sel, jnp.broadcast_to(lse[0:tq], (tq, LANE)),
                                                    jnp.broadcast_to(lse[tq:2 * tq], (tq, LANE)))


def _dilated_one(u3, dil, *, smax=512):
    B, L, _ = u3.shape
    N = L // dil
    S = min(smax, N)
    H = B_SIDE
    uv = u3.reshape(B, N, dil * NU)
    nub = NU // LANE
    ob = (B_HEADS // 2)
    qb, kb, vb = U_BQ // LANE, U_BK // LANE, U_BV // LANE
    sh = S // H
    nh = N // H

    def main(base):
        return pl.BlockSpec((None, S, LANE), lambda b, hp, r, i: (b, i, r * nub + base + hp))

    def prev(base):
        return pl.BlockSpec((None, H, LANE), lambda b, hp, r, i: (b, jnp.maximum(i * sh - 1, 0), r * nub + base + hp))

    def nxt(base):
        return pl.BlockSpec((None, H, LANE), lambda b, hp, r, i: (b, jnp.minimum((i + 1) * sh, nh - 1), r * nub + base + hp))

    out_spec = pl.BlockSpec((None, S, LANE), lambda b, hp, r, i: (b, i, r * ob + hp))
    o, lse = pl.pallas_call(
        functools.partial(_dil_body, S=S, N=N),
        out_shape=[jax.ShapeDtypeStruct((B, N, dil * ob * LANE), F32)] * 2,
        grid=(B, ob, dil, N // S),
        in_specs=[main(qb), main(kb), prev(kb), nxt(kb), main(vb), prev(vb), nxt(vb)],
        out_specs=[out_spec, out_spec],
        scratch_shapes=[pltpu.VMEM((S + 2 * H, LANE), BF16), pltpu.VMEM((S + 2 * H, LANE), BF16)],
        compiler_params=_cparams(("parallel", "parallel", "parallel", "arbitrary")),
        name=f"dilated_attn_d{dil}",
    )(uv, uv, uv, uv, uv, uv, uv)
    W = ob * LANE
    return o.reshape(B * L, W), lse.reshape(B * L, W)


def _combine_body(o1, l1, o2, l2, o3, l3, out_ref):
    a, b, c = l1[...], l2[...], l3[...]
    m = jnp.maximum(jnp.maximum(a, b), c)
    wa, wb, wc = jnp.exp2(a - m), jnp.exp2(b - m), jnp.exp2(c - m)
    num = wa * o1[...] + wb * o2[...] + wc * o3[...]
    out_ref[...] = (num * (1.0 / (wa + wb + wc))).astype(BF16)


def _dilated_attention(u3, *, tm=512):
    B, L, _ = u3.shape
    parts = []
    for _, dil in B_PATTERNS:
        parts.extend(_dilated_one(u3, dil))
    T = B * L
    W = parts[0].shape[1]
    tm = min(tm, T)
    spec = pl.BlockSpec((tm, W), lambda i: (i, 0))
    return pl.pallas_call(
        _combine_body,
        out_shape=jax.ShapeDtypeStruct((T, W), BF16),
        grid=(T // tm,),
        in_specs=[spec] * 6,
        out_specs=spec,
        compiler_params=_cparams(("parallel",)),
        name="dilated_combine",
    )(*parts)


def _gla_body(*refs, R, backward):
    if backward:
        (q_ref, k_ref, v_ref, low_ref, wup_ref, bias_ref, fwd_ref, cg_ref, gn_ref, o_ref, s_ref) = refs
    else:
        (q_ref, k_ref, v_ref, low_ref, wup_ref, bias_ref, o_ref, s_ref) = refs
    C = C_CHUNK
    n = R // C

    @pl.when(pl.program_id(2) == 0)
    def _():
        s_ref[...] = jnp.zeros(s_ref.shape, F32)

    x = _dot(low_ref[...], wup_ref[...]) + bias_ref[...]
    gl = (jnp.minimum(x, 0.0) - jnp.log(1.0 + jnp.exp(-jnp.abs(x)))) * (1.0 / C_GATE_NORM)

    r64 = lax.broadcasted_iota(jnp.int32, (C, C), 0)
    c64 = lax.broadcasted_iota(jnp.int32, (C, C), 1)
    tri = (c64 >= r64) if backward else (c64 <= r64)
    tri_f = tri.astype(F32)
    lane = lax.broadcasted_iota(jnp.int32, (C, LANE), 1)
    qrow = lax.broadcasted_iota(jnp.int32, (C, LANE), 0)
    key = lane % C
    tri_wide = (key >= qrow) if backward else (key <= qrow)
    head0 = lane < HEAD_DIM
    lane2 = lax.broadcasted_iota(jnp.int32, (C, 2 * LANE), 1)
    vhead0 = lane2 < LANE
    rr = lax.broadcasted_iota(jnp.int32, (LANE, 2 * LANE), 0)
    cc = lax.broadcasted_iota(jnp.int32, (LANE, 2 * LANE), 1)
    blockdiag = (rr < HEAD_DIM) == (cc < LANE)
    eye = (lax.broadcasted_iota(jnp.int32, (LANE, LANE), 0) == lax.broadcasted_iota(jnp.int32, (LANE, LANE), 1))
    zpad = jnp.zeros((C, LANE), F32)
    zpad_v = jnp.zeros((C, 2 * LANE), BF16)

    order = range(n - 1, -1, -1) if backward else range(n)
    for c in order:
        rows = slice(c * C, (c + 1) * C)
        gc = gl[rows]
        cum = jnp.dot(tri_f, gc, preferred_element_type=F32, precision=lax.Precision.HIGHEST)
        last = cum[0:1] if backward else cum[C - 1:C]
        qc = q_ref[rows, :].astype(F32)
        kc = k_ref[rows, :].astype(F32)
        vc = v_ref[rows, :]
        qd = (qc * jnp.exp(cum)).astype(BF16)
        kinv = kc * jnp.exp(-cum)
        kdec = kc * jnp.exp(last - cum)
        zero = jnp.zeros_like(kinv)
        kbd = jnp.concatenate([jnp.where(head0, kinv, zero), jnp.where(head0, zero, kinv)], axis=0).astype(BF16)
        att = jnp.where(tri_wide, _dot_nt(qd, kbd), 0.0).astype(BF16)
        zv = jnp.zeros_like(vc)
        vbd = jnp.concatenate([jnp.where(vhead0, vc, zv), jnp.where(vhead0, zv, vc)], axis=0)
        state = s_ref[...]
        o = _dot(att, vbd) + _dot(qd, state.astype(BF16))
        kdec_t = jnp.concatenate([kdec, zpad], axis=0).T.astype(BF16)
        upd = _dot(kdec_t, jnp.concatenate([vc, zpad_v], axis=0))
        dec_row = jnp.broadcast_to(jnp.exp(last), (LANE, LANE))
        dec_col = jnp.sum(jnp.where(eye, dec_row, 0.0), axis=-1, keepdims=True)
        s_ref[...] = dec_col * state + jnp.where(blockdiag, upd, 0.0)
        if backward:
            tot = o + fwd_ref[rows, :]
            gate = cg_ref[rows, :].astype(F32)
            gate = gate * (1.0 / (1.0 + jnp.exp(-gate)))
            for h in range(2):
                hs = slice(h * LANE, (h + 1) * LANE)
                o_ref[rows, hs] = (_rms(tot[:, hs], gn_ref[...]) * gate[:, hs]).astype(BF16)
        else:
            o_ref[rows, :] = o


def _gla(u3, wup_f, bias_f, wup_b, bias_b, gn, *, R=512):
    B, L, _ = u3.shape
    R = min(R, L)
    nb = L // R
    pairs = C_HEADS // 2
    W = C_HEADS * LANE
    qb, kb = U_CQ // LANE, U_CK // LANE
    vb, gb, lb = U_CV // (2 * LANE), U_CG // (2 * LANE), U_CLOW // (2 * LANE)

    def specs(rev):
        def ri(i):
            return nb - 1 - i if rev else i
        return dict(
            q=pl.BlockSpec((None, R, LANE), lambda b, p, i: (b, ri(i), qb + p)),
            k=pl.BlockSpec((None, R, LANE), lambda b, p, i: (b, ri(i), kb + p)),
            v=pl.BlockSpec((None, R, 2 * LANE), lambda b, p, i: (b, ri(i), vb + p)),
            low=pl.BlockSpec((None, R, 2 * LANE), lambda b, p, i: (b, ri(i), lb)),
            wup=pl.BlockSpec((2 * LANE, LANE), lambda b, p, i: (0, p)),
            bias=pl.BlockSpec((1, LANE), lambda b, p, i: (0, p)),
            out=pl.BlockSpec((None, R, 2 * LANE), lambda b, p, i: (b, ri(i), p)),
            cg=pl.BlockSpec((None, R, 2 * LANE), lambda b, p, i: (b, ri(i), gb + p)),
            gn=pl.BlockSpec((1, LANE), lambda b, p, i: (0, 0)),
        )

    sem = _cparams(("parallel", "parallel", "arbitrary"))
    sf = specs(False)
    fwd = pl.pallas_call(
        functools.partial(_gla_body, R=R, backward=False),
        out_shape=jax.ShapeDtypeStruct((B, L, W), F32),
        grid=(B, pairs, nb),
        in_specs=[sf["q"], sf["k"], sf["v"], sf["low"], sf["wup"], sf["bias"]],
        out_specs=sf["out"],
        scratch_shapes=[pltpu.VMEM((LANE, 2 * LANE), F32)],
        compiler_params=sem,
        name="gla_fwd",
    )(u3, u3, u3, u3, wup_f, bias_f)
    sb = specs(True)
    return pl.pallas_call(
        functools.partial(_gla_body, R=R, backward=True),
        out_shape=jax.ShapeDtypeStruct((B, L, W), BF16),
        grid=(B, pairs, nb),
        in_specs=[sb["q"], sb["k"], sb["v"], sb["low"], sb["wup"], sb["bias"], sb["out"], sb["cg"], sb["gn"]],
        out_specs=sb["out"],
        scratch_shapes=[pltpu.VMEM((LANE, 2 * LANE), F32)],
        compiler_params=sem,
        name="gla_bwd",
    )(u3, u3, u3, u3, wup_b, bias_b, fwd, u3, gn)


def _outproj_body(x_ref, a_ref, b_ref, c_ref, d_ref, wa_ref, wb_ref, wc_ref, wd_ref, o_ref):
    acc = _dot(a_ref[...], wa_ref[...]) + _dot(b_ref[...], wb_ref[...])
    acc = acc + _dot(c_ref[...], wc_ref[...]) + _dot(d_ref[...], wd_ref[...])
    o_ref[...] = x_ref[...] + acc


def _outproj(x, oa, ob, oc, od, wa, wb, wc, wd, *, tm=1024, tn=512):
    T, D = x.shape
    tm = min(tm, T)
    tn = min(tn, D)

    def act(w):
        return pl.BlockSpec((tm, w), lambda i, j: (i, 0))

    def wt(w):
        return pl.BlockSpec((w, tn), lambda i, j: (0, j))

    widths = [oa.shape[1], ob.shape[1], oc.shape[1], od.shape[1]]
    return pl.pallas_call(
        _outproj_body,
        out_shape=jax.ShapeDtypeStruct((T, D), F32),
        grid=(T // tm, D // tn),
        in_specs=[pl.BlockSpec((tm, tn), lambda i, j: (i, j))] + [act(w) for w in widths] + [wt(w) for w in widths],
        out_specs=pl.BlockSpec((tm, tn), lambda i, j: (i, j)),
        compiler_params=_cparams(("parallel", "arbitrary")),
        name="outproj",
    )(x, oa, ob, oc, od, wa, wb, wc, wd)


def _pad_heads(w, n):
    k = w.shape[0]
    return jnp.pad(w.reshape(k, n, HEAD_DIM), ((0, 0), (0, 0), (0, LANE - HEAD_DIM))).reshape(k, n * LANE)


def _prep_w_in(w):
    sizes = (768, 768, 768, 768, 768, 768, 384, 384, 768, 768, 32, 768, 256, 256)
    offs = [0]
    for s in sizes:
        offs.append(offs[-1] + s)
    (a_q, a_k, a_v, b_q, b_k, b_v, c_q, c_k, c_v, c_g, c_low, d_q, d_k, d_v) = [
        w[:, offs[i]:offs[i + 1]] for i in range(len(sizes))]
    low = jnp.pad(c_low, ((0, 0), (0, 2 * LANE - 2 * C_RANK)))
    cols = [a_q, a_k, b_q, b_k,
            _pad_heads(d_q, D_Q_HEADS), _pad_heads(d_k, D_KV_HEADS),
            a_v, b_v, _pad_heads(d_v, D_KV_HEADS),
            c_q * QK_SCALE, c_k, c_v, c_g, low]
    out = jnp.concatenate(cols, axis=1).astype(BF16)
    assert out.shape[1] == NU
    return out


def _prep_w_out(w):
    wa = w[0:768]
    wb = w[768:1536]
    wc = w[1536:2304]
    wd = w[2304:3072]
    n = w.shape[1]
    wd = jnp.pad(wd.reshape(D_Q_HEADS, HEAD_DIM, n), ((0, 0), (0, LANE - HEAD_DIM), (0, 0))).reshape(D_PAD_W, n)
    return [t.astype(BF16) for t in (wa, wb, wc, wd)]


def _rope_col_scale():
    col = jnp.arange(ROPE_COLS)
    is_q = ((col >= U_AQ) & (col < U_AK)) | ((col >= U_BQ) & (col < U_BK))
    return jnp.where(is_q, QK_SCALE * LOG2E, 1.0).astype(F32).reshape(1, -1)


def _rope_tables(L):
    t = jnp.arange(L, dtype=F32)
    lane = jnp.arange(LANE)
    l64 = lane % HEAD_DIM
    half = ROPE_DIMS // 2
    inv = ROPE_THETA ** (-jnp.arange(0, ROPE_DIMS, 2, dtype=F32) / ROPE_DIMS)
    ang = t[:, None] * inv[None, :]
    ang_l = ang[:, l64 % half]
    in_rot = (l64 < ROPE_DIMS)[None, :]
    c8 = jnp.where(in_rot, jnp.cos(ang_l), 1.0)
    sa8 = jnp.where(((l64 >= half) & (l64 < ROPE_DIMS))[None, :], jnp.sin(ang_l), 0.0)
    sb8 = jnp.where((l64 < half)[None, :], -jnp.sin(ang_l), 0.0)

    q = HEAD_DIM // 4
    inv2 = AXIAL_THETA ** (-jnp.arange(0, HEAD_DIM // 2, 2, dtype=F32) / (HEAD_DIM // 2))
    rows = L // GRID_W
    row_pos = jnp.repeat(jnp.arange(rows, dtype=F32), GRID_W)
    col_pos = jnp.tile(jnp.arange(GRID_W, dtype=F32), rows)
    ang_r = row_pos[:, None] * inv2[None, :]
    ang_c = col_pos[:, None] * inv2[None, :]
    ang_x = jnp.where((l64 < 2 * q)[None, :], ang_r[:, l64 % q], ang_c[:, l64 % q])
    real = (lane < HEAD_DIM)[None, :]
    cx = jnp.where(real, jnp.cos(ang_x), 0.0)
    upper = ((l64 % (2 * q)) >= q)[None, :]
    sax = jnp.where(real & upper, jnp.sin(ang_x), 0.0)
    sbx = jnp.where(real & ~upper, -jnp.sin(ang_x), 0.0)
    return [c8, sa8, sb8, cx, sax, sbx]


def _prep_layer(l, ffn1_norm, ffn1_w_gate, ffn1_w_up, ffn1_w_down, mix_norm, w_in, w_out,
                diff_lambda_q1, diff_lambda_k1, diff_lambda_q2, diff_lambda_k2, diff_out_norm,
                gla_gate_up_f, gla_gate_bias_f, gla_gate_up_b, gla_gate_bias_b, gla_out_norm,
                gqa_q_norm, gqa_k_norm, ffn2_norm, ffn2_w_gate, ffn2_w_up, ffn2_w_down):
    def row(v):
        return v.astype(F32).reshape(1, -1)

    zeros64 = jnp.zeros((HEAD_DIM,), F32)
    gq = jnp.tile(jnp.concatenate([gqa_q_norm[l].astype(F32) * (QK_SCALE * LOG2E), zeros64]), D_Q_HEADS)
    gk = jnp.tile(jnp.concatenate([gqa_k_norm[l].astype(F32), zeros64]), D_KV_HEADS)
    wup_f = jnp.zeros((2 * LANE, C_HEADS * HEAD_DIM), F32).at[0:C_RANK].set(gla_gate_up_f[l])
    wup_b = jnp.zeros((2 * LANE, C_HEADS * HEAD_DIM), F32).at[C_RANK:2 * C_RANK].set(gla_gate_up_b[l])
    return dict(
        n1=row(ffn1_norm[l]), wg1=ffn1_w_gate[l].astype(BF16), wu1=ffn1_w_up[l].astype(BF16),
        wd1=ffn1_w_down[l].astype(BF16),
        nmix=row(mix_norm[l]), w_in=_prep_w_in(w_in[l]), w_out=_prep_w_out(w_out[l]),
        gd=jnp.concatenate([gq, gk]).reshape(1, -1),
        lam=jnp.stack([diff_lambda_q1[l], diff_lambda_k1[l], diff_lambda_q2[l], diff_lambda_k2[l]]).astype(F32),
        lam_init=0.8 - 0.6 * math.exp(-0.3 * l),
        gdiff=row(diff_out_norm[l]),
        wup_f=wup_f.astype(BF16), bias_f=row(gla_gate_bias_f[l]),
        wup_b=wup_b.astype(BF16), bias_b=row(gla_gate_bias_b[l]),
        ggla=row(gla_out_norm[l]),
        n2=row(ffn2_norm[l]), wg2=ffn2_w_gate[l].astype(BF16), wu2=ffn2_w_up[l].astype(BF16),
        wd2=ffn2_w_down[l].astype(BF16),
    )


def _trunk(x, layers, final_g):
    B, L, D = x.shape
    xt = x.reshape(B * L, D)
    tabs = _rope_tables(L)
    qscale = _rope_col_scale()
    for l, p in enumerate(layers):
        xt = _ffn(xt, p["n1"], p["wg1"], p["wu1"], p["wd1"], final_g, final=False)
        u = _inproj(xt, p["nmix"], p["w_in"], qscale, p["gd"], tabs, L)
        u3 = u.reshape(B, L, NU)
        oa = _diff_attention(u3, p["lam"], p["gdiff"], p["lam_init"]).reshape(B * L, -1)
        ob = _dilated_attention(u3)
        oc = _gla(u3, p["wup_f"], p["bias_f"], p["wup_b"], p["bias_b"], p["ggla"]).reshape(B * L, -1)
        od = _gqa_attention(u3).reshape(B * L, -1)
        xt = _outproj(xt, oa, ob, oc, od, *p["w_out"])
        xt = _ffn(xt, p["n2"], p["wg2"], p["wu2"], p["wd2"], final_g, final=(l == len(layers) - 1))
    return xt.reshape(B, L, D)


def kernel(x_prompt, x_sample, ffn1_norm, ffn1_w_gate, ffn1_w_up, ffn1_w_down, mix_norm, w_in, w_out, diff_lambda_q1, diff_lambda_k1, diff_lambda_q2, diff_lambda_k2, diff_out_norm, gla_gate_up_f, gla_gate_bias_f, gla_gate_up_b, gla_gate_bias_b, gla_out_norm, gqa_q_norm, gqa_k_norm, ffn2_norm, ffn2_w_gate, ffn2_w_up, ffn2_w_down, final_norm):
    depth = w_in.shape[0]
    layers = [_prep_layer(l, ffn1_norm, ffn1_w_gate, ffn1_w_up, ffn1_w_down, mix_norm, w_in, w_out,
                          diff_lambda_q1, diff_lambda_k1, diff_lambda_q2, diff_lambda_k2, diff_out_norm,
                          gla_gate_up_f, gla_gate_bias_f, gla_gate_up_b, gla_gate_bias_b, gla_out_norm,
                          gqa_q_norm, gqa_k_norm, ffn2_norm, ffn2_w_gate, ffn2_w_up, ffn2_w_down)
              for l in range(depth)]
    final_g = final_norm.astype(F32).reshape(1, -1)
    return (_trunk(x_prompt, layers, final_g), _trunk(x_sample, layers, final_g))
```

```python
import functools
import math

import jax
import jax.numpy as jnp
from jax import lax
from jax.experimental import pallas as pl
from jax.experimental.pallas import tpu as pltpu

F32 = jnp.float32
BF16 = jnp.bfloat16

HEAD_DIM = 64
EPS = 1e-6
ROPE_THETA = 500000.0
ROPE_DIMS = HEAD_DIM // 4
AXIAL_THETA = 10000.0
GRID_W = 64
A_HEADS = 6
B_HEADS = 12
B_PATTERNS = ((128, 1), (512, 4), (2048, 16))
B_SIDE = 64
C_HEADS = 6
C_RANK = 16
C_CHUNK = 64
C_GATE_NORM = 16.0
D_Q_HEADS = 12
D_KV_HEADS = 4
D_GROUP = D_Q_HEADS // D_KV_HEADS
QK_SCALE = HEAD_DIM ** -0.5
LOG2E = math.log2(math.e)

LANE = 128
NEG = -1e30

U_BQ = 0
U_BK = 768
U_AQ = 1536
U_AK = 2304
U_DQ = 3072
U_DK = 4608
U_BV = 5120
U_AV = 5888
U_DV = 6656
U_CQ = 7168
U_CK = 7552
U_CV = 7936
U_CG = 8704
U_CLOW = 9472
NU = 9728
ROPE_COLS = U_DQ
AXIAL_COLS = U_BV - U_DQ
D_PAD_W = D_Q_HEADS * LANE
UB_Q = 0
UB_K = 768
UB_V = 1536
UB_QK_COLS = U_AQ - U_BQ
UB_V_COLS = 1024
UB_W = UB_QK_COLS + UB_V_COLS


def _cparams(sem, vmem_mb=56):
    return pltpu.CompilerParams(dimension_semantics=sem, vmem_limit_bytes=vmem_mb * 1024 * 1024)


def _dot(a, b):
    return jnp.dot(a, b, preferred_element_type=F32)


def _dot_nt(a, b):
    return lax.dot_general(a, b, (((1,), (1,)), ((), ())), preferred_element_type=F32)


def _rms(x, g):
    ms = jnp.mean(x * x, axis=-1, keepdims=True)
    return x * lax.rsqrt(ms + EPS) * g


def _ffn_body(x_ref, g_ref, wg_ref, wu_ref, wd_ref, fg_ref, o_ref, h_ref, acc_ref, *, final):
    j = pl.program_id(1)
    nj = pl.num_programs(1)

    @pl.when(j == 0)
    def _():
        h_ref[...] = _rms(x_ref[...], g_ref[...]).astype(BF16)
        acc_ref[...] = jnp.zeros(acc_ref.shape, F32)

    h = h_ref[...]
    gate = _dot(h, wg_ref[...])
    up = _dot(h, wu_ref[...])
    act = (gate * (1.0 / (1.0 + jnp.exp(-gate))) * up).astype(BF16)
    acc_ref[...] += _dot(act, wd_ref[...])

    @pl.when(j == nj - 1)
    def _():
        y = x_ref[...] + 0.5 * acc_ref[...]
        if final:
            y = _rms(y, fg_ref[...])
        o_ref[...] = y


def _ffn(x, g, wg, wu, wd, fg, *, final, tm=512, tf=512):
    T, D = x.shape
    FF = wg.shape[1]
    tm = min(tm, T)
    tf = min(tf, FF)
    assert T % tm == 0 and FF % tf == 0
    return pl.pallas_call(
        functools.partial(_ffn_body, final=final),
        out_shape=jax.ShapeDtypeStruct((T, D), F32),
        grid=(T // tm, FF // tf),
        in_specs=[
            pl.BlockSpec((tm, D), lambda i, j: (i, 0)),
            pl.BlockSpec((1, D), lambda i, j: (0, 0)),
            pl.BlockSpec((D, tf), lambda i, j: (0, j)),
            pl.BlockSpec((D, tf), lambda i, j: (0, j)),
            pl.BlockSpec((tf, D), lambda i, j: (j, 0)),
            pl.BlockSpec((1, D), lambda i, j: (0, 0)),
        ],
        out_specs=pl.BlockSpec((tm, D), lambda i, j: (i, 0)),
        scratch_shapes=[pltpu.VMEM((tm, D), BF16), pltpu.VMEM((tm, D), F32)],
        compiler_params=_cparams(("parallel", "arbitrary")),
        name="ffn",
    )(x, g, wg, wu, wd, fg)


def _rot(x, c, sa, sb, shift):
    return x * c + pltpu.roll(x, shift, 1) * sa + pltpu.roll(x, LANE - shift, 1) * sb


def _inproj_body(x_ref, g_ref, w_ref, qs_ref, gd_ref, c8_ref, sa8_ref, sb8_ref, cx_ref, sax_ref, sbx_ref,
                 o_ref, ob_ref, h_ref, *, n_rope, n_axial, tn):
    j = pl.program_id(1)
    nb_qk = UB_QK_COLS // tn
    jb_v = U_BV // tn
    nb_v = UB_V_COLS // tn

    @pl.when(j == 0)
    def _():
        h_ref[...] = _rms(x_ref[...], g_ref[...]).astype(BF16)

    acc = _dot(h_ref[...], w_ref[...])
    chunks = tn // LANE

    @pl.when(j < n_rope)
    def _():
        ys = []
        for c in range(chunks):
            sl = slice(c * LANE, (c + 1) * LANE)
            y = _rot(acc[:, sl], c8_ref[...], sa8_ref[...], sb8_ref[...], ROPE_DIMS // 2)
            ys.append(y * qs_ref[:, sl])
        y = jnp.concatenate(ys, axis=1)
        o_ref[...] = y.astype(BF16)

        @pl.when(j < nb_qk)
        def _():
            ob_ref[...] = y

    @pl.when((j >= jb_v) & (j < jb_v + nb_v))
    def _():
        ob_ref[...] = acc

    @pl.when((j >= n_rope) & (j < n_rope + n_axial))
    def _():
        for c in range(chunks):
            sl = slice(c * LANE, (c + 1) * LANE)
            xc = acc[:, sl]
            ms = jnp.sum(xc * xc, axis=-1, keepdims=True) * (1.0 / HEAD_DIM)
            y = xc * lax.rsqrt(ms + EPS) * gd_ref[:, sl]
            o_ref[:, sl] = _rot(y, cx_ref[...], sax_ref[...], sbx_ref[...], HEAD_DIM // 4).astype(BF16)

    @pl.when(j >= n_rope + n_axial)
    def _():
        o_ref[...] = acc.astype(BF16)


def _inproj(x, g, w, qs, gd, tabs, L, *, tm=1024, tn=512):
    T, D = x.shape
    tm = min(tm, L)
    assert T % tm == 0 and L % tm == 0 and NU % tn == 0 and ROPE_COLS % tn == 0 and AXIAL_COLS % tn == 0
    assert U_BQ == 0 and UB_QK_COLS % tn == 0 and U_BV % tn == 0 and UB_V_COLS % tn == 0
    n_rope = ROPE_COLS // tn
    n_axial = AXIAL_COLS // tn
    lt = L // tm
    tab_spec = pl.BlockSpec((tm, LANE), lambda i, j: (i % lt, 0))
    nb_qk, jb_v, nb_v = UB_QK_COLS // tn, U_BV // tn, UB_V_COLS // tn

    def ub_col(j):
        return jnp.where(j < jb_v, jnp.minimum(j, nb_qk - 1), nb_qk + jnp.minimum(j - jb_v, nb_v - 1))

    return pl.pallas_call(
        functools.partial(_inproj_body, n_rope=n_rope, n_axial=n_axial, tn=tn),
        out_shape=[jax.ShapeDtypeStruct((T, NU), BF16), jax.ShapeDtypeStruct((T, UB_W), F32)],
        grid=(T // tm, NU // tn),
        in_specs=[
            pl.BlockSpec((tm, D), lambda i, j: (i, 0)),
            pl.BlockSpec((1, D), lambda i, j: (0, 0)),
            pl.BlockSpec((D, tn), lambda i, j: (0, j)),
            pl.BlockSpec((1, tn), lambda i, j: (0, jnp.minimum(j, n_rope - 1))),
            pl.BlockSpec((1, tn), lambda i, j: (0, jnp.clip(j - n_rope, 0, n_axial - 1))),
            tab_spec, tab_spec, tab_spec, tab_spec, tab_spec, tab_spec,
        ],
        out_specs=[pl.BlockSpec((tm, tn), lambda i, j: (i, j)),
                   pl.BlockSpec((tm, tn), lambda i, j: (i, ub_col(j)))],
        scratch_shapes=[pltpu.VMEM((tm, D), BF16)],
        compiler_params=_cparams(("parallel", "arbitrary")),
        name="inproj",
    )(x, g, w, qs, gd, *tabs)


def _flash_init(m_ref, l_ref, acc_ref):
    m_ref[...] = jnp.full(m_ref.shape, NEG, F32)
    l_ref[...] = jnp.zeros(l_ref.shape, F32)
    acc_ref[...] = jnp.zeros(acc_ref.shape, F32)


def _flash_step(qq_ref, k_ref, v_ref, m_ref, l_ref, acc_ref):
    s = _dot_nt(qq_ref[...], k_ref[...])
    chunks = [s[:, c * LANE:(c + 1) * LANE] for c in range(s.shape[1] // LANE)]
    mx = chunks[0]
    for sc in chunks[1:]:
        mx = jnp.maximum(mx, sc)
    m_prev = m_ref[...]
    m_new = jnp.maximum(m_prev, jnp.max(mx, axis=-1, keepdims=True))
    alpha = jnp.exp2(m_prev - m_new)
    ps = [jnp.exp2(sc - m_new) for sc in chunks]
    lsum = ps[0]
    for pc in ps[1:]:
        lsum = lsum + pc
    p = jnp.concatenate([pc.astype(BF16) for pc in ps], axis=1)
    l_ref[...] = alpha * l_ref[...] + lsum
    acc_ref[...] = alpha * acc_ref[...] + _dot(p, v_ref[...])
    m_ref[...] = m_new


def _flash_out(l_ref, acc_ref):
    return acc_ref[...] * (1.0 / jnp.sum(l_ref[...], axis=-1, keepdims=True))


def _diff_body(q_ref, k_ref, v_ref, lam_ref, gn_ref, o_ref, qq_ref, m_ref, l_ref, acc_ref, *, tq, lam_init):
    ki = pl.program_id(3)

    @pl.when(ki == 0)
    def _():
        q = q_ref[...]
        lane = lax.broadcasted_iota(jnp.int32, q.shape, 1)
        zero = jnp.zeros_like(q)
        qq_ref[0:tq, :] = jnp.where(lane < HEAD_DIM, q, zero)
        qq_ref[tq:2 * tq, :] = jnp.where(lane >= HEAD_DIM, q, zero)
        _flash_init(m_ref, l_ref, acc_ref)

    _flash_step(qq_ref, k_ref, v_ref, m_ref, l_ref, acc_ref)

    @pl.when(ki == pl.num_programs(3) - 1)
    def _():
        o = _flash_out(l_ref, acc_ref)
        lv = lam_ref[...]
        lam = (jnp.exp(jnp.sum(lv[0:1] * lv[1:2], axis=-1, keepdims=True))
               - jnp.exp(jnp.sum(lv[2:3] * lv[3:4], axis=-1, keepdims=True)) + lam_init)
        a = o[0:tq] - lam * o[tq:2 * tq]
        o_ref[...] = (_rms(a, gn_ref[...]) * (1.0 - lam_init)).astype(BF16)


def _diff_attention(u3, lam_vecs, gn, lam_init, *, tq=512, tk=2048):
    B, L, _ = u3.shape
    tq = min(tq, L)
    tk = min(tk, L)
    qb, kb, vb = U_AQ // LANE, U_AK // LANE, U_AV // LANE
    return pl.pallas_call(
        functools.partial(_diff_body, tq=tq, lam_init=lam_init),
        out_shape=jax.ShapeDtypeStruct((B, L, A_HEADS * LANE), BF16),
        grid=(B, A_HEADS, L // tq, L // tk),
        in_specs=[
            pl.BlockSpec((None, tq, LANE), lambda b, h, qi, ki: (b, qi, qb + h)),
            pl.BlockSpec((None, tk, LANE), lambda b, h, qi, ki: (b, ki, kb + h)),
            pl.BlockSpec((None, tk, LANE), lambda b, h, qi, ki: (b, ki, vb + h)),
            pl.BlockSpec((4, HEAD_DIM), lambda b, h, qi, ki: (0, 0)),
            pl.BlockSpec((1, LANE), lambda b, h, qi, ki: (0, 0)),
        ],
        out_specs=pl.BlockSpec((None, tq, LANE), lambda b, h, qi, ki: (b, qi, h)),
        scratch_shapes=[
            pltpu.VMEM((2 * tq, LANE), BF16),
            pltpu.VMEM((2 * tq, LANE), F32),
            pltpu.VMEM((2 * tq, LANE), F32),
            pltpu.VMEM((2 * tq, LANE), F32),
        ],
        compiler_params=_cparams(("parallel", "parallel", "parallel", "arbitrary")),
        name="diff_attn",
    )(u3, u3, u3, lam_vecs, gn)


def _gqa_body(q_ref, k_ref, v_ref, o_ref, qq_ref, m_ref, l_ref, acc_ref, *, tq):
    ki = pl.program_id(3)

    @pl.when(ki == 0)
    def _():
        for j in range(D_GROUP):
            qq_ref[j * tq:(j + 1) * tq, :] = q_ref[:, j * LANE:(j + 1) * LANE]
        _flash_init(m_ref, l_ref, acc_ref)

    _flash_step(qq_ref, k_ref, v_ref, m_ref, l_ref, acc_ref)

    @pl.when(ki == pl.num_programs(3) - 1)
    def _():
        o = _flash_out(l_ref, acc_ref).astype(BF16)
        for j in range(D_GROUP):
            o_ref[:, j * LANE:(j + 1) * LANE] = o[j * tq:(j + 1) * tq]


def _gqa_attention(u3, *, tq=512, tk=2048):
    B, L, _ = u3.shape
    tq = min(tq, L)
    tk = min(tk, L)
    gw = D_GROUP * LANE
    qb, kb, vb = U_DQ // gw, U_DK // LANE, U_DV // LANE
    return pl.pallas_call(
        functools.partial(_gqa_body, tq=tq),
        out_shape=jax.ShapeDtypeStruct((B, L, D_PAD_W), BF16),
        grid=(B, D_KV_HEADS, L // tq, L // tk),
        in_specs=[
            pl.BlockSpec((None, tq, gw), lambda b, g, qi, ki: (b, qi, qb + g)),
            pl.BlockSpec((None, tk, LANE), lambda b, g, qi, ki: (b, ki, kb + g)),
            pl.BlockSpec((None, tk, LANE), lambda b, g, qi, ki: (b, ki, vb + g)),
        ],
        out_specs=pl.BlockSpec((None, tq, gw), lambda b, g, qi, ki: (b, qi, g)),
        scratch_shapes=[
            pltpu.VMEM((D_GROUP * tq, LANE), BF16),
            pltpu.VMEM((D_GROUP * tq, LANE), F32),
            pltpu.VMEM((D_GROUP * tq, LANE), F32),
            pltpu.VMEM((D_GROUP * tq, LANE), F32),
        ],
        compiler_params=_cparams(("parallel", "parallel", "parallel", "arbitrary")),
        name="gqa_attn",
    )(u3, u3, u3)


def _dil_body(q_ref, k_ref, kp_ref, kn_ref, v_ref, vp_ref, vn_ref, o_ref, kbuf, vbuf, m_ref, l_ref, acc_ref,
              *, TT, HALO, L):
    i = pl.program_id(2)
    H = B_SIDE
    kbuf[0:HALO, :] = kp_ref[...]
    kbuf[HALO:HALO + TT, :] = k_ref[...]
    kbuf[HALO + TT:HALO + TT + HALO, :] = kn_ref[...]
    vbuf[0:HALO, :] = vp_ref[...]
    vbuf[HALO:HALO + TT, :] = v_ref[...]
    vbuf[HALO + TT:HALO + TT + HALO, :] = vn_ref[...]

    tq = LANE
    tw = tq + 2 * H
    head0 = lax.broadcasted_iota(jnp.int32, (tq, LANE), 1) < HEAD_DIM
    row = lax.broadcasted_iota(jnp.int32, (2 * tq, tw), 0)
    col = lax.broadcasted_iota(jnp.int32, (2 * tq, tw), 1)
    band = jnp.abs(col - H - (row % tq)) <= H

    def per_head(x):
        return jnp.where(head0, jnp.broadcast_to(x[0:tq], (tq, LANE)), jnp.broadcast_to(x[tq:2 * tq], (tq, LANE)))

    for p, (_, d) in enumerate(B_PATTERNS):
        n_sub = L // d

        def tile(idx, carry, d=d, p=p, n_sub=n_sub):
            r = idx % d
            j = idx // d
            q_start = r + d * (j * tq)
            k_start = HALO + r + d * (j * tq - H)
            if d == 1:
                q_rows, k_rows = pl.ds(q_start, tq), pl.ds(k_start, tw)
            else:
                q_rows, k_rows = pl.ds(q_start, tq, stride=d), pl.ds(k_start, tw, stride=d)
            q = q_ref[q_rows, :].astype(BF16)
            zero = jnp.zeros_like(q)
            qq = jnp.concatenate([jnp.where(head0, q, zero), jnp.where(head0, zero, q)], axis=0)
            kw = kbuf[k_rows, :].astype(BF16)
            vw = vbuf[k_rows, :].astype(BF16)
            n0 = (i * TT) // d + j * tq - H
            valid = band & (col >= -n0) & (col < n_sub - n0)
            s = jnp.where(valid, _dot_nt(qq, kw), NEG)
            m = jnp.max(s, axis=-1, keepdims=True)
            e = jnp.exp2(s - m)
            m_t = per_head(m)
            l_t = per_head(jnp.sum(e, axis=-1, keepdims=True))
            pv = _dot(e.astype(BF16), vw)
            pv_t = jnp.where(head0, pv[0:tq], pv[tq:2 * tq])
            if p == 0:
                m_ref[q_rows, :] = m_t
                l_ref[q_rows, :] = l_t
                acc_ref[q_rows, :] = pv_t
            else:
                m_old = m_ref[q_rows, :]
                m_new = jnp.maximum(m_old, m_t)
                a_old = jnp.exp2(m_old - m_new)
                a_t = jnp.exp2(m_t - m_new)
                m_ref[q_rows, :] = m_new
                l_ref[q_rows, :] = a_old * l_ref[q_rows, :] + a_t * l_t
                acc_ref[q_rows, :] = a_old * acc_ref[q_rows, :] + a_t * pv_t
            return carry

        lax.fori_loop(0, TT // tq, tile, 0, unroll=4)

    o_ref[...] = (acc_ref[...] * (1.0 / l_ref[...])).astype(BF16)


B_TILE = 2048
B_HALO = 1024


def _dilated_attention(ub3):
    B, L, _ = ub3.shape
    TT, HALO = B_TILE, B_HALO
    assert L % TT == 0 and TT % HALO == 0 and HALO >= B_SIDE * max(d for _, d in B_PATTERNS)
    pairs = B_HEADS // 2
    qb, kb, vb = UB_Q // LANE, UB_K // LANE, UB_V // LANE
    th = TT // HALO
    nh = L // HALO

    def main(base):
        return pl.BlockSpec((None, TT, LANE), lambda b, hp, i: (b, i, base + hp))

    def prev(base):
        return pl.BlockSpec((None, HALO, LANE), lambda b, hp, i: (b, jnp.maximum(i * th - 1, 0), base + hp))

    def nxt(base):
        return pl.BlockSpec((None, HALO, LANE), lambda b, hp, i: (b, jnp.minimum((i + 1) * th, nh - 1), base + hp))

    return pl.pallas_call(
        functools.partial(_dil_body, TT=TT, HALO=HALO, L=L),
        out_shape=jax.ShapeDtypeStruct((B, L, pairs * LANE), BF16),
        grid=(B, pairs, L // TT),
        in_specs=[main(qb), main(kb), prev(kb), nxt(kb), main(vb), prev(vb), nxt(vb)],
        out_specs=pl.BlockSpec((None, TT, LANE), lambda b, hp, i: (b, i, hp)),
        scratch_shapes=[pltpu.VMEM((TT + 2 * HALO, LANE), F32), pltpu.VMEM((TT + 2 * HALO, LANE), F32),
                        pltpu.VMEM((TT, LANE), F32), pltpu.VMEM((TT, LANE), F32), pltpu.VMEM((TT, LANE), F32)],
        compiler_params=_cparams(("parallel", "parallel", "parallel")),
        name="dilated_attn",
    )(ub3, ub3, ub3, ub3, ub3, ub3, ub3)


def _gla_body(*refs, R, backward):
    if backward:
        (q_ref, k_ref, v_ref, low_ref, wup_ref, bias_ref, fwd_ref, cg_ref, gn_ref, o_ref, s_ref) = refs
    else:
        (q_ref, k_ref, v_ref, low_ref, wup_ref, bias_ref, o_ref, s_ref) = refs
    C = C_CHUNK
    n = R // C

    @pl.when(pl.program_id(2) == 0)
    def _():
        s_ref[...] = jnp.zeros(s_ref.shape, F32)

    x = _dot(low_ref[...], wup_ref[...]) + bias_ref[...]
    gl = (jnp.minimum(x, 0.0) - jnp.log(1.0 + jnp.exp(-jnp.abs(x)))) * (1.0 / C_GATE_NORM)

    r64 = lax.broadcasted_iota(jnp.int32, (C, C), 0)
    c64 = lax.broadcasted_iota(jnp.int32, (C, C), 1)
    tri = (c64 >= r64) if backward else (c64 <= r64)
    tri_f = tri.astype(F32)
    lane = lax.broadcasted_iota(jnp.int32, (C, LANE), 1)
    qrow = lax.broadcasted_iota(jnp.int32, (C, LANE), 0)
    key = lane % C
    tri_wide = (key >= qrow) if backward else (key <= qrow)
    head0 = lane < HEAD_DIM
    lane2 = lax.broadcasted_iota(jnp.int32, (C, 2 * LANE), 1)
    vhead0 = lane2 < LANE
    rr = lax.broadcasted_iota(jnp.int32, (LANE, 2 * LANE), 0)
    cc = lax.broadcasted_iota(jnp.int32, (LANE, 2 * LANE), 1)
    blockdiag = (rr < HEAD_DIM) == (cc < LANE)
    eye = (lax.broadcasted_iota(jnp.int32, (LANE, LANE), 0) == lax.broadcasted_iota(jnp.int32, (LANE, LANE), 1))
    zpad = jnp.zeros((C, LANE), F32)
    zpad_v = jnp.zeros((C, 2 * LANE), BF16)

    order = range(n - 1, -1, -1) if backward else range(n)
    for c in order:
        rows = slice(c * C, (c + 1) * C)
        gc = gl[rows]
        cum = jnp.dot(tri_f, gc, preferred_element_type=F32, precision=lax.Precision.HIGHEST)
        last = cum[0:1] if backward else cum[C - 1:C]
        qc = q_ref[rows, :].astype(F32)
        kc = k_ref[rows, :].astype(F32)
        vc = v_ref[rows, :]
        qd = (qc * jnp.exp(cum)).astype(BF16)
        kinv = kc * jnp.exp(-cum)
        kdec = kc * jnp.exp(last - cum)
        zero = jnp.zeros_like(kinv)
        kbd = jnp.concatenate([jnp.where(head0, kinv, zero), jnp.where(head0, zero, kinv)], axis=0).astype(BF16)
        att = jnp.where(tri_wide, _dot_nt(qd, kbd), 0.0).astype(BF16)
        zv = jnp.zeros_like(vc)
        vbd = jnp.concatenate([jnp.where(vhead0, vc, zv), jnp.where(vhead0, zv, vc)], axis=0)
        state = s_ref[...]
        o = _dot(att, vbd) + _dot(qd, state.astype(BF16))
        kdec_t = jnp.concatenate([kdec, zpad], axis=0).T.astype(BF16)
        upd = _dot(kdec_t, jnp.concatenate([vc, zpad_v], axis=0))
        dec_row = jnp.broadcast_to(jnp.exp(last), (LANE, LANE))
        dec_col = jnp.sum(jnp.where(eye, dec_row, 0.0), axis=-1, keepdims=True)
        s_ref[...] = dec_col * state + jnp.where(blockdiag, upd, 0.0)
        if backward:
            tot = o + fwd_ref[rows, :]
            gate = cg_ref[rows, :].astype(F32)
            gate = gate * (1.0 / (1.0 + jnp.exp(-gate)))
            for h in range(2):
                hs = slice(h * LANE, (h + 1) * LANE)
                o_ref[rows, hs] = (_rms(tot[:, hs], gn_ref[...]) * gate[:, hs]).astype(BF16)
        else:
            o_ref[rows, :] = o


def _gla(u3, wup_f, bias_f, wup_b, bias_b, gn, *, R=512):
    B, L, _ = u3.shape
    R = min(R, L)
    nb = L // R
    pairs = C_HEADS // 2
    W = C_HEADS * LANE
    qb, kb = U_CQ // LANE, U_CK // LANE
    vb, gb, lb = U_CV // (2 * LANE), U_CG // (2 * LANE), U_CLOW // (2 * LANE)

    def specs(rev):
        def ri(i):
            return nb - 1 - i if rev else i
        return dict(
            q=pl.BlockSpec((None, R, LANE), lambda b, p, i: (b, ri(i), qb + p)),
            k=pl.BlockSpec((None, R, LANE), lambda b, p, i: (b, ri(i), kb + p)),
            v=pl.BlockSpec((None, R, 2 * LANE), lambda b, p, i: (b, ri(i), vb + p)),
            low=pl.BlockSpec((None, R, 2 * LANE), lambda b, p, i: (b, ri(i), lb)),
            wup=pl.BlockSpec((2 * LANE, LANE), lambda b, p, i: (0, p)),
            bias=pl.BlockSpec((1, LANE), lambda b, p, i: (0, p)),
            out=pl.BlockSpec((None, R, 2 * LANE), lambda b, p, i: (b, ri(i), p)),
            cg=pl.BlockSpec((None, R, 2 * LANE), lambda b, p, i: (b, ri(i), gb + p)),
            gn=pl.BlockSpec((1, LANE), lambda b, p, i: (0, 0)),
        )

    sem = _cparams(("parallel", "parallel", "arbitrary"))
    sf = specs(False)
    fwd = pl.pallas_call(
        functools.partial(_gla_body, R=R, backward=False),
        out_shape=jax.ShapeDtypeStruct((B, L, W), F32),
        grid=(B, pairs, nb),
        in_specs=[sf["q"], sf["k"], sf["v"], sf["low"], sf["wup"], sf["bias"]],
        out_specs=sf["out"],
        scratch_shapes=[pltpu.VMEM((LANE, 2 * LANE), F32)],
        compiler_params=sem,
        name="gla_fwd",
    )(u3, u3, u3, u3, wup_f, bias_f)
    sb = specs(True)
    return pl.pallas_call(
        functools.partial(_gla_body, R=R, backward=True),
        out_shape=jax.ShapeDtypeStruct((B, L, W), BF16),
        grid=(B, pairs, nb),
        in_specs=[sb["q"], sb["k"], sb["v"], sb["low"], sb["wup"], sb["bias"], sb["out"], sb["cg"], sb["gn"]],
        out_specs=sb["out"],
        scratch_shapes=[pltpu.VMEM((LANE, 2 * LANE), F32)],
        compiler_params=sem,
        name="gla_bwd",
    )(u3, u3, u3, u3, wup_b, bias_b, fwd, u3, gn)


def _outproj_body(x_ref, a_ref, b_ref, c_ref, d_ref, wa_ref, wb_ref, wc_ref, wd_ref, o_ref):
    acc = _dot(a_ref[...], wa_ref[...]) + _dot(b_ref[...], wb_ref[...])
    acc = acc + _dot(c_ref[...], wc_ref[...]) + _dot(d_ref[...], wd_ref[...])
    o_ref[...] = x_ref[...] + acc


def _outproj(x, oa, ob, oc, od, wa, wb, wc, wd, *, tm=1024, tn=512):
    T, D = x.shape
    tm = min(tm, T)
    tn = min(tn, D)

    def act(w):
        return pl.BlockSpec((tm, w), lambda i, j: (i, 0))

    def wt(w):
        return pl.BlockSpec((w, tn), lambda i, j: (0, j))

    widths = [oa.shape[1], ob.shape[1], oc.shape[1], od.shape[1]]
    return pl.pallas_call(
        _outproj_body,
        out_shape=jax.ShapeDtypeStruct((T, D), F32),
        grid=(T // tm, D // tn),
        in_specs=[pl.BlockSpec((tm, tn), lambda i, j: (i, j))] + [act(w) for w in widths] + [wt(w) for w in widths],
        out_specs=pl.BlockSpec((tm, tn), lambda i, j: (i, j)),
        compiler_params=_cparams(("parallel", "arbitrary")),
        name="outproj",
    )(x, oa, ob, oc, od, wa, wb, wc, wd)


def _pad_heads(w, n):
    k = w.shape[0]
    return jnp.pad(w.reshape(k, n, HEAD_DIM), ((0, 0), (0, 0), (0, LANE - HEAD_DIM))).reshape(k, n * LANE)


def _prep_w_in(w):
    sizes = (768, 768, 768, 768, 768, 768, 384, 384, 768, 768, 32, 768, 256, 256)
    offs = [0]
    for s in sizes:
        offs.append(offs[-1] + s)
    (a_q, a_k, a_v, b_q, b_k, b_v, c_q, c_k, c_v, c_g, c_low, d_q, d_k, d_v) = [
        w[:, offs[i]:offs[i + 1]] for i in range(len(sizes))]
    low = jnp.pad(c_low, ((0, 0), (0, 2 * LANE - 2 * C_RANK)))
    cols = [b_q, b_k, a_q, a_k,
            _pad_heads(d_q, D_Q_HEADS), _pad_heads(d_k, D_KV_HEADS),
            b_v, a_v, _pad_heads(d_v, D_KV_HEADS),
            c_q * QK_SCALE, c_k, c_v, c_g, low]
    out = jnp.concatenate(cols, axis=1).astype(BF16)
    assert out.shape[1] == NU
    return out


def _prep_w_out(w):
    wa = w[0:768]
    wb = w[768:1536]
    wc = w[1536:2304]
    wd = w[2304:3072]
    n = w.shape[1]
    wd = jnp.pad(wd.reshape(D_Q_HEADS, HEAD_DIM, n), ((0, 0), (0, LANE - HEAD_DIM), (0, 0))).reshape(D_PAD_W, n)
    return [t.astype(BF16) for t in (wa, wb, wc, wd)]


def _rope_col_scale():
    col = jnp.arange(ROPE_COLS)
    is_q = ((col >= U_AQ) & (col < U_AK)) | ((col >= U_BQ) & (col < U_BK))
    return jnp.where(is_q, QK_SCALE * LOG2E, 1.0).astype(F32).reshape(1, -1)


def _rope_tables(L):
    t = jnp.arange(L, dtype=F32)
    lane = jnp.arange(LANE)
    l64 = lane % HEAD_DIM
    half = ROPE_DIMS // 2
    inv = ROPE_THETA ** (-jnp.arange(0, ROPE_DIMS, 2, dtype=F32) / ROPE_DIMS)
    ang = t[:, None] * inv[None, :]
    ang_l = ang[:, l64 % half]
    in_rot = (l64 < ROPE_DIMS)[None, :]
    c8 = jnp.where(in_rot, jnp.cos(ang_l), 1.0)
    sa8 = jnp.where(((l64 >= half) & (l64 < ROPE_DIMS))[None, :], jnp.sin(ang_l), 0.0)
    sb8 = jnp.where((l64 < half)[None, :], -jnp.sin(ang_l), 0.0)

    q = HEAD_DIM // 4
    inv2 = AXIAL_THETA ** (-jnp.arange(0, HEAD_DIM // 2, 2, dtype=F32) / (HEAD_DIM // 2))
    rows = L // GRID_W
    row_pos = jnp.repeat(jnp.arange(rows, dtype=F32), GRID_W)
    col_pos = jnp.tile(jnp.arange(GRID_W, dtype=F32), rows)
    ang_r = row_pos[:, None] * inv2[None, :]
    ang_c = col_pos[:, None] * inv2[None, :]
    ang_x = jnp.where((l64 < 2 * q)[None, :], ang_r[:, l64 % q], ang_c[:, l64 % q])
    real = (lane < HEAD_DIM)[None, :]
    cx = jnp.where(real, jnp.cos(ang_x), 0.0)
    upper = ((l64 % (2 * q)) >= q)[None, :]
    sax = jnp.where(real & upper, jnp.sin(ang_x), 0.0)
    sbx = jnp.where(real & ~upper, -jnp.sin(ang_x), 0.0)
    return [c8, sa8, sb8, cx, sax, sbx]


def _prep_layer(l, ffn1_norm, ffn1_w_gate, ffn1_w_up, ffn1_w_down, mix_norm, w_in, w_out,
                diff_lambda_q1, diff_lambda_k1, diff_lambda_q2, diff_lambda_k2, diff_out_norm,
                gla_gate_up_f, gla_gate_bias_f, gla_gate_up_b, gla_gate_bias_b, gla_out_norm,
                gqa_q_norm, gqa_k_norm, ffn2_norm, ffn2_w_gate, ffn2_w_up, ffn2_w_down):
    def row(v):
        return v.astype(F32).reshape(1, -1)

    zeros64 = jnp.zeros((HEAD_DIM,), F32)
    gq = jnp.tile(jnp.concatenate([gqa_q_norm[l].astype(F32) * (QK_SCALE * LOG2E), zeros64]), D_Q_HEADS)
    gk = jnp.tile(jnp.concatenate([gqa_k_norm[l].astype(F32), zeros64]), D_KV_HEADS)
    wup_f = jnp.zeros((2 * LANE, C_HEADS * HEAD_DIM), F32).at[0:C_RANK].set(gla_gate_up_f[l])
    wup_b = jnp.zeros((2 * LANE, C_HEADS * HEAD_DIM), F32).at[C_RANK:2 * C_RANK].set(gla_gate_up_b[l])
    return dict(
        n1=row(ffn1_norm[l]), wg1=ffn1_w_gate[l].astype(BF16), wu1=ffn1_w_up[l].astype(BF16),
        wd1=ffn1_w_down[l].astype(BF16),
        nmix=row(mix_norm[l]), w_in=_prep_w_in(w_in[l]), w_out=_prep_w_out(w_out[l]),
        gd=jnp.concatenate([gq, gk]).reshape(1, -1),
        lam=jnp.stack([diff_lambda_q1[l], diff_lambda_k1[l], diff_lambda_q2[l], diff_lambda_k2[l]]).astype(F32),
        lam_init=0.8 - 0.6 * math.exp(-0.3 * l),
        gdiff=row(diff_out_norm[l]),
        wup_f=wup_f.astype(BF16), bias_f=row(gla_gate_bias_f[l]),
        wup_b=wup_b.astype(BF16), bias_b=row(gla_gate_bias_b[l]),
        ggla=row(gla_out_norm[l]),
        n2=row(ffn2_norm[l]), wg2=ffn2_w_gate[l].astype(BF16), wu2=ffn2_w_up[l].astype(BF16),
        wd2=ffn2_w_down[l].astype(BF16),
    )


def _trunk(x, layers, final_g):
    B, L, D = x.shape
    xt = x.reshape(B * L, D)
    tabs = _rope_tables(L)
    qscale = _rope_col_scale()
    for l, p in enumerate(layers):
        xt = _ffn(xt, p["n1"], p["wg1"], p["wu1"], p["wd1"], final_g, final=False)
        u, ub = _inproj(xt, p["nmix"], p["w_in"], qscale, p["gd"], tabs, L)
        u3 = u.reshape(B, L, NU)
        oa = _diff_attention(u3, p["lam"], p["gdiff"], p["lam_init"]).reshape(B * L, -1)
        ob = _dilated_attention(ub.reshape(B, L, UB_W)).reshape(B * L, -1)
        oc = _gla(u3, p["wup_f"], p["bias_f"], p["wup_b"], p["bias_b"], p["ggla"]).reshape(B * L, -1)
        od = _gqa_attention(u3).reshape(B * L, -1)
        xt = _outproj(xt, oa, ob, oc, od, *p["w_out"])
        xt = _ffn(xt, p["n2"], p["wg2"], p["wu2"], p["wd2"], final_g, final=(l == len(layers) - 1))
    return xt.reshape(B, L, D)


def kernel(x_prompt, x_sample, ffn1_norm, ffn1_w_gate, ffn1_w_up, ffn1_w_down, mix_norm, w_in, w_out, diff_lambda_q1, diff_lambda_k1, diff_lambda_q2, diff_lambda_k2, diff_out_norm, gla_gate_up_f, gla_gate_bias_f, gla_gate_up_b, gla_gate_bias_b, gla_out_norm, gqa_q_norm, gqa_k_norm, ffn2_norm, ffn2_w_gate, ffn2_w_up, ffn2_w_down, final_norm):
    depth = w_in.shape[0]
    layers = [_prep_layer(l, ffn1_norm, ffn1_w_gate, ffn1_w_up, ffn1_w_down, mix_norm, w_in, w_out,
                          diff_lambda_q1, diff_lambda_k1, diff_lambda_q2, diff_lambda_k2, diff_out_norm,
                          gla_gate_up_f, gla_gate_bias_f, gla_gate_up_b, gla_gate_bias_b, gla_out_norm,
                          gqa_q_norm, gqa_k_norm, ffn2_norm, ffn2_w_gate, ffn2_w_up, ffn2_w_down)
              for l in range(depth)]
    final_g = final_norm.astype(F32).reshape(1, -1)
    return (_trunk(x_prompt, layers, final_g), _trunk(x_sample, layers, final_g))
```

```python
import functools
import math

import jax
import jax.numpy as jnp
from jax import lax
from jax.experimental import pallas as pl
from jax.experimental.pallas import tpu as pltpu

F32 = jnp.float32
BF16 = jnp.bfloat16

HEAD_DIM = 64
EPS = 1e-6
ROPE_THETA = 500000.0
ROPE_DIMS = HEAD_DIM // 4
AXIAL_THETA = 10000.0
GRID_W = 64
A_HEADS = 6
B_HEADS = 12
B_PATTERNS = ((128, 1), (512, 4), (2048, 16))
B_SIDE = 64
C_HEADS = 6
C_RANK = 16
C_CHUNK = 64
C_GATE_NORM = 16.0
D_Q_HEADS = 12
D_KV_HEADS = 4
D_GROUP = D_Q_HEADS // D_KV_HEADS
QK_SCALE = HEAD_DIM ** -0.5
LOG2E = math.log2(math.e)

LANE = 128
NEG = -1e30

PROJ_TN = 768
UB_Q = 0
UB_K = 768
UB_V = 1536
UB_W = 2304
UB_ROPE_TILES = 2
UA_AQ = 0
UA_AK = 768
UA_AV = 1536
UA_DV = 2304
UA_CQ = 2816
UA_CK = 3200
UA_CV = 3584
UA_CG = 4352
UA_CLOW = 5120
UA_W = 5376
UA_ROPE_TILES = 2
UD_Q = 0
UD_K = 1536
UD_W = 2048
D_PAD_W = D_Q_HEADS * LANE


def _cparams(sem, vmem_mb=56):
    return pltpu.CompilerParams(dimension_semantics=sem, vmem_limit_bytes=vmem_mb * 1024 * 1024)


def _dot(a, b):
    return jnp.dot(a, b, preferred_element_type=F32)


def _dot_nt(a, b):
    return lax.dot_general(a, b, (((1,), (1,)), ((), ())), preferred_element_type=F32)


def _rms(x, g):
    ms = jnp.mean(x * x, axis=-1, keepdims=True)
    return x * lax.rsqrt(ms + EPS) * g


def _ffn_body(x_ref, g_ref, wg_ref, wu_ref, wd_ref, fg_ref, o_ref, h_ref, acc_ref, *, final):
    j = pl.program_id(1)
    nj = pl.num_programs(1)

    @pl.when(j == 0)
    def _():
        h_ref[...] = _rms(x_ref[...], g_ref[...]).astype(BF16)
        acc_ref[...] = jnp.zeros(acc_ref.shape, F32)

    h = h_ref[...]
    gate = _dot(h, wg_ref[...])
    up = _dot(h, wu_ref[...])
    act = (gate * (1.0 / (1.0 + jnp.exp(-gate))) * up).astype(BF16)
    acc_ref[...] += _dot(act, wd_ref[...])

    @pl.when(j == nj - 1)
    def _():
        y = x_ref[...] + 0.5 * acc_ref[...]
        if final:
            y = _rms(y, fg_ref[...])
        o_ref[...] = y


def _ffn(x, g, wg, wu, wd, fg, *, final, tm=512, tf=512):
    T, D = x.shape
    FF = wg.shape[1]
    tm = min(tm, T)
    tf = min(tf, FF)
    assert T % tm == 0 and FF % tf == 0
    return pl.pallas_call(
        functools.partial(_ffn_body, final=final),
        out_shape=jax.ShapeDtypeStruct((T, D), F32),
        grid=(T // tm, FF // tf),
        in_specs=[
            pl.BlockSpec((tm, D), lambda i, j: (i, 0)),
            pl.BlockSpec((1, D), lambda i, j: (0, 0)),
            pl.BlockSpec((D, tf), lambda i, j: (0, j)),
            pl.BlockSpec((D, tf), lambda i, j: (0, j)),
            pl.BlockSpec((tf, D), lambda i, j: (j, 0)),
            pl.BlockSpec((1, D), lambda i, j: (0, 0)),
        ],
        out_specs=pl.BlockSpec((tm, D), lambda i, j: (i, 0)),
        scratch_shapes=[pltpu.VMEM((tm, D), BF16), pltpu.VMEM((tm, D), F32)],
        compiler_params=_cparams(("parallel", "arbitrary")),
        name="ffn",
    )(x, g, wg, wu, wd, fg)


def _rot(x, c, sa, sb, shift):
    return x * c + pltpu.roll(x, shift, 1) * sa + pltpu.roll(x, LANE - shift, 1) * sb


def _proj_body(x_ref, g_ref, w_ref, cs_ref, c_ref, sa_ref, sb_ref, o_ref, h_ref, *, tr, shift, head_norm):
    @pl.when(pl.program_id(1) == 0)
    def _():
        h_ref[...] = _rms(x_ref[...], g_ref[...]).astype(BF16)

    tm, tn = o_ref.shape
    for rc in range(tm // tr):
        rows = slice(rc * tr, (rc + 1) * tr)
        acc = _dot(h_ref[rows, :], w_ref[...])
        c_t, sa_t, sb_t = c_ref[rows, :], sa_ref[rows, :], sb_ref[rows, :]
        for c in range(tn // LANE):
            sl = slice(c * LANE, (c + 1) * LANE)
            y = acc[:, sl]
            if head_norm:
                ms = jnp.sum(y * y, axis=-1, keepdims=True) * (1.0 / HEAD_DIM)
                y = y * lax.rsqrt(ms + EPS) * cs_ref[:, sl]
                y = _rot(y, c_t, sa_t, sb_t, shift)
            else:
                y = _rot(y, c_t, sa_t, sb_t, shift) * cs_ref[:, sl]
            o_ref[rows, sl] = y.astype(o_ref.dtype)


def _proj(x, g, w, cs, tabs, L, *, out_dtype, n_rot_tiles, shift, head_norm, tn, name, tm=1024, tr=256):
    T, D = x.shape
    W = w.shape[1]
    tm = min(tm, L)
    tr = min(tr, tm)
    assert T % tm == 0 and L % tm == 0 and W % tn == 0 and tm % tr == 0
    lt = L // tm
    tab_spec = pl.BlockSpec((None, tm, LANE), lambda i, j: (jnp.where(j < n_rot_tiles, 0, 1), i % lt, 0))
    return pl.pallas_call(
        functools.partial(_proj_body, tr=tr, shift=shift, head_norm=head_norm),
        out_shape=jax.ShapeDtypeStruct((T, W), out_dtype),
        grid=(T // tm, W // tn),
        in_specs=[
            pl.BlockSpec((tm, D), lambda i, j: (i, 0)),
            pl.BlockSpec((1, D), lambda i, j: (0, 0)),
            pl.BlockSpec((D, tn), lambda i, j: (0, j)),
            pl.BlockSpec((1, tn), lambda i, j: (0, j)),
            tab_spec, tab_spec, tab_spec,
        ],
        out_specs=pl.BlockSpec((tm, tn), lambda i, j: (i, j)),
        scratch_shapes=[pltpu.VMEM((tm, D), BF16)],
        compiler_params=_cparams(("parallel", "arbitrary")),
        name=name,
    )(x, g, w, cs, *tabs)


def _flash_init(m_ref, acc_ref):
    m_ref[...] = jnp.full(m_ref.shape, NEG, F32)
    acc_ref[...] = jnp.zeros(acc_ref.shape, F32)


def _flash_step(qq_ref, k_ref, v, m_ref, acc_ref, l_ref=None):
    s = _dot_nt(qq_ref[...], k_ref[...])
    chunks = [s[:, c * LANE:(c + 1) * LANE] for c in range(s.shape[1] // LANE)]
    mx = chunks[0]
    for sc in chunks[1:]:
        mx = jnp.maximum(mx, sc)
    m_prev = m_ref[...]
    m_new = jnp.maximum(m_prev, jnp.max(mx, axis=-1, keepdims=True))
    alpha = jnp.exp2(m_prev - m_new)
    ps = [jnp.exp2(sc - m_new) for sc in chunks]
    if l_ref is not None:
        lsum = ps[0]
        for pc in ps[1:]:
            lsum = lsum + pc
        l_ref[...] = alpha * l_ref[...] + lsum
    p = jnp.concatenate([pc.astype(BF16) for pc in ps], axis=1)
    acc_ref[...] = alpha * acc_ref[...] + _dot(p, v)
    m_ref[...] = m_new


def _flash_out(acc_ref, l_ref=None, ones_lane=None):
    acc = acc_ref[...]
    if l_ref is not None:
        den = jnp.sum(l_ref[...], axis=-1, keepdims=True)
    else:
        lane = lax.broadcasted_iota(jnp.int32, acc.shape, 1)
        den = jnp.sum(jnp.where(lane == ones_lane, acc, 0.0), axis=-1, keepdims=True)
    return acc * (1.0 / den)


def _diff_body(q_ref, k_ref, v_ref, lam_ref, gn_ref, o_ref, qq_ref, m_ref, l_ref, acc_ref, *, tq, lam_init):
    ki = pl.program_id(3)

    @pl.when(ki == 0)
    def _():
        q = q_ref[...]
        lane = lax.broadcasted_iota(jnp.int32, q.shape, 1)
        zero = jnp.zeros_like(q)
        qq_ref[0:tq, :] = jnp.where(lane < HEAD_DIM, q, zero)
        qq_ref[tq:2 * tq, :] = jnp.where(lane >= HEAD_DIM, q, zero)
        _flash_init(m_ref, acc_ref)
        l_ref[...] = jnp.zeros(l_ref.shape, F32)

    _flash_step(qq_ref, k_ref, v_ref[...], m_ref, acc_ref, l_ref)

    @pl.when(ki == pl.num_programs(3) - 1)
    def _():
        o = _flash_out(acc_ref, l_ref=l_ref)
        lv = lam_ref[...]
        lam = (jnp.exp(jnp.sum(lv[0:1] * lv[1:2], axis=-1, keepdims=True))
               - jnp.exp(jnp.sum(lv[2:3] * lv[3:4], axis=-1, keepdims=True)) + lam_init)
        a = o[0:tq] - lam * o[tq:2 * tq]
        o_ref[...] = (_rms(a, gn_ref[...]) * (1.0 - lam_init)).astype(BF16)


def _diff_attention(u3, lam_vecs, gn, lam_init, *, tq=512, tk=2048):
    B, L, _ = u3.shape
    tq = min(tq, L)
    tk = min(tk, L)
    qb, kb, vb = UA_AQ // LANE, UA_AK // LANE, UA_AV // LANE
    return pl.pallas_call(
        functools.partial(_diff_body, tq=tq, lam_init=lam_init),
        out_shape=jax.ShapeDtypeStruct((B, L, A_HEADS * LANE), BF16),
        grid=(B, A_HEADS, L // tq, L // tk),
        in_specs=[
            pl.BlockSpec((None, tq, LANE), lambda b, h, qi, ki: (b, qi, qb + h)),
            pl.BlockSpec((None, tk, LANE), lambda b, h, qi, ki: (b, ki, kb + h)),
            pl.BlockSpec((None, tk, LANE), lambda b, h, qi, ki: (b, ki, vb + h)),
            pl.BlockSpec((4, HEAD_DIM), lambda b, h, qi, ki: (0, 0)),
            pl.BlockSpec((1, LANE), lambda b, h, qi, ki: (0, 0)),
        ],
        out_specs=pl.BlockSpec((None, tq, LANE), lambda b, h, qi, ki: (b, qi, h)),
        scratch_shapes=[
            pltpu.VMEM((2 * tq, LANE), BF16),
            pltpu.VMEM((2 * tq, LANE), F32),
            pltpu.VMEM((2 * tq, LANE), F32),
            pltpu.VMEM((2 * tq, LANE), F32),
        ],
        compiler_params=_cparams(("parallel", "parallel", "parallel", "arbitrary")),
        name="diff_attn",
    )(u3, u3, u3, lam_vecs, gn)


def _gqa_body(q_ref, k_ref, v_ref, o_ref, qq_ref, m_ref, acc_ref, *, tq):
    ki = pl.program_id(3)

    @pl.when(ki == 0)
    def _():
        for j in range(D_GROUP):
            qq_ref[j * tq:(j + 1) * tq, :] = q_ref[:, j * LANE:(j + 1) * LANE]
        _flash_init(m_ref, acc_ref)

    v = v_ref[...]
    lane_v = lax.broadcasted_iota(jnp.int32, v.shape, 1)
    v = jnp.where(lane_v == HEAD_DIM, jnp.ones_like(v), v)
    _flash_step(qq_ref, k_ref, v, m_ref, acc_ref)

    @pl.when(ki == pl.num_programs(3) - 1)
    def _():
        o = _flash_out(acc_ref, ones_lane=HEAD_DIM)
        lane = lax.broadcasted_iota(jnp.int32, o.shape, 1)
        o = jnp.where(lane < HEAD_DIM, o, 0.0).astype(BF16)
        for j in range(D_GROUP):
            o_ref[:, j * LANE:(j + 1) * LANE] = o[j * tq:(j + 1) * tq]


def _gqa_attention(ud3, ua3, *, tq=512, tk=2048):
    B, L, _ = ud3.shape
    tq = min(tq, L)
    tk = min(tk, L)
    gw = D_GROUP * LANE
    qb, kb, vb = UD_Q // gw, UD_K // LANE, UA_DV // LANE
    return pl.pallas_call(
        functools.partial(_gqa_body, tq=tq),
        out_shape=jax.ShapeDtypeStruct((B, L, D_PAD_W), BF16),
        grid=(B, D_KV_HEADS, L // tq, L // tk),
        in_specs=[
            pl.BlockSpec((None, tq, gw), lambda b, g, qi, ki: (b, qi, qb + g)),
            pl.BlockSpec((None, tk, LANE), lambda b, g, qi, ki: (b, ki, kb + g)),
            pl.BlockSpec((None, tk, LANE), lambda b, g, qi, ki: (b, ki, vb + g)),
        ],
        out_specs=pl.BlockSpec((None, tq, gw), lambda b, g, qi, ki: (b, qi, g)),
        scratch_shapes=[
            pltpu.VMEM((D_GROUP * tq, LANE), BF16),
            pltpu.VMEM((D_GROUP * tq, LANE), F32),
            pltpu.VMEM((D_GROUP * tq, LANE), F32),
        ],
        compiler_params=_cparams(("parallel", "parallel", "parallel", "arbitrary")),
        name="gqa_attn",
    )(ud3, ud3, ua3)


def _dil_body(q_ref, k_ref, kp_ref, kn_ref, v_ref, vp_ref, vn_ref, o_ref, kbuf, vbuf, m_ref, l_ref, acc_ref,
              *, TT, HALO, L):
    i = pl.program_id(2)
    H = B_SIDE
    kbuf[0:HALO, :] = kp_ref[...]
    kbuf[HALO:HALO + TT, :] = k_ref[...]
    kbuf[HALO + TT:HALO + TT + HALO, :] = kn_ref[...]
    vbuf[0:HALO, :] = vp_ref[...]
    vbuf[HALO:HALO + TT, :] = v_ref[...]
    vbuf[HALO + TT:HALO + TT + HALO, :] = vn_ref[...]

    tq = LANE
    tw = tq + 2 * H
    head0 = lax.broadcasted_iota(jnp.int32, (tq, LANE), 1) < HEAD_DIM
    row = lax.broadcasted_iota(jnp.int32, (2 * tq, tw), 0)
    col = lax.broadcasted_iota(jnp.int32, (2 * tq, tw), 1)
    band = jnp.abs(col - H - (row % tq)) <= H

    def per_head(x):
        return jnp.where(head0, jnp.broadcast_to(x[0:tq], (tq, LANE)), jnp.broadcast_to(x[tq:2 * tq], (tq, LANE)))

    for p, (_, d) in enumerate(B_PATTERNS):
        n_sub = L // d

        def tile(idx, carry, d=d, p=p, n_sub=n_sub):
            r = idx % d
            j = idx // d
            q_start = r + d * (j * tq)
            k_start = HALO + r + d * (j * tq - H)
            if d == 1:
                q_rows, k_rows = pl.ds(q_start, tq), pl.ds(k_start, tw)
            else:
                q_rows, k_rows = pl.ds(q_start, tq, stride=d), pl.ds(k_start, tw, stride=d)
            q = q_ref[q_rows, :].astype(BF16)
            zero = jnp.zeros_like(q)
            qq = jnp.concatenate([jnp.where(head0, q, zero), jnp.where(head0, zero, q)], axis=0)
            kw = kbuf[k_rows, :].astype(BF16)
            vw = vbuf[k_rows, :].astype(BF16)
            n0 = (i * TT) // d + j * tq - H
            valid = band & (col >= -n0) & (col < n_sub - n0)
            s = jnp.where(valid, _dot_nt(qq, kw), NEG)
            m = jnp.max(s, axis=-1, keepdims=True)
            e = jnp.exp2(s - m)
            m_t = per_head(m)
            l_t = per_head(jnp.sum(e, axis=-1, keepdims=True))
            pv = _dot(e.astype(BF16), vw)
            pv_t = jnp.where(head0, pv[0:tq], pv[tq:2 * tq])
            if p == 0:
                m_ref[q_rows, :] = m_t
                l_ref[q_rows, :] = l_t
                acc_ref[q_rows, :] = pv_t
            else:
                m_old = m_ref[q_rows, :]
                m_new = jnp.maximum(m_old, m_t)
                a_old = jnp.exp2(m_old - m_new)
                a_t = jnp.exp2(m_t - m_new)
                m_ref[q_rows, :] = m_new
                l_ref[q_rows, :] = a_old * l_ref[q_rows, :] + a_t * l_t
                acc_ref[q_rows, :] = a_old * acc_ref[q_rows, :] + a_t * pv_t
            return carry

        lax.fori_loop(0, TT // tq, tile, 0, unroll=8)

    o_ref[...] = (acc_ref[...] * (1.0 / l_ref[...])).astype(BF16)


B_TILE = 2048
B_HALO = 1024


def _dilated_attention(ub3):
    B, L, _ = ub3.shape
    TT, HALO = B_TILE, B_HALO
    assert L % TT == 0 and TT % HALO == 0 and HALO >= B_SIDE * max(d for _, d in B_PATTERNS)
    pairs = B_HEADS // 2
    qb, kb, vb = UB_Q // LANE, UB_K // LANE, UB_V // LANE
    th = TT // HALO
    nh = L // HALO

    def main(base):
        return pl.BlockSpec((None, TT, LANE), lambda b, hp, i: (b, i, base + hp))

    def prev(base):
        return pl.BlockSpec((None, HALO, LANE), lambda b, hp, i: (b, jnp.maximum(i * th - 1, 0), base + hp))

    def nxt(base):
        return pl.BlockSpec((None, HALO, LANE), lambda b, hp, i: (b, jnp.minimum((i + 1) * th, nh - 1), base + hp))

    return pl.pallas_call(
        functools.partial(_dil_body, TT=TT, HALO=HALO, L=L),
        out_shape=jax.ShapeDtypeStruct((B, L, pairs * LANE), BF16),
        grid=(B, pairs, L // TT),
        in_specs=[main(qb), main(kb), prev(kb), nxt(kb), main(vb), prev(vb), nxt(vb)],
        out_specs=pl.BlockSpec((None, TT, LANE), lambda b, hp, i: (b, i, hp)),
        scratch_shapes=[pltpu.VMEM((TT + 2 * HALO, LANE), F32), pltpu.VMEM((TT + 2 * HALO, LANE), F32),
                        pltpu.VMEM((TT, LANE), F32), pltpu.VMEM((TT, LANE), F32), pltpu.VMEM((TT, LANE), F32)],
        compiler_params=_cparams(("parallel", "parallel", "parallel")),
        name="dilated_attn",
    )(ub3, ub3, ub3, ub3, ub3, ub3, ub3)


def _gla_body(*refs, R, backward):
    if backward:
        (q_ref, k_ref, v_ref, low_ref, wup_ref, bias_ref, fwd_ref, cg_ref, gn_ref, o_ref, s_ref) = refs
    else:
        (q_ref, k_ref, v_ref, low_ref, wup_ref, bias_ref, o_ref, s_ref) = refs
    C = C_CHUNK
    n = R // C

    @pl.when(pl.program_id(2) == 0)
    def _():
        s_ref[...] = jnp.zeros(s_ref.shape, F32)

    x = _dot(low_ref[...], wup_ref[...]) + bias_ref[...]
    gl = (jnp.minimum(x, 0.0) - jnp.log(1.0 + jnp.exp(-jnp.abs(x)))) * (1.0 / C_GATE_NORM)

    r64 = lax.broadcasted_iota(jnp.int32, (C, C), 0)
    c64 = lax.broadcasted_iota(jnp.int32, (C, C), 1)
    tri = (c64 >= r64) if backward else (c64 <= r64)
    tri_f = tri.astype(F32)
    lane = lax.broadcasted_iota(jnp.int32, (C, LANE), 1)
    qrow = lax.broadcasted_iota(jnp.int32, (C, LANE), 0)
    key = lane % C
    tri_wide = (key >= qrow) if backward else (key <= qrow)
    head0 = lane < HEAD_DIM
    lane2 = lax.broadcasted_iota(jnp.int32, (C, 2 * LANE), 1)
    vhead0 = lane2 < LANE
    rr = lax.broadcasted_iota(jnp.int32, (LANE, 2 * LANE), 0)
    cc = lax.broadcasted_iota(jnp.int32, (LANE, 2 * LANE), 1)
    blockdiag = (rr < HEAD_DIM) == (cc < LANE)
    eye = (lax.broadcasted_iota(jnp.int32, (LANE, LANE), 0) == lax.broadcasted_iota(jnp.int32, (LANE, LANE), 1))
    zpad = jnp.zeros((C, LANE), F32)
    zpad_v = jnp.zeros((C, 2 * LANE), BF16)

    chunk_rows = [slice(c * C, (c + 1) * C) for c in range(n)]
    cums = [jnp.dot(tri_f, gl[rows], preferred_element_type=F32, precision=lax.Precision.HIGHEST)
            for rows in chunk_rows]
    lasts = [(cum[0:1] if backward else cum[C - 1:C]) for cum in cums]
    qds, atts, intra, upds, dec_cols = [], [], [], [], []
    for rows, cum, last in zip(chunk_rows, cums, lasts):
        kc = k_ref[rows, :].astype(F32)
        qds.append((q_ref[rows, :].astype(F32) * jnp.exp(cum)).astype(BF16))
        kinv = kc * jnp.exp(-cum)
        zero = jnp.zeros_like(kinv)
        kbd = jnp.concatenate([jnp.where(head0, kinv, zero), jnp.where(head0, zero, kinv)], axis=0).astype(BF16)
        atts.append(jnp.where(tri_wide, _dot_nt(qds[-1], kbd), 0.0).astype(BF16))
        kdec = kc * jnp.exp(last - cum)
        kdec_t = jnp.concatenate([kdec, zpad], axis=0).T.astype(BF16)
        vc = v_ref[rows, :]
        upd = _dot(kdec_t, jnp.concatenate([vc, zpad_v], axis=0))
        upds.append(jnp.where(blockdiag, upd, 0.0))
        dec_row = jnp.broadcast_to(jnp.exp(last), (LANE, LANE))
        dec_cols.append(jnp.sum(jnp.where(eye, dec_row, 0.0), axis=-1, keepdims=True))
    for rows, att in zip(chunk_rows, atts):
        vc = v_ref[rows, :]
        zv = jnp.zeros_like(vc)
        vbd = jnp.concatenate([jnp.where(vhead0, vc, zv), jnp.where(vhead0, zv, vc)], axis=0)
        intra.append(_dot(att, vbd))

    state = s_ref[...]
    entering = [None] * n
    for c in (range(n - 1, -1, -1) if backward else range(n)):
        entering[c] = state.astype(BF16)
        state = dec_cols[c] * state + upds[c]
    s_ref[...] = state

    if backward:
        gate = cg_ref[...].astype(F32)
        gate = gate * (1.0 / (1.0 + jnp.exp(-gate)))
    for c, rows in enumerate(chunk_rows):
        o = intra[c] + _dot(qds[c], entering[c])
        if backward:
            tot = o + fwd_ref[rows, :]
            for h in range(2):
                hs = slice(h * LANE, (h + 1) * LANE)
                o_ref[rows, hs] = (_rms(tot[:, hs], gn_ref[...]) * gate[rows, hs]).astype(BF16)
        else:
            o_ref[rows, :] = o


def _gla(u3, wup_f, bias_f, wup_b, bias_b, gn, *, R=512):
    B, L, _ = u3.shape
    R = min(R, L)
    nb = L // R
    pairs = C_HEADS // 2
    W = C_HEADS * LANE
    qb, kb = UA_CQ // LANE, UA_CK // LANE
    vb, gb, lb = UA_CV // (2 * LANE), UA_CG // (2 * LANE), UA_CLOW // (2 * LANE)

    def specs(rev):
        def ri(i):
            return nb - 1 - i if rev else i
        return dict(
            q=pl.BlockSpec((None, R, LANE), lambda b, p, i: (b, ri(i), qb + p)),
            k=pl.BlockSpec((None, R, LANE), lambda b, p, i: (b, ri(i), kb + p)),
            v=pl.BlockSpec((None, R, 2 * LANE), lambda b, p, i: (b, ri(i), vb + p)),
            low=pl.BlockSpec((None, R, 2 * LANE), lambda b, p, i: (b, ri(i), lb)),
            wup=pl.BlockSpec((2 * LANE, LANE), lambda b, p, i: (0, p)),
            bias=pl.BlockSpec((1, LANE), lambda b, p, i: (0, p)),
            out=pl.BlockSpec((None, R, 2 * LANE), lambda b, p, i: (b, ri(i), p)),
            cg=pl.BlockSpec((None, R, 2 * LANE), lambda b, p, i: (b, ri(i), gb + p)),
            gn=pl.BlockSpec((1, LANE), lambda b, p, i: (0, 0)),
        )

    sem = _cparams(("parallel", "parallel", "arbitrary"))
    sf = specs(False)
    fwd = pl.pallas_call(
        functools.partial(_gla_body, R=R, backward=False),
        out_shape=jax.ShapeDtypeStruct((B, L, W), F32),
        grid=(B, pairs, nb),
        in_specs=[sf["q"], sf["k"], sf["v"], sf["low"], sf["wup"], sf["bias"]],
        out_specs=sf["out"],
        scratch_shapes=[pltpu.VMEM((LANE, 2 * LANE), F32)],
        compiler_params=sem,
        name="gla_fwd",
    )(u3, u3, u3, u3, wup_f, bias_f)
    sb = specs(True)
    return pl.pallas_call(
        functools.partial(_gla_body, R=R, backward=True),
        out_shape=jax.ShapeDtypeStruct((B, L, W), BF16),
        grid=(B, pairs, nb),
        in_specs=[sb["q"], sb["k"], sb["v"], sb["low"], sb["wup"], sb["bias"], sb["out"], sb["cg"], sb["gn"]],
        out_specs=sb["out"],
        scratch_shapes=[pltpu.VMEM((LANE, 2 * LANE), F32)],
        compiler_params=sem,
        name="gla_bwd",
    )(u3, u3, u3, u3, wup_b, bias_b, fwd, u3, gn)


def _outproj_body(x_ref, a_ref, b_ref, c_ref, d_ref, wa_ref, wb_ref, wc_ref, wd_ref, o_ref):
    acc = _dot(a_ref[...], wa_ref[...]) + _dot(b_ref[...], wb_ref[...])
    acc = acc + _dot(c_ref[...], wc_ref[...]) + _dot(d_ref[...], wd_ref[...])
    o_ref[...] = x_ref[...] + acc


def _outproj(x, oa, ob, oc, od, wa, wb, wc, wd, *, tm=1024, tn=512):
    T, D = x.shape
    tm = min(tm, T)
    tn = min(tn, D)

    def act(w):
        return pl.BlockSpec((tm, w), lambda i, j: (i, 0))

    def wt(w):
        return pl.BlockSpec((w, tn), lambda i, j: (0, j))

    widths = [oa.shape[1], ob.shape[1], oc.shape[1], od.shape[1]]
    return pl.pallas_call(
        _outproj_body,
        out_shape=jax.ShapeDtypeStruct((T, D), F32),
        grid=(T // tm, D // tn),
        in_specs=[pl.BlockSpec((tm, tn), lambda i, j: (i, j))] + [act(w) for w in widths] + [wt(w) for w in widths],
        out_specs=pl.BlockSpec((tm, tn), lambda i, j: (i, j)),
        compiler_params=_cparams(("parallel", "arbitrary")),
        name="outproj",
    )(x, oa, ob, oc, od, wa, wb, wc, wd)


def _pad_heads(w, n):
    k = w.shape[0]
    return jnp.pad(w.reshape(k, n, HEAD_DIM), ((0, 0), (0, 0), (0, LANE - HEAD_DIM))).reshape(k, n * LANE)


def _prep_w_in(w):
    sizes = (768, 768, 768, 768, 768, 768, 384, 384, 768, 768, 32, 768, 256, 256)
    offs = [0]
    for s in sizes:
        offs.append(offs[-1] + s)
    (a_q, a_k, a_v, b_q, b_k, b_v, c_q, c_k, c_v, c_g, c_low, d_q, d_k, d_v) = [
        w[:, offs[i]:offs[i + 1]] for i in range(len(sizes))]
    low = jnp.pad(c_low, ((0, 0), (0, 2 * LANE - 2 * C_RANK)))
    wb = jnp.concatenate([b_q, b_k, b_v], axis=1).astype(BF16)
    wa = jnp.concatenate([a_q, a_k, a_v, _pad_heads(d_v, D_KV_HEADS),
                          c_q * QK_SCALE, c_k, c_v, c_g, low], axis=1).astype(BF16)
    wd = jnp.concatenate([_pad_heads(d_q, D_Q_HEADS), _pad_heads(d_k, D_KV_HEADS)], axis=1).astype(BF16)
    assert wb.shape[1] == UB_W and wa.shape[1] == UA_W and wd.shape[1] == UD_W
    return wb, wa, wd


def _prep_w_out(w):
    wa = w[0:768]
    wb = w[768:1536]
    wc = w[1536:2304]
    wd = w[2304:3072]
    n = w.shape[1]
    wd = jnp.pad(wd.reshape(D_Q_HEADS, HEAD_DIM, n), ((0, 0), (0, LANE - HEAD_DIM), (0, 0))).reshape(D_PAD_W, n)
    return [t.astype(BF16) for t in (wa, wb, wc, wd)]


def _q_col_scale(width, q_lo, q_hi):
    col = jnp.arange(width)
    return jnp.where((col >= q_lo) & (col < q_hi), QK_SCALE * LOG2E, 1.0).astype(F32).reshape(1, -1)


def _rope_tables(L):
    t = jnp.arange(L, dtype=F32)
    lane = jnp.arange(LANE)
    l64 = lane % HEAD_DIM
    half = ROPE_DIMS // 2
    inv = ROPE_THETA ** (-jnp.arange(0, ROPE_DIMS, 2, dtype=F32) / ROPE_DIMS)
    ang = t[:, None] * inv[None, :]
    ang_l = ang[:, l64 % half]
    in_rot = (l64 < ROPE_DIMS)[None, :]
    c8 = jnp.where(in_rot, jnp.cos(ang_l), 1.0)
    sa8 = jnp.where(((l64 >= half) & (l64 < ROPE_DIMS))[None, :], jnp.sin(ang_l), 0.0)
    sb8 = jnp.where((l64 < half)[None, :], -jnp.sin(ang_l), 0.0)

    q = HEAD_DIM // 4
    inv2 = AXIAL_THETA ** (-jnp.arange(0, HEAD_DIM // 2, 2, dtype=F32) / (HEAD_DIM // 2))
    rows = L // GRID_W
    row_pos = jnp.repeat(jnp.arange(rows, dtype=F32), GRID_W)
    col_pos = jnp.tile(jnp.arange(GRID_W, dtype=F32), rows)
    ang_r = row_pos[:, None] * inv2[None, :]
    ang_c = col_pos[:, None] * inv2[None, :]
    ang_x = jnp.where((l64 < 2 * q)[None, :], ang_r[:, l64 % q], ang_c[:, l64 % q])
    real = (lane < HEAD_DIM)[None, :]
    cx = jnp.where(real, jnp.cos(ang_x), 0.0)
    upper = ((l64 % (2 * q)) >= q)[None, :]
    sax = jnp.where(real & upper, jnp.sin(ang_x), 0.0)
    sbx = jnp.where(real & ~upper, -jnp.sin(ang_x), 0.0)
    one, zero = jnp.ones_like(c8), jnp.zeros_like(c8)
    rope8 = [jnp.stack([c8, one]), jnp.stack([sa8, zero]), jnp.stack([sb8, zero])]
    axial = [jnp.stack([cx, one]), jnp.stack([sax, zero]), jnp.stack([sbx, zero])]
    return rope8, axial


def _prep_layer(l, ffn1_norm, ffn1_w_gate, ffn1_w_up, ffn1_w_down, mix_norm, w_in, w_out,
                diff_lambda_q1, diff_lambda_k1, diff_lambda_q2, diff_lambda_k2, diff_out_norm,
                gla_gate_up_f, gla_gate_bias_f, gla_gate_up_b, gla_gate_bias_b, gla_out_norm,
                gqa_q_norm, gqa_k_norm, ffn2_norm, ffn2_w_gate, ffn2_w_up, ffn2_w_down):
    def row(v):
        return v.astype(F32).reshape(1, -1)

    zeros64 = jnp.zeros((HEAD_DIM,), F32)
    gq = jnp.tile(jnp.concatenate([gqa_q_norm[l].astype(F32) * (QK_SCALE * LOG2E), zeros64]), D_Q_HEADS)
    gk = jnp.tile(jnp.concatenate([gqa_k_norm[l].astype(F32), zeros64]), D_KV_HEADS)
    wup_f = jnp.zeros((2 * LANE, C_HEADS * HEAD_DIM), F32).at[0:C_RANK].set(gla_gate_up_f[l])
    wup_b = jnp.zeros((2 * LANE, C_HEADS * HEAD_DIM), F32).at[C_RANK:2 * C_RANK].set(gla_gate_up_b[l])
    return dict(
        n1=row(ffn1_norm[l]), wg1=ffn1_w_gate[l].astype(BF16), wu1=ffn1_w_up[l].astype(BF16),
        wd1=ffn1_w_down[l].astype(BF16),
        nmix=row(mix_norm[l]), w_in=_prep_w_in(w_in[l]), w_out=_prep_w_out(w_out[l]),
        gd=jnp.concatenate([gq, gk]).reshape(1, -1),
        lam=jnp.stack([diff_lambda_q1[l], diff_lambda_k1[l], diff_lambda_q2[l], diff_lambda_k2[l]]).astype(F32),
        lam_init=0.8 - 0.6 * math.exp(-0.3 * l),
        gdiff=row(diff_out_norm[l]),
        wup_f=wup_f.astype(BF16), bias_f=row(gla_gate_bias_f[l]),
        wup_b=wup_b.astype(BF16), bias_b=row(gla_gate_bias_b[l]),
        ggla=row(gla_out_norm[l]),
        n2=row(ffn2_norm[l]), wg2=ffn2_w_gate[l].astype(BF16), wu2=ffn2_w_up[l].astype(BF16),
        wd2=ffn2_w_down[l].astype(BF16),
    )


def _in_projections(xt, p, tabs, L):
    rope8, axial = tabs
    w_b, w_a, w_d = p["w_in"]
    half = ROPE_DIMS // 2
    ub = _proj(xt, p["nmix"], w_b, _q_col_scale(UB_W, UB_Q, UB_K), rope8, L, out_dtype=F32,
               n_rot_tiles=UB_ROPE_TILES, shift=half, head_norm=False, tn=PROJ_TN, name="inproj_b")
    ua = _proj(xt, p["nmix"], w_a, _q_col_scale(UA_W, UA_AQ, UA_AK), rope8, L, out_dtype=BF16,
               n_rot_tiles=UA_ROPE_TILES, shift=half, head_norm=False, tn=PROJ_TN, name="inproj_a")
    ud = _proj(xt, p["nmix"], w_d, p["gd"], axial, L, out_dtype=BF16, n_rot_tiles=UD_W // 1024,
               shift=HEAD_DIM // 4, head_norm=True, tn=1024, name="inproj_d")
    return ub, ua, ud


def _trunk(x, layers, final_g):
    B, L, D = x.shape
    xt = x.reshape(B * L, D)
    tabs = _rope_tables(L)
    for l, p in enumerate(layers):
        xt = _ffn(xt, p["n1"], p["wg1"], p["wu1"], p["wd1"], final_g, final=False)
        ub, ua, ud = _in_projections(xt, p, tabs, L)
        ua3 = ua.reshape(B, L, UA_W)
        oa = _diff_attention(ua3, p["lam"], p["gdiff"], p["lam_init"]).reshape(B * L, -1)
        ob = _dilated_attention(ub.reshape(B, L, UB_W)).reshape(B * L, -1)
        oc = _gla(ua3, p["wup_f"], p["bias_f"], p["wup_b"], p["bias_b"], p["ggla"]).reshape(B * L, -1)
        od = _gqa_attention(ud.reshape(B, L, UD_W), ua3).reshape(B * L, -1)
        xt = _outproj(xt, oa, ob, oc, od, *p["w_out"])
        xt = _ffn(xt, p["n2"], p["wg2"], p["wu2"], p["wd2"], final_g, final=(l == len(layers) - 1))
    return xt.reshape(B, L, D)


def kernel(x_prompt, x_sample, ffn1_norm, ffn1_w_gate, ffn1_w_up, ffn1_w_down, mix_norm, w_in, w_out, diff_lambda_q1, diff_lambda_k1, diff_lambda_q2, diff_lambda_k2, diff_out_norm, gla_gate_up_f, gla_gate_bias_f, gla_gate_up_b, gla_gate_bias_b, gla_out_norm, gqa_q_norm, gqa_k_norm, ffn2_norm, ffn2_w_gate, ffn2_w_up, ffn2_w_down, final_norm):
    depth = w_in.shape[0]
    layers = [_prep_layer(l, ffn1_norm, ffn1_w_gate, ffn1_w_up, ffn1_w_down, mix_norm, w_in, w_out,
                          diff_lambda_q1, diff_lambda_k1, diff_lambda_q2, diff_lambda_k2, diff_out_norm,
                          gla_gate_up_f, gla_gate_bias_f, gla_gate_up_b, gla_gate_bias_b, gla_out_norm,
                          gqa_q_norm, gqa_k_norm, ffn2_norm, ffn2_w_gate, ffn2_w_up, ffn2_w_down)
              for l in range(depth)]
    final_g = final_norm.astype(F32).reshape(1, -1)
    return (_trunk(x_prompt, layers, final_g), _trunk(x_sample, layers, final_g))
```

```python
import functools
import math

import jax
import jax.numpy as jnp
from jax import lax
from jax.experimental import pallas as pl
from jax.experimental.pallas import tpu as pltpu

F32 = jnp.float32
BF16 = jnp.bfloat16

HEAD_DIM = 64
EPS = 1e-6
ROPE_THETA = 500000.0
ROPE_DIMS = HEAD_DIM // 4
AXIAL_THETA = 10000.0
GRID_W = 64
A_HEADS = 6
B_HEADS = 12
B_PATTERNS = ((128, 1), (512, 4), (2048, 16))
B_SIDE = 64
C_HEADS = 6
C_RANK = 16
C_CHUNK = 64
C_GATE_NORM = 16.0
D_Q_HEADS = 12
D_KV_HEADS = 4
D_GROUP = D_Q_HEADS // D_KV_HEADS
QK_SCALE = HEAD_DIM ** -0.5
LOG2E = math.log2(math.e)

LANE = 128
NEG = -1e30

PROJ_TN = 768
UB_Q = 0
UB_K = 768
UB_V = 1536
UB_W = 2304
UB_ROPE_TILES = 2
UA_AQ = 0
UA_AK = 768
UA_AV = 1536
UA_DV = 2304
UA_CQ = 2816
UA_CK = 3200
UA_CV = 3584
UA_CG = 4352
UA_CLOW = 5120
UA_W = 5376
UA_ROPE_TILES = 2
UD_Q = 0
UD_K = 1536
UD_W = 2048
D_PAD_W = D_Q_HEADS * LANE


def _cparams(sem, vmem_mb=56):
    return pltpu.CompilerParams(dimension_semantics=sem, vmem_limit_bytes=vmem_mb * 1024 * 1024)


def _dot(a, b):
    return jnp.dot(a, b, preferred_element_type=F32)


def _dot_nt(a, b):
    return lax.dot_general(a, b, (((1,), (1,)), ((), ())), preferred_element_type=F32)


def _rms(x, g):
    ms = jnp.mean(x * x, axis=-1, keepdims=True)
    return x * lax.rsqrt(ms + EPS) * g


def _ffn_body(x_ref, g_ref, wg_ref, wu_ref, wd_ref, ng_ref, *rest, tail):
    if tail == "next_norm":
        o_ref, hn_ref, h_ref, acc_ref = rest
    else:
        o_ref, h_ref, acc_ref = rest
    j = pl.program_id(1)
    nj = pl.num_programs(1)

    @pl.when(j == 0)
    def _():
        h_ref[...] = _rms(x_ref[...], g_ref[...]).astype(BF16)
        acc_ref[...] = jnp.zeros(acc_ref.shape, F32)

    h = h_ref[...]
    gate = _dot(h, wg_ref[...])
    up = _dot(h, wu_ref[...])
    act = (gate * (1.0 / (1.0 + jnp.exp(-gate))) * up).astype(BF16)
    acc_ref[...] += _dot(act, wd_ref[...])

    @pl.when(j == nj - 1)
    def _():
        y = x_ref[...] + 0.5 * acc_ref[...]
        if tail == "final_norm":
            y = _rms(y, ng_ref[...])
        o_ref[...] = y
        if tail == "next_norm":
            hn_ref[...] = _rms(y, ng_ref[...]).astype(BF16)


def _ffn(x, g, wg, wu, wd, ng, *, tail, tm=512, tf=512):
    T, D = x.shape
    FF = wg.shape[1]
    tm = min(tm, T)
    tf = min(tf, FF)
    assert T % tm == 0 and FF % tf == 0
    row_spec = pl.BlockSpec((tm, D), lambda i, j: (i, 0))
    if tail == "next_norm":
        out_shape = [jax.ShapeDtypeStruct((T, D), F32), jax.ShapeDtypeStruct((T, D), BF16)]
        out_specs = [row_spec, row_spec]
    else:
        out_shape = jax.ShapeDtypeStruct((T, D), F32)
        out_specs = row_spec
    return pl.pallas_call(
        functools.partial(_ffn_body, tail=tail),
        out_shape=out_shape,
        grid=(T // tm, FF // tf),
        in_specs=[
            pl.BlockSpec((tm, D), lambda i, j: (i, 0)),
            pl.BlockSpec((1, D), lambda i, j: (0, 0)),
            pl.BlockSpec((D, tf), lambda i, j: (0, j)),
            pl.BlockSpec((D, tf), lambda i, j: (0, j)),
            pl.BlockSpec((tf, D), lambda i, j: (j, 0)),
            pl.BlockSpec((1, D), lambda i, j: (0, 0)),
        ],
        out_specs=out_specs,
        scratch_shapes=[pltpu.VMEM((tm, D), BF16), pltpu.VMEM((tm, D), F32)],
        compiler_params=_cparams(("parallel", "arbitrary")),
        name="ffn",
    )(x, g, wg, wu, wd, ng)


def _rot(x, c, sa, sb, shift):
    return x * c + pltpu.roll(x, shift, 1) * sa + pltpu.roll(x, LANE - shift, 1) * sb


def _proj_body(h_ref, w_ref, cs_ref, c_ref, sa_ref, sb_ref, o_ref, *, tr, shift, head_norm):
    tm, tn = o_ref.shape
    for rc in range(tm // tr):
        rows = slice(rc * tr, (rc + 1) * tr)
        acc = _dot(h_ref[rows, :], w_ref[...])
        c_t, sa_t, sb_t = c_ref[rows, :], sa_ref[rows, :], sb_ref[rows, :]
        for c in range(tn // LANE):
            sl = slice(c * LANE, (c + 1) * LANE)
            y = acc[:, sl]
            if head_norm:
                ms = jnp.sum(y * y, axis=-1, keepdims=True) * (1.0 / HEAD_DIM)
                y = y * lax.rsqrt(ms + EPS) * cs_ref[:, sl]
                y = _rot(y, c_t, sa_t, sb_t, shift)
            else:
                y = _rot(y, c_t, sa_t, sb_t, shift) * cs_ref[:, sl]
            o_ref[rows, sl] = y.astype(o_ref.dtype)


def _proj(h, w, cs, tabs, L, *, out_dtype, n_rot_tiles, shift, head_norm, tn, name, tm=1024, tr=256):
    T, D = h.shape
    W = w.shape[1]
    tm = min(tm, L)
    tr = min(tr, tm)
    assert T % tm == 0 and L % tm == 0 and W % tn == 0 and tm % tr == 0
    lt = L // tm
    tab_spec = pl.BlockSpec((None, tm, LANE), lambda i, j: (jnp.where(j < n_rot_tiles, 0, 1), i % lt, 0))
    return pl.pallas_call(
        functools.partial(_proj_body, tr=tr, shift=shift, head_norm=head_norm),
        out_shape=jax.ShapeDtypeStruct((T, W), out_dtype),
        grid=(T // tm, W // tn),
        in_specs=[
            pl.BlockSpec((tm, D), lambda i, j: (i, 0)),
            pl.BlockSpec((D, tn), lambda i, j: (0, j)),
            pl.BlockSpec((1, tn), lambda i, j: (0, j)),
            tab_spec, tab_spec, tab_spec,
        ],
        out_specs=pl.BlockSpec((tm, tn), lambda i, j: (i, j)),
        compiler_params=_cparams(("parallel", "parallel")),
        name=name,
    )(h, w, cs, *tabs)


def _flash_init(m_ref, acc_ref):
    m_ref[...] = jnp.full(m_ref.shape, NEG, F32)
    acc_ref[...] = jnp.zeros(acc_ref.shape, F32)


def _flash_step(qq_ref, k_ref, v, m_ref, acc_ref, l_ref=None):
    s = _dot_nt(qq_ref[...], k_ref[...])
    chunks = [s[:, c * LANE:(c + 1) * LANE] for c in range(s.shape[1] // LANE)]
    mx = chunks[0]
    for sc in chunks[1:]:
        mx = jnp.maximum(mx, sc)
    m_prev = m_ref[...]
    m_new = jnp.maximum(m_prev, jnp.max(mx, axis=-1, keepdims=True))
    alpha = jnp.exp2(m_prev - m_new)
    ps = [jnp.exp2(sc - m_new) for sc in chunks]
    if l_ref is not None:
        lsum = ps[0]
        for pc in ps[1:]:
            lsum = lsum + pc
        l_ref[...] = alpha * l_ref[...] + lsum
    p = jnp.concatenate([pc.astype(BF16) for pc in ps], axis=1)
    acc_ref[...] = alpha * acc_ref[...] + _dot(p, v)
    m_ref[...] = m_new


def _flash_out(acc_ref, l_ref=None, ones_lane=None):
    acc = acc_ref[...]
    if l_ref is not None:
        den = jnp.sum(l_ref[...], axis=-1, keepdims=True)
    else:
        lane = lax.broadcasted_iota(jnp.int32, acc.shape, 1)
        den = jnp.sum(jnp.where(lane == ones_lane, acc, 0.0), axis=-1, keepdims=True)
    return acc * (1.0 / den)


def _diff_body(q_ref, k_ref, v_ref, lam_ref, gn_ref, o_ref, qq_ref, m_ref, l_ref, acc_ref, *, tq, lam_init):
    ki = pl.program_id(3)

    @pl.when(ki == 0)
    def _():
        q = q_ref[...]
        lane = lax.broadcasted_iota(jnp.int32, q.shape, 1)
        zero = jnp.zeros_like(q)
        qq_ref[0:tq, :] = jnp.where(lane < HEAD_DIM, q, zero)
        qq_ref[tq:2 * tq, :] = jnp.where(lane >= HEAD_DIM, q, zero)
        _flash_init(m_ref, acc_ref)
        l_ref[...] = jnp.zeros(l_ref.shape, F32)

    _flash_step(qq_ref, k_ref, v_ref[...], m_ref, acc_ref, l_ref)

    @pl.when(ki == pl.num_programs(3) - 1)
    def _():
        o = _flash_out(acc_ref, l_ref=l_ref)
        lv = lam_ref[...]
        lam = (jnp.exp(jnp.sum(lv[0:1] * lv[1:2], axis=-1, keepdims=True))
               - jnp.exp(jnp.sum(lv[2:3] * lv[3:4], axis=-1, keepdims=True)) + lam_init)
        a = o[0:tq] - lam * o[tq:2 * tq]
        o_ref[...] = (_rms(a, gn_ref[...]) * (1.0 - lam_init)).astype(BF16)


def _diff_attention(u3, lam_vecs, gn, lam_init, *, tq=1024, tk=2048):
    B, L, _ = u3.shape
    tq = min(tq, L)
    tk = min(tk, L)
    qb, kb, vb = UA_AQ // LANE, UA_AK // LANE, UA_AV // LANE
    return pl.pallas_call(
        functools.partial(_diff_body, tq=tq, lam_init=lam_init),
        out_shape=jax.ShapeDtypeStruct((B, L, A_HEADS * LANE), BF16),
        grid=(B, A_HEADS, L // tq, L // tk),
        in_specs=[
            pl.BlockSpec((None, tq, LANE), lambda b, h, qi, ki: (b, qi, qb + h)),
            pl.BlockSpec((None, tk, LANE), lambda b, h, qi, ki: (b, ki, kb + h)),
            pl.BlockSpec((None, tk, LANE), lambda b, h, qi, ki: (b, ki, vb + h)),
            pl.BlockSpec((4, HEAD_DIM), lambda b, h, qi, ki: (0, 0)),
            pl.BlockSpec((1, LANE), lambda b, h, qi, ki: (0, 0)),
        ],
        out_specs=pl.BlockSpec((None, tq, LANE), lambda b, h, qi, ki: (b, qi, h)),
        scratch_shapes=[
            pltpu.VMEM((2 * tq, LANE), BF16),
            pltpu.VMEM((2 * tq, LANE), F32),
            pltpu.VMEM((2 * tq, LANE), F32),
            pltpu.VMEM((2 * tq, LANE), F32),
        ],
        compiler_params=_cparams(("parallel", "parallel", "parallel", "arbitrary")),
        name="diff_attn",
    )(u3, u3, u3, lam_vecs, gn)


def _gqa_body(q_ref, k_ref, v_ref, o_ref, qq_ref, m_ref, acc_ref, *, tq):
    ki = pl.program_id(3)

    @pl.when(ki == 0)
    def _():
        for j in range(D_GROUP):
            qq_ref[j * tq:(j + 1) * tq, :] = q_ref[:, j * LANE:(j + 1) * LANE]
        _flash_init(m_ref, acc_ref)

    v = v_ref[...]
    lane_v = lax.broadcasted_iota(jnp.int32, v.shape, 1)
    v = jnp.where(lane_v == HEAD_DIM, jnp.ones_like(v), v)
    _flash_step(qq_ref, k_ref, v, m_ref, acc_ref)

    @pl.when(ki == pl.num_programs(3) - 1)
    def _():
        o = _flash_out(acc_ref, ones_lane=HEAD_DIM)
        lane = lax.broadcasted_iota(jnp.int32, o.shape, 1)
        o = jnp.where(lane < HEAD_DIM, o, 0.0).astype(BF16)
        for j in range(D_GROUP):
            o_ref[:, j * LANE:(j + 1) * LANE] = o[j * tq:(j + 1) * tq]


def _gqa_attention(ud3, ua3, *, tq=512, tk=2048):
    B, L, _ = ud3.shape
    tq = min(tq, L)
    tk = min(tk, L)
    gw = D_GROUP * LANE
    qb, kb, vb = UD_Q // gw, UD_K // LANE, UA_DV // LANE
    return pl.pallas_call(
        functools.partial(_gqa_body, tq=tq),
        out_shape=jax.ShapeDtypeStruct((B, L, D_PAD_W), BF16),
        grid=(B, D_KV_HEADS, L // tq, L // tk),
        in_specs=[
            pl.BlockSpec((None, tq, gw), lambda b, g, qi, ki: (b, qi, qb + g)),
            pl.BlockSpec((None, tk, LANE), lambda b, g, qi, ki: (b, ki, kb + g)),
            pl.BlockSpec((None, tk, LANE), lambda b, g, qi, ki: (b, ki, vb + g)),
        ],
        out_specs=pl.BlockSpec((None, tq, gw), lambda b, g, qi, ki: (b, qi, g)),
        scratch_shapes=[
            pltpu.VMEM((D_GROUP * tq, LANE), BF16),
            pltpu.VMEM((D_GROUP * tq, LANE), F32),
            pltpu.VMEM((D_GROUP * tq, LANE), F32),
        ],
        compiler_params=_cparams(("parallel", "parallel", "parallel", "arbitrary")),
        name="gqa_attn",
    )(ud3, ud3, ua3)


def _dil_body(q_ref, k_ref, kp_ref, kn_ref, v_ref, vp_ref, vn_ref, o_ref, kbuf, vbuf, m_ref, l_ref, acc_ref,
              *, TT, HALO, L):
    i = pl.program_id(2)
    H = B_SIDE
    kbuf[0:HALO, :] = kp_ref[...]
    kbuf[HALO:HALO + TT, :] = k_ref[...]
    kbuf[HALO + TT:HALO + TT + HALO, :] = kn_ref[...]
    vbuf[0:HALO, :] = vp_ref[...]
    vbuf[HALO:HALO + TT, :] = v_ref[...]
    vbuf[HALO + TT:HALO + TT + HALO, :] = vn_ref[...]

    tq = LANE
    tw = tq + 2 * H
    head0 = lax.broadcasted_iota(jnp.int32, (tq, LANE), 1) < HEAD_DIM
    row = lax.broadcasted_iota(jnp.int32, (2 * tq, tw), 0)
    col = lax.broadcasted_iota(jnp.int32, (2 * tq, tw), 1)
    band = jnp.abs(col - H - (row % tq)) <= H

    def per_head(x):
        return jnp.where(head0, jnp.broadcast_to(x[0:tq], (tq, LANE)), jnp.broadcast_to(x[tq:2 * tq], (tq, LANE)))

    for p, (_, d) in enumerate(reversed(B_PATTERNS)):
        n_sub = L // d

        def tile(idx, carry, d=d, p=p, n_sub=n_sub):
            r = idx % d
            j = idx // d
            q_start = r + d * (j * tq)
            k_start = HALO + r + d * (j * tq - H)
            if d == 1:
                q_rows, k_rows = pl.ds(q_start, tq), pl.ds(k_start, tw)
            else:
                q_rows, k_rows = pl.ds(q_start, tq, stride=d), pl.ds(k_start, tw, stride=d)
            q = q_ref[q_rows, :].astype(BF16)
            zero = jnp.zeros_like(q)
            qq = jnp.concatenate([jnp.where(head0, q, zero), jnp.where(head0, zero, q)], axis=0)
            kw = kbuf[k_rows, :].astype(BF16)
            vw = vbuf[k_rows, :].astype(BF16)
            n0 = (i * TT) // d + j * tq - H
            valid = band & (col >= -n0) & (col < n_sub - n0)
            s = jnp.where(valid, _dot_nt(qq, kw), NEG)
            m = jnp.max(s, axis=-1, keepdims=True)
            e = jnp.exp2(s - m)
            m_t = per_head(m)
            l_t = per_head(jnp.sum(e, axis=-1, keepdims=True))
            pv = _dot(e.astype(BF16), vw)
            pv_t = jnp.where(head0, pv[0:tq], pv[tq:2 * tq])
            if p == 0:
                m_ref[q_rows, :] = m_t
                l_ref[q_rows, :] = l_t
                acc_ref[q_rows, :] = pv_t
            else:
                m_old = m_ref[q_rows, :]
                m_new = jnp.maximum(m_old, m_t)
                a_old = jnp.exp2(m_old - m_new)
                a_t = jnp.exp2(m_t - m_new)
                m_ref[q_rows, :] = m_new
                l_ref[q_rows, :] = a_old * l_ref[q_rows, :] + a_t * l_t
                acc_ref[q_rows, :] = a_old * acc_ref[q_rows, :] + a_t * pv_t
            return carry

        lax.fori_loop(0, TT // tq, tile, 0, unroll=8)

    o_ref[...] = (acc_ref[...] * (1.0 / l_ref[...])).astype(BF16)


B_TILE = 2048
B_HALO = 1024


def _dilated_attention(ub3):
    B, L, _ = ub3.shape
    TT, HALO = B_TILE, B_HALO
    assert L % TT == 0 and TT % HALO == 0 and HALO >= B_SIDE * max(d for _, d in B_PATTERNS)
    pairs = B_HEADS // 2
    qb, kb, vb = UB_Q // LANE, UB_K // LANE, UB_V // LANE
    th = TT // HALO
    nh = L // HALO

    def main(base):
        return pl.BlockSpec((None, TT, LANE), lambda b, hp, i: (b, i, base + hp))

    def prev(base):
        return pl.BlockSpec((None, HALO, LANE), lambda b, hp, i: (b, jnp.maximum(i * th - 1, 0), base + hp))

    def nxt(base):
        return pl.BlockSpec((None, HALO, LANE), lambda b, hp, i: (b, jnp.minimum((i + 1) * th, nh - 1), base + hp))

    return pl.pallas_call(
        functools.partial(_dil_body, TT=TT, HALO=HALO, L=L),
        out_shape=jax.ShapeDtypeStruct((B, L, pairs * LANE), BF16),
        grid=(B, pairs, L // TT),
        in_specs=[main(qb), main(kb), prev(kb), nxt(kb), main(vb), prev(vb), nxt(vb)],
        out_specs=pl.BlockSpec((None, TT, LANE), lambda b, hp, i: (b, i, hp)),
        scratch_shapes=[pltpu.VMEM((TT + 2 * HALO, LANE), F32), pltpu.VMEM((TT + 2 * HALO, LANE), F32),
                        pltpu.VMEM((TT, LANE), F32), pltpu.VMEM((TT, LANE), F32), pltpu.VMEM((TT, LANE), F32)],
        compiler_params=_cparams(("parallel", "parallel", "parallel")),
        name="dilated_attn",
    )(ub3, ub3, ub3, ub3, ub3, ub3, ub3)


def _gla_body(*refs, R, backward):
    if backward:
        (q_ref, k_ref, v_ref, low_ref, wup_ref, bias_ref, fwd_ref, cg_ref, gn_ref, o_ref, s_ref) = refs
    else:
        (q_ref, k_ref, v_ref, low_ref, wup_ref, bias_ref, o_ref, s_ref) = refs
    C = C_CHUNK
    n = R // C

    @pl.when(pl.program_id(2) == 0)
    def _():
        s_ref[...] = jnp.zeros(s_ref.shape, F32)

    x = _dot(low_ref[...], wup_ref[...]) + bias_ref[...]
    gl = (jnp.minimum(x, 0.0) - jnp.log(1.0 + jnp.exp(-jnp.abs(x)))) * (1.0 / C_GATE_NORM)

    r64 = lax.broadcasted_iota(jnp.int32, (C, C), 0)
    c64 = lax.broadcasted_iota(jnp.int32, (C, C), 1)
    tri = (c64 >= r64) if backward else (c64 <= r64)
    tri_f = tri.astype(F32)
    lane = lax.broadcasted_iota(jnp.int32, (C, LANE), 1)
    qrow = lax.broadcasted_iota(jnp.int32, (C, LANE), 0)
    key = lane % C
    tri_wide = (key >= qrow) if backward else (key <= qrow)
    head0 = lane < HEAD_DIM
    lane2 = lax.broadcasted_iota(jnp.int32, (C, 2 * LANE), 1)
    vhead0 = lane2 < LANE
    rr = lax.broadcasted_iota(jnp.int32, (LANE, 2 * LANE), 0)
    cc = lax.broadcasted_iota(jnp.int32, (LANE, 2 * LANE), 1)
    blockdiag = (rr < HEAD_DIM) == (cc < LANE)
    eye = (lax.broadcasted_iota(jnp.int32, (LANE, LANE), 0) == lax.broadcasted_iota(jnp.int32, (LANE, LANE), 1))
    zpad = jnp.zeros((C, LANE), F32)
    zpad_v = jnp.zeros((C, 2 * LANE), BF16)

    chunk_rows = [slice(c * C, (c + 1) * C) for c in range(n)]
    cums = [jnp.dot(tri_f, gl[rows], preferred_element_type=F32, precision=lax.Precision.HIGHEST)
            for rows in chunk_rows]
    lasts = [(cum[0:1] if backward else cum[C - 1:C]) for cum in cums]
    qds, atts, intra, upds, dec_cols = [], [], [], [], []
    for rows, cum, last in zip(chunk_rows, cums, lasts):
        kc = k_ref[rows, :].astype(F32)
        qds.append((q_ref[rows, :].astype(F32) * jnp.exp(cum)).astype(BF16))
        kinv = kc * jnp.exp(-cum)
        zero = jnp.zeros_like(kinv)
        kbd = jnp.concatenate([jnp.where(head0, kinv, zero), jnp.where(head0, zero, kinv)], axis=0).astype(BF16)
        atts.append(jnp.where(tri_wide, _dot_nt(qds[-1], kbd), 0.0).astype(BF16))
        kdec = kc * jnp.exp(last - cum)
        kdec_t = jnp.concatenate([kdec, zpad], axis=0).T.astype(BF16)
        vc = v_ref[rows, :]
        upd = _dot(kdec_t, jnp.concatenate([vc, zpad_v], axis=0))
        upds.append(jnp.where(blockdiag, upd, 0.0))
        dec_row = jnp.broadcast_to(jnp.exp(last), (LANE, LANE))
        dec_cols.append(jnp.sum(jnp.where(eye, dec_row, 0.0), axis=-1, keepdims=True))
    for rows, att in zip(chunk_rows, atts):
        vc = v_ref[rows, :]
        zv = jnp.zeros_like(vc)
        vbd = jnp.concatenate([jnp.where(vhead0, vc, zv), jnp.where(vhead0, zv, vc)], axis=0)
        intra.append(_dot(att, vbd))

    state = s_ref[...]
    entering = [None] * n
    for c in (range(n - 1, -1, -1) if backward else range(n)):
        entering[c] = state.astype(BF16)
        state = dec_cols[c] * state + upds[c]
    s_ref[...] = state

    if backward:
        gate = cg_ref[...].astype(F32)
        gate = gate * (1.0 / (1.0 + jnp.exp(-gate)))
    for c, rows in enumerate(chunk_rows):
        o = intra[c] + _dot(qds[c], entering[c])
        if backward:
            tot = o + fwd_ref[rows, :]
            for h in range(2):
                hs = slice(h * LANE, (h + 1) * LANE)
                o_ref[rows, hs] = (_rms(tot[:, hs], gn_ref[...]) * gate[rows, hs]).astype(BF16)
        else:
            o_ref[rows, :] = o


def _gla(u3, wup_f, bias_f, wup_b, bias_b, gn, *, R=512):
    B, L, _ = u3.shape
    R = min(R, L)
    nb = L // R
    pairs = C_HEADS // 2
    W = C_HEADS * LANE
    qb, kb = UA_CQ // LANE, UA_CK // LANE
    vb, gb, lb = UA_CV // (2 * LANE), UA_CG // (2 * LANE), UA_CLOW // (2 * LANE)

    def specs(rev):
        def ri(i):
            return nb - 1 - i if rev else i
        return dict(
            q=pl.BlockSpec((None, R, LANE), lambda b, p, i: (b, ri(i), qb + p)),
            k=pl.BlockSpec((None, R, LANE), lambda b, p, i: (b, ri(i), kb + p)),
            v=pl.BlockSpec((None, R, 2 * LANE), lambda b, p, i: (b, ri(i), vb + p)),
            low=pl.BlockSpec((None, R, 2 * LANE), lambda b, p, i: (b, ri(i), lb)),
            wup=pl.BlockSpec((2 * LANE, LANE), lambda b, p, i: (0, p)),
            bias=pl.BlockSpec((1, LANE), lambda b, p, i: (0, p)),
            out=pl.BlockSpec((None, R, 2 * LANE), lambda b, p, i: (b, ri(i), p)),
            cg=pl.BlockSpec((None, R, 2 * LANE), lambda b, p, i: (b, ri(i), gb + p)),
            gn=pl.BlockSpec((1, LANE), lambda b, p, i: (0, 0)),
        )

    sem = _cparams(("parallel", "parallel", "arbitrary"))
    sf = specs(False)
    fwd = pl.pallas_call(
        functools.partial(_gla_body, R=R, backward=False),
        out_shape=jax.ShapeDtypeStruct((B, L, W), F32),
        grid=(B, pairs, nb),
        in_specs=[sf["q"], sf["k"], sf["v"], sf["low"], sf["wup"], sf["bias"]],
        out_specs=sf["out"],
        scratch_shapes=[pltpu.VMEM((LANE, 2 * LANE), F32)],
        compiler_params=sem,
        name="gla_fwd",
    )(u3, u3, u3, u3, wup_f, bias_f)
    sb = specs(True)
    return pl.pallas_call(
        functools.partial(_gla_body, R=R, backward=True),
        out_shape=jax.ShapeDtypeStruct((B, L, W), BF16),
        grid=(B, pairs, nb),
        in_specs=[sb["q"], sb["k"], sb["v"], sb["low"], sb["wup"], sb["bias"], sb["out"], sb["cg"], sb["gn"]],
        out_specs=sb["out"],
        scratch_shapes=[pltpu.VMEM((LANE, 2 * LANE), F32)],
        compiler_params=sem,
        name="gla_bwd",
    )(u3, u3, u3, u3, wup_b, bias_b, fwd, u3, gn)


def _outproj_body(x_ref, a_ref, b_ref, c_ref, d_ref, wa_ref, wb_ref, wc_ref, wd_ref, o_ref):
    acc = _dot(a_ref[...], wa_ref[...]) + _dot(b_ref[...], wb_ref[...])
    acc = acc + _dot(c_ref[...], wc_ref[...]) + _dot(d_ref[...], wd_ref[...])
    o_ref[...] = x_ref[...] + acc


def _outproj(x, oa, ob, oc, od, wa, wb, wc, wd, *, tm=1024, tn=512):
    T, D = x.shape
    tm = min(tm, T)
    tn = min(tn, D)

    def act(w):
        return pl.BlockSpec((tm, w), lambda i, j: (i, 0))

    def wt(w):
        return pl.BlockSpec((w, tn), lambda i, j: (0, j))

    widths = [oa.shape[1], ob.shape[1], oc.shape[1], od.shape[1]]
    return pl.pallas_call(
        _outproj_body,
        out_shape=jax.ShapeDtypeStruct((T, D), F32),
        grid=(T // tm, D // tn),
        in_specs=[pl.BlockSpec((tm, tn), lambda i, j: (i, j))] + [act(w) for w in widths] + [wt(w) for w in widths],
        out_specs=pl.BlockSpec((tm, tn), lambda i, j: (i, j)),
        compiler_params=_cparams(("parallel", "arbitrary")),
        name="outproj",
    )(x, oa, ob, oc, od, wa, wb, wc, wd)


def _pad_heads(w, n):
    k = w.shape[0]
    return jnp.pad(w.reshape(k, n, HEAD_DIM), ((0, 0), (0, 0), (0, LANE - HEAD_DIM))).reshape(k, n * LANE)


def _prep_w_in(w):
    sizes = (768, 768, 768, 768, 768, 768, 384, 384, 768, 768, 32, 768, 256, 256)
    offs = [0]
    for s in sizes:
        offs.append(offs[-1] + s)
    (a_q, a_k, a_v, b_q, b_k, b_v, c_q, c_k, c_v, c_g, c_low, d_q, d_k, d_v) = [
        w[:, offs[i]:offs[i + 1]] for i in range(len(sizes))]
    low = jnp.pad(c_low, ((0, 0), (0, 2 * LANE - 2 * C_RANK)))
    wb = jnp.concatenate([b_q, b_k, b_v], axis=1).astype(BF16)
    wa = jnp.concatenate([a_q, a_k, a_v, _pad_heads(d_v, D_KV_HEADS),
                          c_q * QK_SCALE, c_k, c_v, c_g, low], axis=1).astype(BF16)
    wd = jnp.concatenate([_pad_heads(d_q, D_Q_HEADS), _pad_heads(d_k, D_KV_HEADS)], axis=1).astype(BF16)
    assert wb.shape[1] == UB_W and wa.shape[1] == UA_W and wd.shape[1] == UD_W
    return wb, wa, wd


def _prep_w_out(w):
    wa = w[0:768]
    wb = w[768:1536]
    wc = w[1536:2304]
    wd = w[2304:3072]
    n = w.shape[1]
    wd = jnp.pad(wd.reshape(D_Q_HEADS, HEAD_DIM, n), ((0, 0), (0, LANE - HEAD_DIM), (0, 0))).reshape(D_PAD_W, n)
    return [t.astype(BF16) for t in (wa, wb, wc, wd)]


def _q_col_scale(width, q_lo, q_hi):
    col = jnp.arange(width)
    return jnp.where((col >= q_lo) & (col < q_hi), QK_SCALE * LOG2E, 1.0).astype(F32).reshape(1, -1)


def _rope_tables(L):
    t = jnp.arange(L, dtype=F32)
    lane = jnp.arange(LANE)
    l64 = lane % HEAD_DIM
    half = ROPE_DIMS // 2
    inv = ROPE_THETA ** (-jnp.arange(0, ROPE_DIMS, 2, dtype=F32) / ROPE_DIMS)
    ang = t[:, None] * inv[None, :]
    ang_l = ang[:, l64 % half]
    in_rot = (l64 < ROPE_DIMS)[None, :]
    c8 = jnp.where(in_rot, jnp.cos(ang_l), 1.0)
    sa8 = jnp.where(((l64 >= half) & (l64 < ROPE_DIMS))[None, :], jnp.sin(ang_l), 0.0)
    sb8 = jnp.where((l64 < half)[None, :], -jnp.sin(ang_l), 0.0)

    q = HEAD_DIM // 4
    inv2 = AXIAL_THETA ** (-jnp.arange(0, HEAD_DIM // 2, 2, dtype=F32) / (HEAD_DIM // 2))
    rows = L // GRID_W
    row_pos = jnp.repeat(jnp.arange(rows, dtype=F32), GRID_W)
    col_pos = jnp.tile(jnp.arange(GRID_W, dtype=F32), rows)
    ang_r = row_pos[:, None] * inv2[None, :]
    ang_c = col_pos[:, None] * inv2[None, :]
    ang_x = jnp.where((l64 < 2 * q)[None, :], ang_r[:, l64 % q], ang_c[:, l64 % q])
    real = (lane < HEAD_DIM)[None, :]
    cx = jnp.where(real, jnp.cos(ang_x), 0.0)
    upper = ((l64 % (2 * q)) >= q)[None, :]
    sax = jnp.where(real & upper, jnp.sin(ang_x), 0.0)
    sbx = jnp.where(real & ~upper, -jnp.sin(ang_x), 0.0)
    one, zero = jnp.ones_like(c8), jnp.zeros_like(c8)
    rope8 = [jnp.stack([c8, one]), jnp.stack([sa8, zero]), jnp.stack([sb8, zero])]
    axial = [jnp.stack([cx, one]), jnp.stack([sax, zero]), jnp.stack([sbx, zero])]
    return rope8, axial


def _prep_layer(l, ffn1_norm, ffn1_w_gate, ffn1_w_up, ffn1_w_down, mix_norm, w_in, w_out,
                diff_lambda_q1, diff_lambda_k1, diff_lambda_q2, diff_lambda_k2, diff_out_norm,
                gla_gate_up_f, gla_gate_bias_f, gla_gate_up_b, gla_gate_bias_b, gla_out_norm,
                gqa_q_norm, gqa_k_norm, ffn2_norm, ffn2_w_gate, ffn2_w_up, ffn2_w_down):
    def row(v):
        return v.astype(F32).reshape(1, -1)

    zeros64 = jnp.zeros((HEAD_DIM,), F32)
    gq = jnp.tile(jnp.concatenate([gqa_q_norm[l].astype(F32) * (QK_SCALE * LOG2E), zeros64]), D_Q_HEADS)
    gk = jnp.tile(jnp.concatenate([gqa_k_norm[l].astype(F32), zeros64]), D_KV_HEADS)
    wup_f = jnp.zeros((2 * LANE, C_HEADS * HEAD_DIM), F32).at[0:C_RANK].set(gla_gate_up_f[l])
    wup_b = jnp.zeros((2 * LANE, C_HEADS * HEAD_DIM), F32).at[C_RANK:2 * C_RANK].set(gla_gate_up_b[l])
    return dict(
        n1=row(ffn1_norm[l]), wg1=ffn1_w_gate[l].astype(BF16), wu1=ffn1_w_up[l].astype(BF16),
        wd1=ffn1_w_down[l].astype(BF16),
        nmix=row(mix_norm[l]), w_in=_prep_w_in(w_in[l]), w_out=_prep_w_out(w_out[l]),
        gd=jnp.concatenate([gq, gk]).reshape(1, -1),
        lam=jnp.stack([diff_lambda_q1[l], diff_lambda_k1[l], diff_lambda_q2[l], diff_lambda_k2[l]]).astype(F32),
        lam_init=0.8 - 0.6 * math.exp(-0.3 * l),
        gdiff=row(diff_out_norm[l]),
        wup_f=wup_f.astype(BF16), bias_f=row(gla_gate_bias_f[l]),
        wup_b=wup_b.astype(BF16), bias_b=row(gla_gate_bias_b[l]),
        ggla=row(gla_out_norm[l]),
        n2=row(ffn2_norm[l]), wg2=ffn2_w_gate[l].astype(BF16), wu2=ffn2_w_up[l].astype(BF16),
        wd2=ffn2_w_down[l].astype(BF16),
    )


def _in_projections(h, p, tabs, L):
    rope8, axial = tabs
    w_b, w_a, w_d = p["w_in"]
    half = ROPE_DIMS // 2
    ub = _proj(h, w_b, _q_col_scale(UB_W, UB_Q, UB_K), rope8, L, out_dtype=F32,
               n_rot_tiles=UB_ROPE_TILES, shift=half, head_norm=False, tn=PROJ_TN, name="inproj_b")
    ua = _proj(h, w_a, _q_col_scale(UA_W, UA_AQ, UA_AK), rope8, L, out_dtype=BF16,
               n_rot_tiles=UA_ROPE_TILES, shift=half, head_norm=False, tn=PROJ_TN, name="inproj_a")
    ud = _proj(h, w_d, p["gd"], axial, L, out_dtype=BF16, n_rot_tiles=UD_W // 1024,
               shift=HEAD_DIM // 4, head_norm=True, tn=1024, name="inproj_d")
    return ub, ua, ud


def _trunk(x, layers, final_g):
    B, L, D = x.shape
    xt = x.reshape(B * L, D)
    tabs = _rope_tables(L)
    for l, p in enumerate(layers):
        xt, h = _ffn(xt, p["n1"], p["wg1"], p["wu1"], p["wd1"], p["nmix"], tail="next_norm")
        ub, ua, ud = _in_projections(h, p, tabs, L)
        ua3 = ua.reshape(B, L, UA_W)
        oa = _diff_attention(ua3, p["lam"], p["gdiff"], p["lam_init"]).reshape(B * L, -1)
        ob = _dilated_attention(ub.reshape(B, L, UB_W)).reshape(B * L, -1)
        oc = _gla(ua3, p["wup_f"], p["bias_f"], p["wup_b"], p["bias_b"], p["ggla"]).reshape(B * L, -1)
        od = _gqa_attention(ud.reshape(B, L, UD_W), ua3).reshape(B * L, -1)
        xt = _outproj(xt, oa, ob, oc, od, *p["w_out"])
        last = l == len(layers) - 1
        xt = _ffn(xt, p["n2"], p["wg2"], p["wu2"], p["wd2"], final_g, tail="final_norm" if last else "plain")
    return xt.reshape(B, L, D)


def kernel(x_prompt, x_sample, ffn1_norm, ffn1_w_gate, ffn1_w_up, ffn1_w_down, mix_norm, w_in, w_out, diff_lambda_q1, diff_lambda_k1, diff_lambda_q2, diff_lambda_k2, diff_out_norm, gla_gate_up_f, gla_gate_bias_f, gla_gate_up_b, gla_gate_bias_b, gla_out_norm, gqa_q_norm, gqa_k_norm, ffn2_norm, ffn2_w_gate, ffn2_w_up, ffn2_w_down, final_norm):
    depth = w_in.shape[0]
    layers = [_prep_layer(l, ffn1_norm, ffn1_w_gate, ffn1_w_up, ffn1_w_down, mix_norm, w_in, w_out,
                          diff_lambda_q1, diff_lambda_k1, diff_lambda_q2, diff_lambda_k2, diff_out_norm,
                          gla_gate_up_f, gla_gate_bias_f, gla_gate_up_b, gla_gate_bias_b, gla_out_norm,
                          gqa_q_norm, gqa_k_norm, ffn2_norm, ffn2_w_gate, ffn2_w_up, ffn2_w_down)
              for l in range(depth)]
    final_g = final_norm.astype(F32).reshape(1, -1)
    return (_trunk(x_prompt, layers, final_g), _trunk(x_sample, layers, final_g))
```

```python
import functools
import math

import jax
import jax.numpy as jnp
from jax import lax
from jax.experimental import pallas as pl
from jax.experimental.pallas import tpu as pltpu

F32 = jnp.float32
BF16 = jnp.bfloat16

HEAD_DIM = 64
EPS = 1e-6
ROPE_THETA = 500000.0
ROPE_DIMS = HEAD_DIM // 4
AXIAL_THETA = 10000.0
GRID_W = 64
A_HEADS = 6
B_HEADS = 12
B_PATTERNS = ((128, 1), (512, 4), (2048, 16))
B_SIDE = 64
C_HEADS = 6
C_RANK = 16
C_CHUNK = 64
C_GATE_NORM = 16.0
D_Q_HEADS = 12
D_KV_HEADS = 4
D_GROUP = D_Q_HEADS // D_KV_HEADS
QK_SCALE = HEAD_DIM ** -0.5
LOG2E = math.log2(math.e)

LANE = 128
NEG = -1e30

PROJ_TN = 768
UB_Q = 0
UB_K = 768
UB_V = 1536
UB_W = 2304
UB_ROPE_TILES = 2
UA_AQ = 0
UA_AK = 768
UA_AV = 1536
UA_DV = 2304
UA_CQ = 2816
UA_CK = 3200
UA_CV = 3584
UA_CG = 4352
UA_CLOW = 5120
UA_W = 5376
UA_ROPE_TILES = 2
UD_Q = 0
UD_K = 1536
UD_W = 2048
D_PAD_W = D_Q_HEADS * LANE


def _cparams(sem, vmem_mb=56):
    return pltpu.CompilerParams(dimension_semantics=sem, vmem_limit_bytes=vmem_mb * 1024 * 1024)


def _dot(a, b):
    return jnp.dot(a, b, preferred_element_type=F32)


def _dot_nt(a, b):
    return lax.dot_general(a, b, (((1,), (1,)), ((), ())), preferred_element_type=F32)


def _rms(x, g):
    ms = jnp.mean(x * x, axis=-1, keepdims=True)
    return x * lax.rsqrt(ms + EPS) * g


def _ffn_body(x_ref, hg_ref, wg_ref, wu_ref, wd_ref, ng_ref, *rest, tail, h_given):
    rest = list(rest)
    o_ref = rest.pop(0)
    hn_ref = rest.pop(0) if tail == "next_norm" else None
    h_ref = hg_ref if h_given else rest.pop(0)
    acc_ref = rest.pop(0)
    j = pl.program_id(1)
    nj = pl.num_programs(1)

    @pl.when(j == 0)
    def _():
        if not h_given:
            h_ref[...] = _rms(x_ref[...], hg_ref[...]).astype(BF16)
        acc_ref[...] = jnp.zeros(acc_ref.shape, F32)

    h = h_ref[...]
    gate = _dot(h, wg_ref[...])
    up = _dot(h, wu_ref[...])
    act = (gate * (1.0 / (1.0 + jnp.exp(-gate))) * up).astype(BF16)
    acc_ref[...] += _dot(act, wd_ref[...])

    @pl.when(j == nj - 1)
    def _():
        y = x_ref[...] + 0.5 * acc_ref[...]
        if tail == "final_norm":
            y = _rms(y, ng_ref[...])
        o_ref[...] = y
        if tail == "next_norm":
            hn_ref[...] = _rms(y, ng_ref[...]).astype(BF16)


def _ffn(x, hg, wg, wu, wd, ng, *, tail, tm=512, tf=512):
    T, D = x.shape
    FF = wg.shape[1]
    tm = min(tm, T)
    tf = min(tf, FF)
    assert T % tm == 0 and FF % tf == 0
    h_given = hg.shape[0] == T
    row_spec = pl.BlockSpec((tm, D), lambda i, j: (i, 0))
    scratch = [] if h_given else [pltpu.VMEM((tm, D), BF16)]
    if tail == "next_norm":
        out_shape = [jax.ShapeDtypeStruct((T, D), F32), jax.ShapeDtypeStruct((T, D), BF16)]
        out_specs = [row_spec, row_spec]
    else:
        out_shape = jax.ShapeDtypeStruct((T, D), F32)
        out_specs = row_spec
    return pl.pallas_call(
        functools.partial(_ffn_body, tail=tail, h_given=h_given),
        out_shape=out_shape,
        grid=(T // tm, FF // tf),
        in_specs=[
            row_spec,
            row_spec if h_given else pl.BlockSpec((1, D), lambda i, j: (0, 0)),
            pl.BlockSpec((D, tf), lambda i, j: (0, j)),
            pl.BlockSpec((D, tf), lambda i, j: (0, j)),
            pl.BlockSpec((tf, D), lambda i, j: (j, 0)),
            pl.BlockSpec((1, D), lambda i, j: (0, 0)),
        ],
        out_specs=out_specs,
        scratch_shapes=scratch + [pltpu.VMEM((tm, D), F32)],
        compiler_params=_cparams(("parallel", "arbitrary")),
        name="ffn",
    )(x, hg, wg, wu, wd, ng)


def _rot(x, c, sa, sb, shift):
    return x * c + pltpu.roll(x, shift, 1) * sa + pltpu.roll(x, LANE - shift, 1) * sb


def _proj_body(h_ref, w_ref, cs_ref, c_ref, sa_ref, sb_ref, o_ref, *, tr, shift, head_norm):
    tm, tn = o_ref.shape
    for rc in range(tm // tr):
        rows = slice(rc * tr, (rc + 1) * tr)
        acc = _dot(h_ref[rows, :], w_ref[...])
        c_t, sa_t, sb_t = c_ref[rows, :], sa_ref[rows, :], sb_ref[rows, :]
        for c in range(tn // LANE):
            sl = slice(c * LANE, (c + 1) * LANE)
            y = acc[:, sl]
            if head_norm:
                ms = jnp.sum(y * y, axis=-1, keepdims=True) * (1.0 / HEAD_DIM)
                y = y * lax.rsqrt(ms + EPS) * cs_ref[:, sl]
                y = _rot(y, c_t, sa_t, sb_t, shift)
            else:
                y = _rot(y, c_t, sa_t, sb_t, shift) * cs_ref[:, sl]
            o_ref[rows, sl] = y.astype(o_ref.dtype)


def _proj(h, w, cs, tabs, L, *, out_dtype, n_rot_tiles, shift, head_norm, tn, name, tm=1024, tr=256):
    T, D = h.shape
    W = w.shape[1]
    tm = min(tm, L)
    tr = min(tr, tm)
    assert T % tm == 0 and L % tm == 0 and W % tn == 0 and tm % tr == 0
    lt = L // tm
    tab_spec = pl.BlockSpec((None, tm, LANE), lambda i, j: (jnp.where(j < n_rot_tiles, 0, 1), i % lt, 0))
    return pl.pallas_call(
        functools.partial(_proj_body, tr=tr, shift=shift, head_norm=head_norm),
        out_shape=jax.ShapeDtypeStruct((T, W), out_dtype),
        grid=(T // tm, W // tn),
        in_specs=[
            pl.BlockSpec((tm, D), lambda i, j: (i, 0)),
            pl.BlockSpec((D, tn), lambda i, j: (0, j)),
            pl.BlockSpec((1, tn), lambda i, j: (0, j)),
            tab_spec, tab_spec, tab_spec,
        ],
        out_specs=pl.BlockSpec((tm, tn), lambda i, j: (i, j)),
        compiler_params=_cparams(("parallel", "parallel")),
        name=name,
    )(h, w, cs, *tabs)


def _flash_init(m_ref, acc_ref):
    m_ref[...] = jnp.full(m_ref.shape, NEG, F32)
    acc_ref[...] = jnp.zeros(acc_ref.shape, F32)


def _flash_step(qq_ref, k_ref, v, m_ref, acc_ref, l_ref=None):
    s = _dot_nt(qq_ref[...], k_ref[...])
    chunks = [s[:, c * LANE:(c + 1) * LANE] for c in range(s.shape[1] // LANE)]
    mx = chunks[0]
    for sc in chunks[1:]:
        mx = jnp.maximum(mx, sc)
    m_prev = m_ref[...]
    m_new = jnp.maximum(m_prev, jnp.max(mx, axis=-1, keepdims=True))
    alpha = jnp.exp2(m_prev - m_new)
    ps = [jnp.exp2(sc - m_new) for sc in chunks]
    if l_ref is not None:
        lsum = ps[0]
        for pc in ps[1:]:
            lsum = lsum + pc
        l_ref[...] = alpha * l_ref[...] + lsum
    p = jnp.concatenate([pc.astype(BF16) for pc in ps], axis=1)
    acc_ref[...] = alpha * acc_ref[...] + _dot(p, v)
    m_ref[...] = m_new


def _flash_out(acc_ref, l_ref=None, ones_lane=None):
    acc = acc_ref[...]
    if l_ref is not None:
        den = jnp.sum(l_ref[...], axis=-1, keepdims=True)
    else:
        lane = lax.broadcasted_iota(jnp.int32, acc.shape, 1)
        den = jnp.sum(jnp.where(lane == ones_lane, acc, 0.0), axis=-1, keepdims=True)
    return acc * (1.0 / den)


def _diff_body(q_ref, k_ref, v_ref, lam_ref, gn_ref, o_ref, qq_ref, m_ref, l_ref, acc_ref, *, tq, lam_init):
    ki = pl.program_id(3)

    @pl.when(ki == 0)
    def _():
        q = q_ref[...]
        lane = lax.broadcasted_iota(jnp.int32, q.shape, 1)
        zero = jnp.zeros_like(q)
        qq_ref[0:tq, :] = jnp.where(lane < HEAD_DIM, q, zero)
        qq_ref[tq:2 * tq, :] = jnp.where(lane >= HEAD_DIM, q, zero)
        _flash_init(m_ref, acc_ref)
        l_ref[...] = jnp.zeros(l_ref.shape, F32)

    _flash_step(qq_ref, k_ref, v_ref[...], m_ref, acc_ref, l_ref)

    @pl.when(ki == pl.num_programs(3) - 1)
    def _():
        o = _flash_out(acc_ref, l_ref=l_ref)
        lv = lam_ref[...]
        lam = (jnp.exp(jnp.sum(lv[0:1] * lv[1:2], axis=-1, keepdims=True))
               - jnp.exp(jnp.sum(lv[2:3] * lv[3:4], axis=-1, keepdims=True)) + lam_init)
        a = o[0:tq] - lam * o[tq:2 * tq]
        o_ref[...] = (_rms(a, gn_ref[...]) * (1.0 - lam_init)).astype(BF16)


def _diff_attention(u3, lam_vecs, gn, lam_init, *, tq=1024, tk=2048):
    B, L, _ = u3.shape
    tq = min(tq, L)
    tk = min(tk, L)
    qb, kb, vb = UA_AQ // LANE, UA_AK // LANE, UA_AV // LANE
    return pl.pallas_call(
        functools.partial(_diff_body, tq=tq, lam_init=lam_init),
        out_shape=jax.ShapeDtypeStruct((B, L, A_HEADS * LANE), BF16),
        grid=(B, A_HEADS, L // tq, L // tk),
        in_specs=[
            pl.BlockSpec((None, tq, LANE), lambda b, h, qi, ki: (b, qi, qb + h)),
            pl.BlockSpec((None, tk, LANE), lambda b, h, qi, ki: (b, ki, kb + h)),
            pl.BlockSpec((None, tk, LANE), lambda b, h, qi, ki: (b, ki, vb + h)),
            pl.BlockSpec((4, HEAD_DIM), lambda b, h, qi, ki: (0, 0)),
            pl.BlockSpec((1, LANE), lambda b, h, qi, ki: (0, 0)),
        ],
        out_specs=pl.BlockSpec((None, tq, LANE), lambda b, h, qi, ki: (b, qi, h)),
        scratch_shapes=[
            pltpu.VMEM((2 * tq, LANE), BF16),
            pltpu.VMEM((2 * tq, LANE), F32),
            pltpu.VMEM((2 * tq, LANE), F32),
            pltpu.VMEM((2 * tq, LANE), F32),
        ],
        compiler_params=_cparams(("parallel", "parallel", "parallel", "arbitrary")),
        name="diff_attn",
    )(u3, u3, u3, lam_vecs, gn)


def _gqa_body(q_ref, k_ref, v_ref, o_ref, qq_ref, m_ref, acc_ref, *, tq):
    ki = pl.program_id(3)

    @pl.when(ki == 0)
    def _():
        for j in range(D_GROUP):
            qq_ref[j * tq:(j + 1) * tq, :] = q_ref[:, j * LANE:(j + 1) * LANE]
        _flash_init(m_ref, acc_ref)

    v = v_ref[...]
    lane_v = lax.broadcasted_iota(jnp.int32, v.shape, 1)
    v = jnp.where(lane_v == HEAD_DIM, jnp.ones_like(v), v)
    _flash_step(qq_ref, k_ref, v, m_ref, acc_ref)

    @pl.when(ki == pl.num_programs(3) - 1)
    def _():
        o = _flash_out(acc_ref, ones_lane=HEAD_DIM)
        lane = lax.broadcasted_iota(jnp.int32, o.shape, 1)
        o = jnp.where(lane < HEAD_DIM, o, 0.0).astype(BF16)
        for j in range(D_GROUP):
            o_ref[:, j * LANE:(j + 1) * LANE] = o[j * tq:(j + 1) * tq]


def _gqa_attention(ud3, ua3, *, tq=1024, tk=2048):
    B, L, _ = ud3.shape
    tq = min(tq, L)
    tk = min(tk, L)
    gw = D_GROUP * LANE
    qb, kb, vb = UD_Q // gw, UD_K // LANE, UA_DV // LANE
    return pl.pallas_call(
        functools.partial(_gqa_body, tq=tq),
        out_shape=jax.ShapeDtypeStruct((B, L, D_PAD_W), BF16),
        grid=(B, D_KV_HEADS, L // tq, L // tk),
        in_specs=[
            pl.BlockSpec((None, tq, gw), lambda b, g, qi, ki: (b, qi, qb + g)),
            pl.BlockSpec((None, tk, LANE), lambda b, g, qi, ki: (b, ki, kb + g)),
            pl.BlockSpec((None, tk, LANE), lambda b, g, qi, ki: (b, ki, vb + g)),
        ],
        out_specs=pl.BlockSpec((None, tq, gw), lambda b, g, qi, ki: (b, qi, g)),
        scratch_shapes=[
            pltpu.VMEM((D_GROUP * tq, LANE), BF16),
            pltpu.VMEM((D_GROUP * tq, LANE), F32),
            pltpu.VMEM((D_GROUP * tq, LANE), F32),
        ],
        compiler_params=_cparams(("parallel", "parallel", "parallel", "arbitrary")),
        name="gqa_attn",
    )(ud3, ud3, ua3)


def _dil_body(q_ref, k_ref, kp_ref, kn_ref, v_ref, vp_ref, vn_ref, o_ref, kbuf, vbuf, m_ref, l_ref, acc_ref,
              *, TT, HALO, L):
    i = pl.program_id(2)
    H = B_SIDE
    kbuf[0:HALO, :] = kp_ref[...]
    kbuf[HALO:HALO + TT, :] = k_ref[...]
    kbuf[HALO + TT:HALO + TT + HALO, :] = kn_ref[...]
    vbuf[0:HALO, :] = vp_ref[...]
    vbuf[HALO:HALO + TT, :] = v_ref[...]
    vbuf[HALO + TT:HALO + TT + HALO, :] = vn_ref[...]

    tq = LANE
    tw = tq + 2 * H
    head0 = lax.broadcasted_iota(jnp.int32, (tq, LANE), 1) < HEAD_DIM
    row = lax.broadcasted_iota(jnp.int32, (2 * tq, tw), 0)
    col = lax.broadcasted_iota(jnp.int32, (2 * tq, tw), 1)
    band = jnp.abs(col - H - (row % tq)) <= H

    def per_head(x):
        return jnp.where(head0, jnp.broadcast_to(x[0:tq], (tq, LANE)), jnp.broadcast_to(x[tq:2 * tq], (tq, LANE)))

    for p, (_, d) in enumerate(reversed(B_PATTERNS)):
        n_sub = L // d

        def tile(idx, carry, d=d, p=p, n_sub=n_sub):
            r = idx % d
            j = idx // d
            q_start = r + d * (j * tq)
            k_start = HALO + r + d * (j * tq - H)
            if d == 1:
                q_rows, k_rows = pl.ds(q_start, tq), pl.ds(k_start, tw)
            else:
                q_rows, k_rows = pl.ds(q_start, tq, stride=d), pl.ds(k_start, tw, stride=d)
            q = q_ref[q_rows, :].astype(BF16)
            zero = jnp.zeros_like(q)
            qq = jnp.concatenate([jnp.where(head0, q, zero), jnp.where(head0, zero, q)], axis=0)
            kw = kbuf[k_rows, :].astype(BF16)
            vw = vbuf[k_rows, :].astype(BF16)
            n0 = (i * TT) // d + j * tq - H
            valid = band & (col >= -n0) & (col < n_sub - n0)
            s = jnp.where(valid, _dot_nt(qq, kw), NEG)
            m = jnp.max(s, axis=-1, keepdims=True)
            e = jnp.exp2(s - m)
            m_t = per_head(m)
            l_t = per_head(jnp.sum(e, axis=-1, keepdims=True))
            pv = _dot(e.astype(BF16), vw)
            pv_t = jnp.where(head0, pv[0:tq], pv[tq:2 * tq])
            if p == 0:
                m_ref[q_rows, :] = m_t
                l_ref[q_rows, :] = l_t
                acc_ref[q_rows, :] = pv_t
            else:
                m_old = m_ref[q_rows, :]
                m_new = jnp.maximum(m_old, m_t)
                a_old = jnp.exp2(m_old - m_new)
                a_t = jnp.exp2(m_t - m_new)
                m_ref[q_rows, :] = m_new
                l_ref[q_rows, :] = a_old * l_ref[q_rows, :] + a_t * l_t
                acc_ref[q_rows, :] = a_old * acc_ref[q_rows, :] + a_t * pv_t
            return carry

        lax.fori_loop(0, TT // tq, tile, 0, unroll=8)

    o_ref[...] = (acc_ref[...] * (1.0 / l_ref[...])).astype(BF16)


B_TILE = 2048
B_HALO = 1024


def _dilated_attention(ub3):
    B, L, _ = ub3.shape
    TT, HALO = B_TILE, B_HALO
    assert L % TT == 0 and TT % HALO == 0 and HALO >= B_SIDE * max(d for _, d in B_PATTERNS)
    pairs = B_HEADS // 2
    qb, kb, vb = UB_Q // LANE, UB_K // LANE, UB_V // LANE
    th = TT // HALO
    nh = L // HALO

    def main(base):
        return pl.BlockSpec((None, TT, LANE), lambda b, hp, i: (b, i, base + hp))

    def prev(base):
        return pl.BlockSpec((None, HALO, LANE), lambda b, hp, i: (b, jnp.maximum(i * th - 1, 0), base + hp))

    def nxt(base):
        return pl.BlockSpec((None, HALO, LANE), lambda b, hp, i: (b, jnp.minimum((i + 1) * th, nh - 1), base + hp))

    return pl.pallas_call(
        functools.partial(_dil_body, TT=TT, HALO=HALO, L=L),
        out_shape=jax.ShapeDtypeStruct((B, L, pairs * LANE), BF16),
        grid=(B, pairs, L // TT),
        in_specs=[main(qb), main(kb), prev(kb), nxt(kb), main(vb), prev(vb), nxt(vb)],
        out_specs=pl.BlockSpec((None, TT, LANE), lambda b, hp, i: (b, i, hp)),
        scratch_shapes=[pltpu.VMEM((TT + 2 * HALO, LANE), F32), pltpu.VMEM((TT + 2 * HALO, LANE), F32),
                        pltpu.VMEM((TT, LANE), F32), pltpu.VMEM((TT, LANE), F32), pltpu.VMEM((TT, LANE), F32)],
        compiler_params=_cparams(("parallel", "parallel", "parallel")),
        name="dilated_attn",
    )(ub3, ub3, ub3, ub3, ub3, ub3, ub3)


def _gla_body(*refs, R, backward):
    if backward:
        (q_ref, k_ref, v_ref, low_ref, wup_ref, bias_ref, fwd_ref, cg_ref, gn_ref, o_ref, s_ref) = refs
    else:
        (q_ref, k_ref, v_ref, low_ref, wup_ref, bias_ref, o_ref, s_ref) = refs
    C = C_CHUNK
    n = R // C

    @pl.when(pl.program_id(2) == 0)
    def _():
        s_ref[...] = jnp.zeros(s_ref.shape, F32)

    x = _dot(low_ref[...], wup_ref[...]) + bias_ref[...]
    gl = (jnp.minimum(x, 0.0) - jnp.log(1.0 + jnp.exp(-jnp.abs(x)))) * (1.0 / C_GATE_NORM)

    r64 = lax.broadcasted_iota(jnp.int32, (C, C), 0)
    c64 = lax.broadcasted_iota(jnp.int32, (C, C), 1)
    tri = (c64 >= r64) if backward else (c64 <= r64)
    tri_f = tri.astype(F32)
    lane = lax.broadcasted_iota(jnp.int32, (C, LANE), 1)
    qrow = lax.broadcasted_iota(jnp.int32, (C, LANE), 0)
    key = lane % C
    tri_wide = (key >= qrow) if backward else (key <= qrow)
    head0 = lane < HEAD_DIM
    lane2 = lax.broadcasted_iota(jnp.int32, (C, 2 * LANE), 1)
    vhead0 = lane2 < LANE
    rr = lax.broadcasted_iota(jnp.int32, (LANE, 2 * LANE), 0)
    cc = lax.broadcasted_iota(jnp.int32, (LANE, 2 * LANE), 1)
    blockdiag = (rr < HEAD_DIM) == (cc < LANE)
    eye = (lax.broadcasted_iota(jnp.int32, (LANE, LANE), 0) == lax.broadcasted_iota(jnp.int32, (LANE, LANE), 1))
    zpad = jnp.zeros((C, LANE), F32)
    zpad_v = jnp.zeros((C, 2 * LANE), BF16)

    chunk_rows = [slice(c * C, (c + 1) * C) for c in range(n)]
    cums = [jnp.dot(tri_f, gl[rows], preferred_element_type=F32, precision=lax.Precision.HIGHEST)
            for rows in chunk_rows]
    lasts = [(cum[0:1] if backward else cum[C - 1:C]) for cum in cums]
    qds, atts, intra, upds, dec_cols = [], [], [], [], []
    for rows, cum, last in zip(chunk_rows, cums, lasts):
        kc = k_ref[rows, :].astype(F32)
        qds.append((q_ref[rows, :].astype(F32) * jnp.exp(cum)).astype(BF16))
        kinv = kc * jnp.exp(-cum)
        zero = jnp.zeros_like(kinv)
        kbd = jnp.concatenate([jnp.where(head0, kinv, zero), jnp.where(head0, zero, kinv)], axis=0).astype(BF16)
        atts.append(jnp.where(tri_wide, _dot_nt(qds[-1], kbd), 0.0).astype(BF16))
        kdec = kc * jnp.exp(last - cum)
        kdec_t = jnp.concatenate([kdec, zpad], axis=0).T.astype(BF16)
        vc = v_ref[rows, :]
        upd = _dot(kdec_t, jnp.concatenate([vc, zpad_v], axis=0))
        upds.append(jnp.where(blockdiag, upd, 0.0))
        dec_row = jnp.broadcast_to(jnp.exp(last), (LANE, LANE))
        dec_cols.append(jnp.sum(jnp.where(eye, dec_row, 0.0), axis=-1, keepdims=True))
    for rows, att in zip(chunk_rows, atts):
        vc = v_ref[rows, :]
        zv = jnp.zeros_like(vc)
        vbd = jnp.concatenate([jnp.where(vhead0, vc, zv), jnp.where(vhead0, zv, vc)], axis=0)
        intra.append(_dot(att, vbd))

    state = s_ref[...]
    entering = [None] * n
    for c in (range(n - 1, -1, -1) if backward else range(n)):
        entering[c] = state.astype(BF16)
        state = dec_cols[c] * state + upds[c]
    s_ref[...] = state

    if backward:
        gate = cg_ref[...].astype(F32)
        gate = gate * (1.0 / (1.0 + jnp.exp(-gate)))
    for c, rows in enumerate(chunk_rows):
        o = intra[c] + _dot(qds[c], entering[c])
        if backward:
            tot = o + fwd_ref[rows, :]
            for h in range(2):
                hs = slice(h * LANE, (h + 1) * LANE)
                o_ref[rows, hs] = (_rms(tot[:, hs], gn_ref[...]) * gate[rows, hs]).astype(BF16)
        else:
            o_ref[rows, :] = o


def _gla(u3, wup_f, bias_f, wup_b, bias_b, gn, *, R=512):
    B, L, _ = u3.shape
    R = min(R, L)
    nb = L // R
    pairs = C_HEADS // 2
    W = C_HEADS * LANE
    qb, kb = UA_CQ // LANE, UA_CK // LANE
    vb, gb, lb = UA_CV // (2 * LANE), UA_CG // (2 * LANE), UA_CLOW // (2 * LANE)

    def specs(rev):
        def ri(i):
            return nb - 1 - i if rev else i
        return dict(
            q=pl.BlockSpec((None, R, LANE), lambda b, p, i: (b, ri(i), qb + p)),
            k=pl.BlockSpec((None, R, LANE), lambda b, p, i: (b, ri(i), kb + p)),
            v=pl.BlockSpec((None, R, 2 * LANE), lambda b, p, i: (b, ri(i), vb + p)),
            low=pl.BlockSpec((None, R, 2 * LANE), lambda b, p, i: (b, ri(i), lb)),
            wup=pl.BlockSpec((2 * LANE, LANE), lambda b, p, i: (0, p)),
            bias=pl.BlockSpec((1, LANE), lambda b, p, i: (0, p)),
            out=pl.BlockSpec((None, R, 2 * LANE), lambda b, p, i: (b, ri(i), p)),
            cg=pl.BlockSpec((None, R, 2 * LANE), lambda b, p, i: (b, ri(i), gb + p)),
            gn=pl.BlockSpec((1, LANE), lambda b, p, i: (0, 0)),
        )

    sem = _cparams(("parallel", "parallel", "arbitrary"))
    sf = specs(False)
    fwd = pl.pallas_call(
        functools.partial(_gla_body, R=R, backward=False),
        out_shape=jax.ShapeDtypeStruct((B, L, W), F32),
        grid=(B, pairs, nb),
        in_specs=[sf["q"], sf["k"], sf["v"], sf["low"], sf["wup"], sf["bias"]],
        out_specs=sf["out"],
        scratch_shapes=[pltpu.VMEM((LANE, 2 * LANE), F32)],
        compiler_params=sem,
        name="gla_fwd",
    )(u3, u3, u3, u3, wup_f, bias_f)
    sb = specs(True)
    return pl.pallas_call(
        functools.partial(_gla_body, R=R, backward=True),
        out_shape=jax.ShapeDtypeStruct((B, L, W), BF16),
        grid=(B, pairs, nb),
        in_specs=[sb["q"], sb["k"], sb["v"], sb["low"], sb["wup"], sb["bias"], sb["out"], sb["cg"], sb["gn"]],
        out_specs=sb["out"],
        scratch_shapes=[pltpu.VMEM((LANE, 2 * LANE), F32)],
        compiler_params=sem,
        name="gla_bwd",
    )(u3, u3, u3, u3, wup_b, bias_b, fwd, u3, gn)


def _outproj_body(x_ref, a_ref, b_ref, c_ref, d_ref, wa_ref, wb_ref, wc_ref, wd_ref, ng_ref, o_ref, hn_ref):
    acc = _dot(a_ref[...], wa_ref[...]) + _dot(b_ref[...], wb_ref[...])
    acc = acc + _dot(c_ref[...], wc_ref[...]) + _dot(d_ref[...], wd_ref[...])
    y = x_ref[...] + acc
    o_ref[...] = y
    hn_ref[...] = _rms(y, ng_ref[...]).astype(BF16)


def _outproj(x, oa, ob, oc, od, wa, wb, wc, wd, ng, *, tm=512):
    T, D = x.shape
    tm = min(tm, T)
    row_spec = pl.BlockSpec((tm, D), lambda i: (i, 0))

    def act(w):
        return pl.BlockSpec((tm, w), lambda i: (i, 0))

    def wt(w):
        return pl.BlockSpec((w, D), lambda i: (0, 0), pipeline_mode=pl.Buffered(1))

    widths = [oa.shape[1], ob.shape[1], oc.shape[1], od.shape[1]]
    return pl.pallas_call(
        _outproj_body,
        out_shape=[jax.ShapeDtypeStruct((T, D), F32), jax.ShapeDtypeStruct((T, D), BF16)],
        grid=(T // tm,),
        in_specs=[row_spec] + [act(w) for w in widths] + [wt(w) for w in widths]
        + [pl.BlockSpec((1, D), lambda i: (0, 0))],
        out_specs=[row_spec, row_spec],
        compiler_params=_cparams(("parallel",)),
        name="outproj",
    )(x, oa, ob, oc, od, wa, wb, wc, wd, ng)


def _pad_heads(w, n):
    k = w.shape[0]
    return jnp.pad(w.reshape(k, n, HEAD_DIM), ((0, 0), (0, 0), (0, LANE - HEAD_DIM))).reshape(k, n * LANE)


def _prep_w_in(w):
    sizes = (768, 768, 768, 768, 768, 768, 384, 384, 768, 768, 32, 768, 256, 256)
    offs = [0]
    for s in sizes:
        offs.append(offs[-1] + s)
    (a_q, a_k, a_v, b_q, b_k, b_v, c_q, c_k, c_v, c_g, c_low, d_q, d_k, d_v) = [
        w[:, offs[i]:offs[i + 1]] for i in range(len(sizes))]
    low = jnp.pad(c_low, ((0, 0), (0, 2 * LANE - 2 * C_RANK)))
    wb = jnp.concatenate([b_q, b_k, b_v], axis=1).astype(BF16)
    wa = jnp.concatenate([a_q, a_k, a_v, _pad_heads(d_v, D_KV_HEADS),
                          c_q * QK_SCALE, c_k, c_v, c_g, low], axis=1).astype(BF16)
    wd = jnp.concatenate([_pad_heads(d_q, D_Q_HEADS), _pad_heads(d_k, D_KV_HEADS)], axis=1).astype(BF16)
    assert wb.shape[1] == UB_W and wa.shape[1] == UA_W and wd.shape[1] == UD_W
    return wb, wa, wd


def _prep_w_out(w):
    wa = w[0:768]
    wb = w[768:1536]
    wc = w[1536:2304]
    wd = w[2304:3072]
    n = w.shape[1]
    wd = jnp.pad(wd.reshape(D_Q_HEADS, HEAD_DIM, n), ((0, 0), (0, LANE - HEAD_DIM), (0, 0))).reshape(D_PAD_W, n)
    return [t.astype(BF16) for t in (wa, wb, wc, wd)]


def _q_col_scale(width, q_lo, q_hi):
    col = jnp.arange(width)
    return jnp.where((col >= q_lo) & (col < q_hi), QK_SCALE * LOG2E, 1.0).astype(F32).reshape(1, -1)


def _rope_tables(L):
    t = jnp.arange(L, dtype=F32)
    lane = jnp.arange(LANE)
    l64 = lane % HEAD_DIM
    half = ROPE_DIMS // 2
    inv = ROPE_THETA ** (-jnp.arange(0, ROPE_DIMS, 2, dtype=F32) / ROPE_DIMS)
    ang = t[:, None] * inv[None, :]
    ang_l = ang[:, l64 % half]
    in_rot = (l64 < ROPE_DIMS)[None, :]
    c8 = jnp.where(in_rot, jnp.cos(ang_l), 1.0)
    sa8 = jnp.where(((l64 >= half) & (l64 < ROPE_DIMS))[None, :], jnp.sin(ang_l), 0.0)
    sb8 = jnp.where((l64 < half)[None, :], -jnp.sin(ang_l), 0.0)

    q = HEAD_DIM // 4
    inv2 = AXIAL_THETA ** (-jnp.arange(0, HEAD_DIM // 2, 2, dtype=F32) / (HEAD_DIM // 2))
    rows = L // GRID_W
    row_pos = jnp.repeat(jnp.arange(rows, dtype=F32), GRID_W)
    col_pos = jnp.tile(jnp.arange(GRID_W, dtype=F32), rows)
    ang_r = row_pos[:, None] * inv2[None, :]
    ang_c = col_pos[:, None] * inv2[None, :]
    ang_x = jnp.where((l64 < 2 * q)[None, :], ang_r[:, l64 % q], ang_c[:, l64 % q])
    real = (lane < HEAD_DIM)[None, :]
    cx = jnp.where(real, jnp.cos(ang_x), 0.0)
    upper = ((l64 % (2 * q)) >= q)[None, :]
    sax = jnp.where(real & upper, jnp.sin(ang_x), 0.0)
    sbx = jnp.where(real & ~upper, -jnp.sin(ang_x), 0.0)
    one, zero = jnp.ones_like(c8), jnp.zeros_like(c8)
    rope8 = [jnp.stack([c8, one]), jnp.stack([sa8, zero]), jnp.stack([sb8, zero])]
    axial = [jnp.stack([cx, one]), jnp.stack([sax, zero]), jnp.stack([sbx, zero])]
    return rope8, axial


def _prep_layer(l, ffn1_norm, ffn1_w_gate, ffn1_w_up, ffn1_w_down, mix_norm, w_in, w_out,
                diff_lambda_q1, diff_lambda_k1, diff_lambda_q2, diff_lambda_k2, diff_out_norm,
                gla_gate_up_f, gla_gate_bias_f, gla_gate_up_b, gla_gate_bias_b, gla_out_norm,
                gqa_q_norm, gqa_k_norm, ffn2_norm, ffn2_w_gate, ffn2_w_up, ffn2_w_down):
    def row(v):
        return v.astype(F32).reshape(1, -1)

    zeros64 = jnp.zeros((HEAD_DIM,), F32)
    gq = jnp.tile(jnp.concatenate([gqa_q_norm[l].astype(F32) * (QK_SCALE * LOG2E), zeros64]), D_Q_HEADS)
    gk = jnp.tile(jnp.concatenate([gqa_k_norm[l].astype(F32), zeros64]), D_KV_HEADS)
    wup_f = jnp.zeros((2 * LANE, C_HEADS * HEAD_DIM), F32).at[0:C_RANK].set(gla_gate_up_f[l])
    wup_b = jnp.zeros((2 * LANE, C_HEADS * HEAD_DIM), F32).at[C_RANK:2 * C_RANK].set(gla_gate_up_b[l])
    return dict(
        n1=row(ffn1_norm[l]), wg1=ffn1_w_gate[l].astype(BF16), wu1=ffn1_w_up[l].astype(BF16),
        wd1=ffn1_w_down[l].astype(BF16),
        nmix=row(mix_norm[l]), w_in=_prep_w_in(w_in[l]), w_out=_prep_w_out(w_out[l]),
        gd=jnp.concatenate([gq, gk]).reshape(1, -1),
        lam=jnp.stack([diff_lambda_q1[l], diff_lambda_k1[l], diff_lambda_q2[l], diff_lambda_k2[l]]).astype(F32),
        lam_init=0.8 - 0.6 * math.exp(-0.3 * l),
        gdiff=row(diff_out_norm[l]),
        wup_f=wup_f.astype(BF16), bias_f=row(gla_gate_bias_f[l]),
        wup_b=wup_b.astype(BF16), bias_b=row(gla_gate_bias_b[l]),
        ggla=row(gla_out_norm[l]),
        n2=row(ffn2_norm[l]), wg2=ffn2_w_gate[l].astype(BF16), wu2=ffn2_w_up[l].astype(BF16),
        wd2=ffn2_w_down[l].astype(BF16),
    )


def _in_projections(h, p, tabs, L):
    rope8, axial = tabs
    w_b, w_a, w_d = p["w_in"]
    half = ROPE_DIMS // 2
    ub = _proj(h, w_b, _q_col_scale(UB_W, UB_Q, UB_K), rope8, L, out_dtype=F32,
               n_rot_tiles=UB_ROPE_TILES, shift=half, head_norm=False, tn=PROJ_TN, name="inproj_b")
    ua = _proj(h, w_a, _q_col_scale(UA_W, UA_AQ, UA_AK), rope8, L, out_dtype=BF16,
               n_rot_tiles=UA_ROPE_TILES, shift=half, head_norm=False, tn=PROJ_TN, name="inproj_a")
    ud = _proj(h, w_d, p["gd"], axial, L, out_dtype=BF16, n_rot_tiles=UD_W // 1024,
               shift=HEAD_DIM // 4, head_norm=True, tn=1024, name="inproj_d")
    return ub, ua, ud


def _trunk(x, layers, final_g):
    B, L, D = x.shape
    xt = x.reshape(B * L, D)
    tabs = _rope_tables(L)
    h_ffn1 = None
    for l, p in enumerate(layers):
        xt, h = _ffn(xt, p["n1"] if h_ffn1 is None else h_ffn1, p["wg1"], p["wu1"], p["wd1"], p["nmix"],
                     tail="next_norm")
        ub, ua, ud = _in_projections(h, p, tabs, L)
        ua3 = ua.reshape(B, L, UA_W)
        oa = _diff_attention(ua3, p["lam"], p["gdiff"], p["lam_init"]).reshape(B * L, -1)
        ob = _dilated_attention(ub.reshape(B, L, UB_W)).reshape(B * L, -1)
        oc = _gla(ua3, p["wup_f"], p["bias_f"], p["wup_b"], p["bias_b"], p["ggla"]).reshape(B * L, -1)
        od = _gqa_attention(ud.reshape(B, L, UD_W), ua3).reshape(B * L, -1)
        xt, h2 = _outproj(xt, oa, ob, oc, od, *p["w_out"], p["n2"])
        if l == len(layers) - 1:
            xt = _ffn(xt, h2, p["wg2"], p["wu2"], p["wd2"], final_g, tail="final_norm")
        else:
            xt, h_ffn1 = _ffn(xt, h2, p["wg2"], p["wu2"], p["wd2"], layers[l + 1]["n1"], tail="next_norm")
    return xt.reshape(B, L, D)


def kernel(x_prompt, x_sample, ffn1_norm, ffn1_w_gate, ffn1_w_up, ffn1_w_down, mix_norm, w_in, w_out, diff_lambda_q1, diff_lambda_k1, diff_lambda_q2, diff_lambda_k2, diff_out_norm, gla_gate_up_f, gla_gate_bias_f, gla_gate_up_b, gla_gate_bias_b, gla_out_norm, gqa_q_norm, gqa_k_norm, ffn2_norm, ffn2_w_gate, ffn2_w_up, ffn2_w_down, final_norm):
    depth = w_in.shape[0]
    layers = [_prep_layer(l, ffn1_norm, ffn1_w_gate, ffn1_w_up, ffn1_w_down, mix_norm, w_in, w_out,
                          diff_lambda_q1, diff_lambda_k1, diff_lambda_q2, diff_lambda_k2, diff_out_norm,
                          gla_gate_up_f, gla_gate_bias_f, gla_gate_up_b, gla_gate_bias_b, gla_out_norm,
                          gqa_q_norm, gqa_k_norm, ffn2_norm, ffn2_w_gate, ffn2_w_up, ffn2_w_down)
              for l in range(depth)]
    final_g = final_norm.astype(F32).reshape(1, -1)
    return (_trunk(x_prompt, layers, final_g), _trunk(x_sample, layers, final_g))
```

```python
import functools
import math

import jax
import jax.numpy as jnp
from jax import lax
from jax.experimental import pallas as pl
from jax.experimental.pallas import tpu as pltpu

F32 = jnp.float32
BF16 = jnp.bfloat16

HEAD_DIM = 64
EPS = 1e-6
ROPE_THETA = 500000.0
ROPE_DIMS = HEAD_DIM // 4
AXIAL_THETA = 10000.0
GRID_W = 64
A_HEADS = 6
B_HEADS = 12
B_PATTERNS = ((128, 1), (512, 4), (2048, 16))
B_SIDE = 64
C_HEADS = 6
C_RANK = 16
C_CHUNK = 64
C_GATE_NORM = 16.0
D_Q_HEADS = 12
D_KV_HEADS = 4
D_GROUP = D_Q_HEADS // D_KV_HEADS
QK_SCALE = HEAD_DIM ** -0.5
LOG2E = math.log2(math.e)

LANE = 128
NEG = -1e30

PROJ_TN = 768
UB_Q = 0
UB_K = 768
UB_V = 1536
UB_W = 2304
UB_ROPE_TILES = 2
UA_AQ = 0
UA_AK = 768
UA_AV = 1536
UA_DV = 2304
UA_CQ = 2816
UA_CK = 3200
UA_CV = 3584
UA_CG = 4352
UA_CLOW = 5120
UA_W = 5376
UA_ROPE_TILES = 2
UD_Q = 0
UD_K = 768
UD_W = 1024
D_HEAD_ORDER = tuple(D_GROUP * (2 * gp + half) + j
                     for gp in range(D_KV_HEADS // 2) for j in range(D_GROUP) for half in range(2))


def _cparams(sem, vmem_mb=56):
    return pltpu.CompilerParams(dimension_semantics=sem, vmem_limit_bytes=vmem_mb * 1024 * 1024)


def _dot(a, b):
    return jnp.dot(a, b, preferred_element_type=F32)


def _dot_nt(a, b):
    return lax.dot_general(a, b, (((1,), (1,)), ((), ())), preferred_element_type=F32)


def _rms(x, g):
    ms = jnp.mean(x * x, axis=-1, keepdims=True)
    return x * lax.rsqrt(ms + EPS) * g


def _ffn_body(x_ref, hg_ref, wg_ref, wu_ref, wd_ref, ng_ref, *rest, tail, h_given):
    rest = list(rest)
    o_ref = rest.pop(0)
    hn_ref = rest.pop(0) if tail == "next_norm" else None
    h_ref = hg_ref if h_given else rest.pop(0)
    acc_ref = rest.pop(0)
    j = pl.program_id(1)
    nj = pl.num_programs(1)

    @pl.when(j == 0)
    def _():
        if not h_given:
            h_ref[...] = _rms(x_ref[...], hg_ref[...]).astype(BF16)
        acc_ref[...] = jnp.zeros(acc_ref.shape, F32)

    h = h_ref[...]
    gate = _dot(h, wg_ref[...])
    up = _dot(h, wu_ref[...])
    act = (gate * (1.0 / (1.0 + jnp.exp(-gate))) * up).astype(BF16)
    acc_ref[...] += _dot(act, wd_ref[...])

    @pl.when(j == nj - 1)
    def _():
        y = x_ref[...] + 0.5 * acc_ref[...]
        if tail == "final_norm":
            y = _rms(y, ng_ref[...])
        o_ref[...] = y
        if tail == "next_norm":
            hn_ref[...] = _rms(y, ng_ref[...]).astype(BF16)


def _ffn(x, hg, wg, wu, wd, ng, *, tail, tm=512, tf=512):
    T, D = x.shape
    FF = wg.shape[1]
    tm = min(tm, T)
    tf = min(tf, FF)
    assert T % tm == 0 and FF % tf == 0
    h_given = hg.shape[0] == T
    row_spec = pl.BlockSpec((tm, D), lambda i, j: (i, 0))
    scratch = [] if h_given else [pltpu.VMEM((tm, D), BF16)]
    if tail == "next_norm":
        out_shape = [jax.ShapeDtypeStruct((T, D), F32), jax.ShapeDtypeStruct((T, D), BF16)]
        out_specs = [row_spec, row_spec]
    else:
        out_shape = jax.ShapeDtypeStruct((T, D), F32)
        out_specs = row_spec
    return pl.pallas_call(
        functools.partial(_ffn_body, tail=tail, h_given=h_given),
        out_shape=out_shape,
        grid=(T // tm, FF // tf),
        in_specs=[
            row_spec,
            row_spec if h_given else pl.BlockSpec((1, D), lambda i, j: (0, 0)),
            pl.BlockSpec((D, tf), lambda i, j: (0, j)),
            pl.BlockSpec((D, tf), lambda i, j: (0, j)),
            pl.BlockSpec((tf, D), lambda i, j: (j, 0)),
            pl.BlockSpec((1, D), lambda i, j: (0, 0)),
        ],
        out_specs=out_specs,
        scratch_shapes=scratch + [pltpu.VMEM((tm, D), F32)],
        compiler_params=_cparams(("parallel", "arbitrary")),
        name="ffn",
    )(x, hg, wg, wu, wd, ng)


def _rot(x, c, sa, sb, shift):
    return x * c + pltpu.roll(x, shift, 1) * sa + pltpu.roll(x, LANE - shift, 1) * sb


def _proj_body(h_ref, w_ref, cs_ref, c_ref, sa_ref, sb_ref, o_ref, *, tr, shift, head_norm):
    tm, tn = o_ref.shape
    for rc in range(tm // tr):
        rows = slice(rc * tr, (rc + 1) * tr)
        acc = _dot(h_ref[rows, :], w_ref[...])
        c_t, sa_t, sb_t = c_ref[rows, :], sa_ref[rows, :], sb_ref[rows, :]
        for c in range(tn // LANE):
            sl = slice(c * LANE, (c + 1) * LANE)
            y = acc[:, sl]
            if head_norm:
                low = lax.broadcasted_iota(jnp.int32, y.shape, 1) < HEAD_DIM
                sq = y * y
                ss_low = jnp.sum(jnp.where(low, sq, 0.0), axis=-1, keepdims=True)
                ss_high = jnp.sum(jnp.where(low, 0.0, sq), axis=-1, keepdims=True)
                ms = jnp.where(low, ss_low, ss_high) * (1.0 / HEAD_DIM)
                y = y * lax.rsqrt(ms + EPS) * cs_ref[:, sl]
                y = _rot(y, c_t, sa_t, sb_t, shift)
            else:
                y = _rot(y, c_t, sa_t, sb_t, shift) * cs_ref[:, sl]
            o_ref[rows, sl] = y.astype(o_ref.dtype)


def _proj(h, w, cs, tabs, L, *, out_dtype, n_rot_tiles, shift, head_norm, tn, name, tm=1024, tr=256):
    T, D = h.shape
    W = w.shape[1]
    tm = min(tm, L)
    tr = min(tr, tm)
    assert T % tm == 0 and L % tm == 0 and W % tn == 0 and tm % tr == 0
    lt = L // tm
    tab_spec = pl.BlockSpec((None, tm, LANE), lambda i, j: (jnp.where(j < n_rot_tiles, 0, 1), i % lt, 0))
    return pl.pallas_call(
        functools.partial(_proj_body, tr=tr, shift=shift, head_norm=head_norm),
        out_shape=jax.ShapeDtypeStruct((T, W), out_dtype),
        grid=(T // tm, W // tn),
        in_specs=[
            pl.BlockSpec((tm, D), lambda i, j: (i, 0)),
            pl.BlockSpec((D, tn), lambda i, j: (0, j)),
            pl.BlockSpec((1, tn), lambda i, j: (0, j)),
            tab_spec, tab_spec, tab_spec,
        ],
        out_specs=pl.BlockSpec((tm, tn), lambda i, j: (i, j)),
        compiler_params=_cparams(("parallel", "parallel")),
        name=name,
    )(h, w, cs, *tabs)


def _flash_init(m_ref, acc_ref):
    m_ref[...] = jnp.full(m_ref.shape, NEG, F32)
    acc_ref[...] = jnp.zeros(acc_ref.shape, F32)


def _flash_step(qq_ref, k_ref, vs, m_ref, acc_ref, l_ref=None):
    s = _dot_nt(qq_ref[...], k_ref[...])
    chunks = [s[:, c * LANE:(c + 1) * LANE] for c in range(s.shape[1] // LANE)]
    mx = chunks[0]
    for sc in chunks[1:]:
        mx = jnp.maximum(mx, sc)
    m_prev = m_ref[...]
    m_new = jnp.maximum(m_prev, jnp.max(mx, axis=-1, keepdims=True))
    alpha = jnp.exp2(m_prev - m_new)
    ps = [jnp.exp2(sc - m_new) for sc in chunks]
    if l_ref is not None:
        lsum = ps[0]
        for pc in ps[1:]:
            lsum = lsum + pc
        l_ref[...] = alpha * l_ref[...] + lsum
    p = jnp.concatenate([pc.astype(BF16) for pc in ps], axis=1)
    share = p.shape[0] // len(vs)
    for t, v in enumerate(vs):
        rows = slice(t * share, (t + 1) * share)
        acc_ref[rows, :] = alpha[rows] * acc_ref[rows, :] + _dot(p[rows], v)
    m_ref[...] = m_new


def _flash_out(acc_ref, l_ref=None, ones_lane=None):
    acc = acc_ref[...]
    if l_ref is not None:
        den = jnp.sum(l_ref[...], axis=-1, keepdims=True)
    else:
        lane = lax.broadcasted_iota(jnp.int32, acc.shape, 1)
        den = jnp.sum(jnp.where(lane == ones_lane, acc, 0.0), axis=-1, keepdims=True)
    return acc * (1.0 / den)


def _diff_body(q_ref, k_ref, v_ref, lam_ref, gn_ref, o_ref, qq_ref, m_ref, l_ref, acc_ref, *, tq, lam_init):
    ki = pl.program_id(3)

    @pl.when(ki == 0)
    def _():
        q = q_ref[...]
        lane = lax.broadcasted_iota(jnp.int32, q.shape, 1)
        zero = jnp.zeros_like(q)
        qq_ref[0:tq, :] = jnp.where(lane < HEAD_DIM, q, zero)
        qq_ref[tq:2 * tq, :] = jnp.where(lane >= HEAD_DIM, q, zero)
        _flash_init(m_ref, acc_ref)
        l_ref[...] = jnp.zeros(l_ref.shape, F32)

    _flash_step(qq_ref, k_ref, [v_ref[...]], m_ref, acc_ref, l_ref)

    @pl.when(ki == pl.num_programs(3) - 1)
    def _():
        o = _flash_out(acc_ref, l_ref=l_ref)
        lv = lam_ref[...]
        lam = (jnp.exp(jnp.sum(lv[0:1] * lv[1:2], axis=-1, keepdims=True))
               - jnp.exp(jnp.sum(lv[2:3] * lv[3:4], axis=-1, keepdims=True)) + lam_init)
        a = o[0:tq] - lam * o[tq:2 * tq]
        o_ref[...] = (_rms(a, gn_ref[...]) * (1.0 - lam_init)).astype(BF16)


def _diff_attention(u3, lam_vecs, gn, lam_init, *, tq=1024, tk=2048):
    B, L, _ = u3.shape
    tq = min(tq, L)
    tk = min(tk, L)
    qb, kb, vb = UA_AQ // LANE, UA_AK // LANE, UA_AV // LANE
    return pl.pallas_call(
        functools.partial(_diff_body, tq=tq, lam_init=lam_init),
        out_shape=jax.ShapeDtypeStruct((B, L, A_HEADS * LANE), BF16),
        grid=(B, A_HEADS, L // tq, L // tk),
        in_specs=[
            pl.BlockSpec((None, tq, LANE), lambda b, h, qi, ki: (b, qi, qb + h)),
            pl.BlockSpec((None, tk, LANE), lambda b, h, qi, ki: (b, ki, kb + h)),
            pl.BlockSpec((None, tk, LANE), lambda b, h, qi, ki: (b, ki, vb + h)),
            pl.BlockSpec((4, HEAD_DIM), lambda b, h, qi, ki: (0, 0)),
            pl.BlockSpec((1, LANE), lambda b, h, qi, ki: (0, 0)),
        ],
        out_specs=pl.BlockSpec((None, tq, LANE), lambda b, h, qi, ki: (b, qi, h)),
        scratch_shapes=[
            pltpu.VMEM((2 * tq, LANE), BF16),
            pltpu.VMEM((2 * tq, LANE), F32),
            pltpu.VMEM((2 * tq, LANE), F32),
            pltpu.VMEM((2 * tq, LANE), F32),
        ],
        compiler_params=_cparams(("parallel", "parallel", "parallel", "arbitrary")),
        name="diff_attn",
    )(u3, u3, u3, lam_vecs, gn)


def _gqa_body(q_ref, k_ref, v0_ref, v1_ref, o_ref, qq_ref, m_ref, acc_ref, *, tq):
    ki = pl.program_id(3)

    @pl.when(ki == 0)
    def _():
        for half in range(2):
            for j in range(D_GROUP):
                q = q_ref[:, j * LANE:(j + 1) * LANE]
                lane = lax.broadcasted_iota(jnp.int32, q.shape, 1)
                own = (lane >= HEAD_DIM) if half == 1 else (lane < HEAD_DIM)
                r0 = (half * D_GROUP + j) * tq
                qq_ref[r0:r0 + tq, :] = jnp.where(own, q, jnp.zeros_like(q))
        _flash_init(m_ref, acc_ref)

    def with_ones(v_ref):
        v = v_ref[...]
        lane_v = lax.broadcasted_iota(jnp.int32, v.shape, 1)
        return jnp.where(lane_v == HEAD_DIM, jnp.ones_like(v), v)

    _flash_step(qq_ref, k_ref, [with_ones(v0_ref), with_ones(v1_ref)], m_ref, acc_ref)

    @pl.when(ki == pl.num_programs(3) - 1)
    def _():
        o = _flash_out(acc_ref, ones_lane=HEAD_DIM)
        low = lax.broadcasted_iota(jnp.int32, (tq, LANE), 1) < HEAD_DIM
        for j in range(D_GROUP):
            even = o[j * tq:(j + 1) * tq]
            odd = pltpu.roll(o[(D_GROUP + j) * tq:(D_GROUP + j + 1) * tq], HEAD_DIM, 1)
            o_ref[:, j * LANE:(j + 1) * LANE] = jnp.where(low, even, odd).astype(BF16)


def _gqa_attention(ud3, ua3, *, tq=512, tk=2048):
    B, L, _ = ud3.shape
    tq = min(tq, L)
    tk = min(tk, L)
    gw = D_GROUP * LANE
    R = 2 * D_GROUP * tq
    qb, kb, vb = UD_Q // gw, UD_K // LANE, UA_DV // LANE
    return pl.pallas_call(
        functools.partial(_gqa_body, tq=tq),
        out_shape=jax.ShapeDtypeStruct((B, L, D_Q_HEADS * HEAD_DIM), BF16),
        grid=(B, D_KV_HEADS // 2, L // tq, L // tk),
        in_specs=[
            pl.BlockSpec((None, tq, gw), lambda b, gp, qi, ki: (b, qi, qb + gp)),
            pl.BlockSpec((None, tk, LANE), lambda b, gp, qi, ki: (b, ki, kb + gp)),
            pl.BlockSpec((None, tk, LANE), lambda b, gp, qi, ki: (b, ki, vb + 2 * gp)),
            pl.BlockSpec((None, tk, LANE), lambda b, gp, qi, ki: (b, ki, vb + 2 * gp + 1)),
        ],
        out_specs=pl.BlockSpec((None, tq, gw), lambda b, gp, qi, ki: (b, qi, gp)),
        scratch_shapes=[
            pltpu.VMEM((R, LANE), BF16),
            pltpu.VMEM((R, LANE), F32),
            pltpu.VMEM((R, LANE), F32),
        ],
        compiler_params=_cparams(("parallel", "parallel", "parallel", "arbitrary")),
        name="gqa_attn",
    )(ud3, ud3, ua3, ua3)


def _dil_body(q_ref, k_ref, kp_ref, kn_ref, v_ref, vp_ref, vn_ref, o_ref, kbuf, vbuf, m_ref, l_ref, acc_ref,
              *, TT, HALO, L):
    i = pl.program_id(2)
    H = B_SIDE
    kbuf[0:HALO, :] = kp_ref[...]
    kbuf[HALO:HALO + TT, :] = k_ref[...]
    kbuf[HALO + TT:HALO + TT + HALO, :] = kn_ref[...]
    vbuf[0:HALO, :] = vp_ref[...]
    vbuf[HALO:HALO + TT, :] = v_ref[...]
    vbuf[HALO + TT:HALO + TT + HALO, :] = vn_ref[...]

    tq = LANE
    tw = tq + 2 * H
    head0 = lax.broadcasted_iota(jnp.int32, (tq, LANE), 1) < HEAD_DIM
    row = lax.broadcasted_iota(jnp.int32, (2 * tq, tw), 0)
    col = lax.broadcasted_iota(jnp.int32, (2 * tq, tw), 1)
    band = jnp.abs(col - H - (row % tq)) <= H

    def per_head(x):
        return jnp.where(head0, jnp.broadcast_to(x[0:tq], (tq, LANE)), jnp.broadcast_to(x[tq:2 * tq], (tq, LANE)))

    for p, (_, d) in enumerate(reversed(B_PATTERNS)):
        n_sub = L // d

        def tile(idx, carry, d=d, p=p, n_sub=n_sub):
            r = idx % d
            j = idx // d
            q_start = r + d * (j * tq)
            k_start = HALO + r + d * (j * tq - H)
            if d == 1:
                q_rows, k_rows = pl.ds(q_start, tq), pl.ds(k_start, tw)
            else:
                q_rows, k_rows = pl.ds(q_start, tq, stride=d), pl.ds(k_start, tw, stride=d)
            q = q_ref[q_rows, :].astype(BF16)
            zero = jnp.zeros_like(q)
            qq = jnp.concatenate([jnp.where(head0, q, zero), jnp.where(head0, zero, q)], axis=0)
            kw = kbuf[k_rows, :].astype(BF16)
            vw = vbuf[k_rows, :].astype(BF16)
            n0 = (i * TT) // d + j * tq - H
            valid = band & (col >= -n0) & (col < n_sub - n0)
            s = jnp.where(valid, _dot_nt(qq, kw), NEG)
            m = jnp.max(s, axis=-1, keepdims=True)
            e = jnp.exp2(s - m)
            m_t = per_head(m)
            l_t = per_head(jnp.sum(e, axis=-1, keepdims=True))
            pv = _dot(e.astype(BF16), vw)
            pv_t = jnp.where(head0, pv[0:tq], pv[tq:2 * tq])
            if p == 0:
                m_ref[q_rows, :] = m_t
                l_ref[q_rows, :] = l_t
                acc_ref[q_rows, :] = pv_t
            else:
                m_old = m_ref[q_rows, :]
                m_new = jnp.maximum(m_old, m_t)
                a_old = jnp.exp2(m_old - m_new)
                a_t = jnp.exp2(m_t - m_new)
                m_ref[q_rows, :] = m_new
                l_ref[q_rows, :] = a_old * l_ref[q_rows, :] + a_t * l_t
                acc_ref[q_rows, :] = a_old * acc_ref[q_rows, :] + a_t * pv_t
            return carry

        lax.fori_loop(0, TT // tq, tile, 0, unroll=8)

    o_ref[...] = (acc_ref[...] * (1.0 / l_ref[...])).astype(BF16)


B_TILE = 2048
B_HALO = 1024


def _dilated_attention(ub3):
    B, L, _ = ub3.shape
    TT, HALO = B_TILE, B_HALO
    assert L % TT == 0 and TT % HALO == 0 and HALO >= B_SIDE * max(d for _, d in B_PATTERNS)
    pairs = B_HEADS // 2
    qb, kb, vb = UB_Q // LANE, UB_K // LANE, UB_V // LANE
    th = TT // HALO
    nh = L // HALO

    def main(base):
        return pl.BlockSpec((None, TT, LANE), lambda b, hp, i: (b, i, base + hp))

    def prev(base):
        return pl.BlockSpec((None, HALO, LANE), lambda b, hp, i: (b, jnp.maximum(i * th - 1, 0), base + hp))

    def nxt(base):
        return pl.BlockSpec((None, HALO, LANE), lambda b, hp, i: (b, jnp.minimum((i + 1) * th, nh - 1), base + hp))

    return pl.pallas_call(
        functools.partial(_dil_body, TT=TT, HALO=HALO, L=L),
        out_shape=jax.ShapeDtypeStruct((B, L, pairs * LANE), BF16),
        grid=(B, pairs, L // TT),
        in_specs=[main(qb), main(kb), prev(kb), nxt(kb), main(vb), prev(vb), nxt(vb)],
        out_specs=pl.BlockSpec((None, TT, LANE), lambda b, hp, i: (b, i, hp)),
        scratch_shapes=[pltpu.VMEM((TT + 2 * HALO, LANE), F32), pltpu.VMEM((TT + 2 * HALO, LANE), F32),
                        pltpu.VMEM((TT, LANE), F32), pltpu.VMEM((TT, LANE), F32), pltpu.VMEM((TT, LANE), F32)],
        compiler_params=_cparams(("parallel", "parallel", "parallel")),
        name="dilated_attn",
    )(ub3, ub3, ub3, ub3, ub3, ub3, ub3)


def _gla_body(*refs, R, backward):
    if backward:
        (q_ref, k_ref, v_ref, low_ref, wup_ref, bias_ref, fwd_ref, cg_ref, gn_ref, o_ref, s_ref) = refs
    else:
        (q_ref, k_ref, v_ref, low_ref, wup_ref, bias_ref, o_ref, s_ref) = refs
    C = C_CHUNK
    n = R // C

    @pl.when(pl.program_id(2) == 0)
    def _():
        s_ref[...] = jnp.zeros(s_ref.shape, F32)

    x = _dot(low_ref[...], wup_ref[...]) + bias_ref[...]
    gl = (jnp.minimum(x, 0.0) - jnp.log(1.0 + jnp.exp(-jnp.abs(x)))) * (1.0 / C_GATE_NORM)

    r64 = lax.broadcasted_iota(jnp.int32, (C, C), 0)
    c64 = lax.broadcasted_iota(jnp.int32, (C, C), 1)
    tri = (c64 >= r64) if backward else (c64 <= r64)
    tri_f = tri.astype(F32)
    lane = lax.broadcasted_iota(jnp.int32, (C, LANE), 1)
    qrow = lax.broadcasted_iota(jnp.int32, (C, LANE), 0)
    key = lane % C
    tri_wide = (key >= qrow) if backward else (key <= qrow)
    head0 = lane < HEAD_DIM
    lane2 = lax.broadcasted_iota(jnp.int32, (C, 2 * LANE), 1)
    vhead0 = lane2 < LANE
    rr = lax.broadcasted_iota(jnp.int32, (LANE, 2 * LANE), 0)
    cc = lax.broadcasted_iota(jnp.int32, (LANE, 2 * LANE), 1)
    blockdiag = (rr < HEAD_DIM) == (cc < LANE)
    eye = (lax.broadcasted_iota(jnp.int32, (LANE, LANE), 0) == lax.broadcasted_iota(jnp.int32, (LANE, LANE), 1))
    zpad = jnp.zeros((C, LANE), F32)
    zpad_v = jnp.zeros((C, 2 * LANE), BF16)

    chunk_rows = [slice(c * C, (c + 1) * C) for c in range(n)]
    cums = [jnp.dot(tri_f, gl[rows], preferred_element_type=F32, precision=lax.Precision.HIGHEST)
            for rows in chunk_rows]
    lasts = [(cum[0:1] if backward else cum[C - 1:C]) for cum in cums]
    qds, atts, intra, upds, dec_cols = [], [], [], [], []
    for rows, cum, last in zip(chunk_rows, cums, lasts):
        kc = k_ref[rows, :].astype(F32)
        qds.append((q_ref[rows, :].astype(F32) * jnp.exp(cum)).astype(BF16))
        kinv = kc * jnp.exp(-cum)
        zero = jnp.zeros_like(kinv)
        kbd = jnp.concatenate([jnp.where(head0, kinv, zero), jnp.where(head0, zero, kinv)], axis=0).astype(BF16)
        atts.append(jnp.where(tri_wide, _dot_nt(qds[-1], kbd), 0.0).astype(BF16))
        kdec = kc * jnp.exp(last - cum)
        kdec_t = jnp.concatenate([kdec, zpad], axis=0).T.astype(BF16)
        vc = v_ref[rows, :]
        upd = _dot(kdec_t, jnp.concatenate([vc, zpad_v], axis=0))
        upds.append(jnp.where(blockdiag, upd, 0.0))
        dec_row = jnp.broadcast_to(jnp.exp(last), (LANE, LANE))
        dec_cols.append(jnp.sum(jnp.where(eye, dec_row, 0.0), axis=-1, keepdims=True))
    for rows, att in zip(chunk_rows, atts):
        vc = v_ref[rows, :]
        zv = jnp.zeros_like(vc)
        vbd = jnp.concatenate([jnp.where(vhead0, vc, zv), jnp.where(vhead0, zv, vc)], axis=0)
        intra.append(_dot(att, vbd))

    state = s_ref[...]
    entering = [None] * n
    for c in (range(n - 1, -1, -1) if backward else range(n)):
        entering[c] = state.astype(BF16)
        state = dec_cols[c] * state + upds[c]
    s_ref[...] = state

    if backward:
        gate = cg_ref[...].astype(F32)
        gate = gate * (1.0 / (1.0 + jnp.exp(-gate)))
    for c, rows in enumerate(chunk_rows):
        o = intra[c] + _dot(qds[c], entering[c])
        if backward:
            tot = o + fwd_ref[rows, :]
            for h in range(2):
                hs = slice(h * LANE, (h + 1) * LANE)
                o_ref[rows, hs] = (_rms(tot[:, hs], gn_ref[...]) * gate[rows, hs]).astype(BF16)
        else:
            o_ref[rows, :] = o


def _gla(u3, wup_f, bias_f, wup_b, bias_b, gn, *, R=512):
    B, L, _ = u3.shape
    R = min(R, L)
    nb = L // R
    pairs = C_HEADS // 2
    W = C_HEADS * LANE
    qb, kb = UA_CQ // LANE, UA_CK // LANE
    vb, gb, lb = UA_CV // (2 * LANE), UA_CG // (2 * LANE), UA_CLOW // (2 * LANE)

    def specs(rev):
        def ri(i):
            return nb - 1 - i if rev else i
        return dict(
            q=pl.BlockSpec((None, R, LANE), lambda b, p, i: (b, ri(i), qb + p)),
            k=pl.BlockSpec((None, R, LANE), lambda b, p, i: (b, ri(i), kb + p)),
            v=pl.BlockSpec((None, R, 2 * LANE), lambda b, p, i: (b, ri(i), vb + p)),
            low=pl.BlockSpec((None, R, 2 * LANE), lambda b, p, i: (b, ri(i), lb)),
            wup=pl.BlockSpec((2 * LANE, LANE), lambda b, p, i: (0, p)),
            bias=pl.BlockSpec((1, LANE), lambda b, p, i: (0, p)),
            out=pl.BlockSpec((None, R, 2 * LANE), lambda b, p, i: (b, ri(i), p)),
            cg=pl.BlockSpec((None, R, 2 * LANE), lambda b, p, i: (b, ri(i), gb + p)),
            gn=pl.BlockSpec((1, LANE), lambda b, p, i: (0, 0)),
        )

    sem = _cparams(("parallel", "parallel", "arbitrary"))
    sf = specs(False)
    fwd = pl.pallas_call(
        functools.partial(_gla_body, R=R, backward=False),
        out_shape=jax.ShapeDtypeStruct((B, L, W), F32),
        grid=(B, pairs, nb),
        in_specs=[sf["q"], sf["k"], sf["v"], sf["low"], sf["wup"], sf["bias"]],
        out_specs=sf["out"],
        scratch_shapes=[pltpu.VMEM((LANE, 2 * LANE), F32)],
        compiler_params=sem,
        name="gla_fwd",
    )(u3, u3, u3, u3, wup_f, bias_f)
    sb = specs(True)
    return pl.pallas_call(
        functools.partial(_gla_body, R=R, backward=True),
        out_shape=jax.ShapeDtypeStruct((B, L, W), BF16),
        grid=(B, pairs, nb),
        in_specs=[sb["q"], sb["k"], sb["v"], sb["low"], sb["wup"], sb["bias"], sb["out"], sb["cg"], sb["gn"]],
        out_specs=sb["out"],
        scratch_shapes=[pltpu.VMEM((LANE, 2 * LANE), F32)],
        compiler_params=sem,
        name="gla_bwd",
    )(u3, u3, u3, u3, wup_b, bias_b, fwd, u3, gn)


def _outproj_body(x_ref, a_ref, b_ref, c_ref, d_ref, wa_ref, wb_ref, wc_ref, wd_ref, ng_ref, o_ref, hn_ref):
    acc = _dot(a_ref[...], wa_ref[...]) + _dot(b_ref[...], wb_ref[...])
    acc = acc + _dot(c_ref[...], wc_ref[...]) + _dot(d_ref[...], wd_ref[...])
    y = x_ref[...] + acc
    o_ref[...] = y
    hn_ref[...] = _rms(y, ng_ref[...]).astype(BF16)


def _outproj(x, oa, ob, oc, od, wa, wb, wc, wd, ng, *, tm=512):
    T, D = x.shape
    tm = min(tm, T)
    row_spec = pl.BlockSpec((tm, D), lambda i: (i, 0))

    def act(w):
        return pl.BlockSpec((tm, w), lambda i: (i, 0))

    def wt(w):
        return pl.BlockSpec((w, D), lambda i: (0, 0), pipeline_mode=pl.Buffered(1))

    widths = [oa.shape[1], ob.shape[1], oc.shape[1], od.shape[1]]
    return pl.pallas_call(
        _outproj_body,
        out_shape=[jax.ShapeDtypeStruct((T, D), F32), jax.ShapeDtypeStruct((T, D), BF16)],
        grid=(T // tm,),
        in_specs=[row_spec] + [act(w) for w in widths] + [wt(w) for w in widths]
        + [pl.BlockSpec((1, D), lambda i: (0, 0))],
        out_specs=[row_spec, row_spec],
        compiler_params=_cparams(("parallel",)),
        name="outproj",
    )(x, oa, ob, oc, od, wa, wb, wc, wd, ng)


def _pad_heads(w, n):
    k = w.shape[0]
    return jnp.pad(w.reshape(k, n, HEAD_DIM), ((0, 0), (0, 0), (0, LANE - HEAD_DIM))).reshape(k, n * LANE)


def _prep_w_in(w):
    sizes = (768, 768, 768, 768, 768, 768, 384, 384, 768, 768, 32, 768, 256, 256)
    offs = [0]
    for s in sizes:
        offs.append(offs[-1] + s)
    (a_q, a_k, a_v, b_q, b_k, b_v, c_q, c_k, c_v, c_g, c_low, d_q, d_k, d_v) = [
        w[:, offs[i]:offs[i + 1]] for i in range(len(sizes))]
    low = jnp.pad(c_low, ((0, 0), (0, 2 * LANE - 2 * C_RANK)))
    wb = jnp.concatenate([b_q, b_k, b_v], axis=1).astype(BF16)
    wa = jnp.concatenate([a_q, a_k, a_v, _pad_heads(d_v, D_KV_HEADS),
                          c_q * QK_SCALE, c_k, c_v, c_g, low], axis=1).astype(BF16)
    order = jnp.asarray(D_HEAD_ORDER)
    d_q = jnp.take(d_q.reshape(-1, D_Q_HEADS, HEAD_DIM), order, axis=1).reshape(-1, D_Q_HEADS * HEAD_DIM)
    wd = jnp.concatenate([d_q, d_k], axis=1).astype(BF16)
    assert wb.shape[1] == UB_W and wa.shape[1] == UA_W and wd.shape[1] == UD_W
    return wb, wa, wd


def _prep_w_out(w):
    wa = w[0:768]
    wb = w[768:1536]
    wc = w[1536:2304]
    wd = w[2304:3072]
    n = w.shape[1]
    wd = jnp.take(wd.reshape(D_Q_HEADS, HEAD_DIM, n), jnp.asarray(D_HEAD_ORDER), axis=0)
    wd = wd.reshape(D_Q_HEADS * HEAD_DIM, n)
    return [t.astype(BF16) for t in (wa, wb, wc, wd)]


def _q_col_scale(width, q_lo, q_hi):
    col = jnp.arange(width)
    return jnp.where((col >= q_lo) & (col < q_hi), QK_SCALE * LOG2E, 1.0).astype(F32).reshape(1, -1)


def _rope_tables(L):
    t = jnp.arange(L, dtype=F32)
    lane = jnp.arange(LANE)
    l64 = lane % HEAD_DIM
    half = ROPE_DIMS // 2
    inv = ROPE_THETA ** (-jnp.arange(0, ROPE_DIMS, 2, dtype=F32) / ROPE_DIMS)
    ang = t[:, None] * inv[None, :]
    ang_l = ang[:, l64 % half]
    in_rot = (l64 < ROPE_DIMS)[None, :]
    c8 = jnp.where(in_rot, jnp.cos(ang_l), 1.0)
    sa8 = jnp.where(((l64 >= half) & (l64 < ROPE_DIMS))[None, :], jnp.sin(ang_l), 0.0)
    sb8 = jnp.where((l64 < half)[None, :], -jnp.sin(ang_l), 0.0)

    q = HEAD_DIM // 4
    inv2 = AXIAL_THETA ** (-jnp.arange(0, HEAD_DIM // 2, 2, dtype=F32) / (HEAD_DIM // 2))
    rows = L // GRID_W
    row_pos = jnp.repeat(jnp.arange(rows, dtype=F32), GRID_W)
    col_pos = jnp.tile(jnp.arange(GRID_W, dtype=F32), rows)
    ang_r = row_pos[:, None] * inv2[None, :]
    ang_c = col_pos[:, None] * inv2[None, :]
    ang_x = jnp.where((l64 < 2 * q)[None, :], ang_r[:, l64 % q], ang_c[:, l64 % q])
    cx = jnp.cos(ang_x)
    upper = ((l64 % (2 * q)) >= q)[None, :]
    sax = jnp.where(upper, jnp.sin(ang_x), 0.0)
    sbx = jnp.where(upper, 0.0, -jnp.sin(ang_x))
    one, zero = jnp.ones_like(c8), jnp.zeros_like(c8)
    rope8 = [jnp.stack([c8, one]), jnp.stack([sa8, zero]), jnp.stack([sb8, zero])]
    axial = [jnp.stack([cx, one]), jnp.stack([sax, zero]), jnp.stack([sbx, zero])]
    return rope8, axial


def _prep_layer(l, ffn1_norm, ffn1_w_gate, ffn1_w_up, ffn1_w_down, mix_norm, w_in, w_out,
                diff_lambda_q1, diff_lambda_k1, diff_lambda_q2, diff_lambda_k2, diff_out_norm,
                gla_gate_up_f, gla_gate_bias_f, gla_gate_up_b, gla_gate_bias_b, gla_out_norm,
                gqa_q_norm, gqa_k_norm, ffn2_norm, ffn2_w_gate, ffn2_w_up, ffn2_w_down):
    def row(v):
        return v.astype(F32).reshape(1, -1)

    gq = jnp.tile(gqa_q_norm[l].astype(F32) * (QK_SCALE * LOG2E), D_Q_HEADS)
    gk = jnp.tile(gqa_k_norm[l].astype(F32), D_KV_HEADS)
    wup_f = jnp.zeros((2 * LANE, C_HEADS * HEAD_DIM), F32).at[0:C_RANK].set(gla_gate_up_f[l])
    wup_b = jnp.zeros((2 * LANE, C_HEADS * HEAD_DIM), F32).at[C_RANK:2 * C_RANK].set(gla_gate_up_b[l])
    return dict(
        n1=row(ffn1_norm[l]), wg1=ffn1_w_gate[l].astype(BF16), wu1=ffn1_w_up[l].astype(BF16),
        wd1=ffn1_w_down[l].astype(BF16),
        nmix=row(mix_norm[l]), w_in=_prep_w_in(w_in[l]), w_out=_prep_w_out(w_out[l]),
        gd=jnp.concatenate([gq, gk]).reshape(1, -1),
        lam=jnp.stack([diff_lambda_q1[l], diff_lambda_k1[l], diff_lambda_q2[l], diff_lambda_k2[l]]).astype(F32),
        lam_init=0.8 - 0.6 * math.exp(-0.3 * l),
        gdiff=row(diff_out_norm[l]),
        wup_f=wup_f.astype(BF16), bias_f=row(gla_gate_bias_f[l]),
        wup_b=wup_b.astype(BF16), bias_b=row(gla_gate_bias_b[l]),
        ggla=row(gla_out_norm[l]),
        n2=row(ffn2_norm[l]), wg2=ffn2_w_gate[l].astype(BF16), wu2=ffn2_w_up[l].astype(BF16),
        wd2=ffn2_w_down[l].astype(BF16),
    )


def _in_projections(h, p, tabs, L):
    rope8, axial = tabs
    w_b, w_a, w_d = p["w_in"]
    half = ROPE_DIMS // 2
    ub = _proj(h, w_b, _q_col_scale(UB_W, UB_Q, UB_K), rope8, L, out_dtype=F32,
               n_rot_tiles=UB_ROPE_TILES, shift=half, head_norm=False, tn=PROJ_TN, name="inproj_b")
    ua = _proj(h, w_a, _q_col_scale(UA_W, UA_AQ, UA_AK), rope8, L, out_dtype=BF16,
               n_rot_tiles=UA_ROPE_TILES, shift=half, head_norm=False, tn=PROJ_TN, name="inproj_a")
    ud = _proj(h, w_d, p["gd"], axial, L, out_dtype=BF16, n_rot_tiles=UD_W // 1024,
               shift=HEAD_DIM // 4, head_norm=True, tn=1024, name="inproj_d")
    return ub, ua, ud


def _trunk(x, layers, final_g):
    B, L, D = x.shape
    xt = x.reshape(B * L, D)
    tabs = _rope_tables(L)
    h_ffn1 = None
    for l, p in enumerate(layers):
        xt, h = _ffn(xt, p["n1"] if h_ffn1 is None else h_ffn1, p["wg1"], p["wu1"], p["wd1"], p["nmix"],
                     tail="next_norm")
        ub, ua, ud = _in_projections(h, p, tabs, L)
        ua3 = ua.reshape(B, L, UA_W)
        oa = _diff_attention(ua3, p["lam"], p["gdiff"], p["lam_init"]).reshape(B * L, -1)
        ob = _dilated_attention(ub.reshape(B, L, UB_W)).reshape(B * L, -1)
        oc = _gla(ua3, p["wup_f"], p["bias_f"], p["wup_b"], p["bias_b"], p["ggla"]).reshape(B * L, -1)
        od = _gqa_attention(ud.reshape(B, L, UD_W), ua3).reshape(B * L, -1)
        xt, h2 = _outproj(xt, oa, ob, oc, od, *p["w_out"], p["n2"])
        if l == len(layers) - 1:
            xt = _ffn(xt, h2, p["wg2"], p["wu2"], p["wd2"], final_g, tail="final_norm")
        else:
            xt, h_ffn1 = _ffn(xt, h2, p["wg2"], p["wu2"], p["wd2"], layers[l + 1]["n1"], tail="next_norm")
    return xt.reshape(B, L, D)


def kernel(x_prompt, x_sample, ffn1_norm, ffn1_w_gate, ffn1_w_up, ffn1_w_down, mix_norm, w_in, w_out, diff_lambda_q1, diff_lambda_k1, diff_lambda_q2, diff_lambda_k2, diff_out_norm, gla_gate_up_f, gla_gate_bias_f, gla_gate_up_b, gla_gate_bias_b, gla_out_norm, gqa_q_norm, gqa_k_norm, ffn2_norm, ffn2_w_gate, ffn2_w_up, ffn2_w_down, final_norm):
    depth = w_in.shape[0]
    layers = [_prep_layer(l, ffn1_norm, ffn1_w_gate, ffn1_w_up, ffn1_w_down, mix_norm, w_in, w_out,
                          diff_lambda_q1, diff_lambda_k1, diff_lambda_q2, diff_lambda_k2, diff_out_norm,
                          gla_gate_up_f, gla_gate_bias_f, gla_gate_up_b, gla_gate_bias_b, gla_out_norm,
                          gqa_q_norm, gqa_k_norm, ffn2_norm, ffn2_w_gate, ffn2_w_up, ffn2_w_down)
              for l in range(depth)]
    final_g = final_norm.astype(F32).reshape(1, -1)
    return (_trunk(x_prompt, layers, final_g), _trunk(x_sample, layers, final_g))
```

```python
import functools
import math

import jax
import jax.numpy as jnp
from jax import lax
from jax.experimental import pallas as pl
from jax.experimental.pallas import tpu as pltpu

F32 = jnp.float32
BF16 = jnp.bfloat16

HEAD_DIM = 64
EPS = 1e-6
ROPE_THETA = 500000.0
ROPE_DIMS = HEAD_DIM // 4
AXIAL_THETA = 10000.0
GRID_W = 64
A_HEADS = 6
B_HEADS = 12
B_PATTERNS = ((128, 1), (512, 4), (2048, 16))
B_SIDE = 64
C_HEADS = 6
C_RANK = 16
C_CHUNK = 64
C_GATE_NORM = 16.0
D_Q_HEADS = 12
D_KV_HEADS = 4
D_GROUP = D_Q_HEADS // D_KV_HEADS
QK_SCALE = HEAD_DIM ** -0.5
LOG2E = math.log2(math.e)

LANE = 128
NEG = -1e30

PROJ_TN = 768
UB_Q = 0
UB_K = 768
UB_V = 1536
UB_W = 2304
UB_ROPE_TILES = 2
UA_AQ = 0
UA_AK = 768
UA_AV = 1536
UA_DV = 2304
UA_CQ = 2816
UA_CK = 3200
UA_CV = 3584
UA_CG = 4352
UA_CLOW = 5120
UA_W = 5376
UA_ROPE_TILES = 2
UD_Q = 0
UD_K = 768
UD_W = 1024
D_HEAD_ORDER = tuple(D_GROUP * (2 * gp + half) + j
                     for gp in range(D_KV_HEADS // 2) for j in range(D_GROUP) for half in range(2))


def _cparams(sem, vmem_mb=56):
    return pltpu.CompilerParams(dimension_semantics=sem, vmem_limit_bytes=vmem_mb * 1024 * 1024)


def _dot(a, b):
    return jnp.dot(a, b, preferred_element_type=F32)


def _dot_nt(a, b):
    return lax.dot_general(a, b, (((1,), (1,)), ((), ())), preferred_element_type=F32)


def _rms(x, g):
    ms = jnp.mean(x * x, axis=-1, keepdims=True)
    return x * lax.rsqrt(ms + EPS) * g


def _ffn_body(x_ref, hg_ref, wgu_ref, wd_ref, ng_ref, *rest, tail, h_given):
    rest = list(rest)
    o_ref = rest.pop(0)
    hn_ref = rest.pop(0) if tail == "next_norm" else None
    h_ref = hg_ref if h_given else rest.pop(0)
    acc_ref = rest.pop(0)
    j = pl.program_id(1)
    nj = pl.num_programs(1)

    @pl.when(j == 0)
    def _():
        if not h_given:
            h_ref[...] = _rms(x_ref[...], hg_ref[...]).astype(BF16)
        acc_ref[...] = jnp.zeros(acc_ref.shape, F32)

    h = h_ref[...]
    gu = _dot(h, wgu_ref[...])
    tf = gu.shape[1] // 2
    gate, up = gu[:, 0:tf], gu[:, tf:2 * tf]
    act = (gate * (1.0 / (1.0 + jnp.exp(-gate))) * up).astype(BF16)
    acc_ref[...] += _dot(act, wd_ref[...])

    @pl.when(j == nj - 1)
    def _():
        y = x_ref[...] + 0.5 * acc_ref[...]
        if tail == "final_norm":
            y = _rms(y, ng_ref[...])
        o_ref[...] = y
        if tail == "next_norm":
            hn_ref[...] = _rms(y, ng_ref[...]).astype(BF16)


FFN_TF = 512


def _fuse_gate_up(wg, wu):
    D, FF = wg.shape
    tf = min(FFN_TF, FF)
    both = jnp.stack([wg.reshape(D, FF // tf, tf), wu.reshape(D, FF // tf, tf)], axis=2)
    return both.reshape(D, 2 * FF)


def _ffn(x, hg, wgu, wd, ng, *, tail, tm=512):
    T, D = x.shape
    FF = wd.shape[0]
    tm = min(tm, T)
    tf = min(FFN_TF, FF)
    assert T % tm == 0 and FF % tf == 0
    h_given = hg.shape[0] == T
    row_spec = pl.BlockSpec((tm, D), lambda i, j: (i, 0))
    scratch = [] if h_given else [pltpu.VMEM((tm, D), BF16)]
    if tail == "next_norm":
        out_shape = [jax.ShapeDtypeStruct((T, D), F32), jax.ShapeDtypeStruct((T, D), BF16)]
        out_specs = [row_spec, row_spec]
    else:
        out_shape = jax.ShapeDtypeStruct((T, D), F32)
        out_specs = row_spec
    return pl.pallas_call(
        functools.partial(_ffn_body, tail=tail, h_given=h_given),
        out_shape=out_shape,
        grid=(T // tm, FF // tf),
        in_specs=[
            row_spec,
            row_spec if h_given else pl.BlockSpec((1, D), lambda i, j: (0, 0)),
            pl.BlockSpec((D, 2 * tf), lambda i, j: (0, j)),
            pl.BlockSpec((tf, D), lambda i, j: (j, 0)),
            pl.BlockSpec((1, D), lambda i, j: (0, 0)),
        ],
        out_specs=out_specs,
        scratch_shapes=scratch + [pltpu.VMEM((tm, D), F32)],
        compiler_params=_cparams(("parallel", "arbitrary")),
        name="ffn",
    )(x, hg, wgu, wd, ng)


def _rot(x, c, sa, sb, shift):
    return x * c + pltpu.roll(x, shift, 1) * sa + pltpu.roll(x, LANE - shift, 1) * sb


def _proj_body(h_ref, w_ref, cs_ref, c_ref, sa_ref, sb_ref, o_ref, *, tr, shift, head_norm):
    tm, tn = o_ref.shape
    for rc in range(tm // tr):
        rows = slice(rc * tr, (rc + 1) * tr)
        acc = _dot(h_ref[rows, :], w_ref[...])
        c_t, sa_t, sb_t = c_ref[rows, :], sa_ref[rows, :], sb_ref[rows, :]
        for c in range(tn // LANE):
            sl = slice(c * LANE, (c + 1) * LANE)
            y = acc[:, sl]
            if head_norm:
                low = lax.broadcasted_iota(jnp.int32, y.shape, 1) < HEAD_DIM
                sq = y * y
                ss_low = jnp.sum(jnp.where(low, sq, 0.0), axis=-1, keepdims=True)
                ss_high = jnp.sum(jnp.where(low, 0.0, sq), axis=-1, keepdims=True)
                ms = jnp.where(low, ss_low, ss_high) * (1.0 / HEAD_DIM)
                y = y * lax.rsqrt(ms + EPS) * cs_ref[:, sl]
                y = _rot(y, c_t, sa_t, sb_t, shift)
            else:
                y = _rot(y, c_t, sa_t, sb_t, shift) * cs_ref[:, sl]
            o_ref[rows, sl] = y.astype(o_ref.dtype)


def _proj(h, w, cs, tabs, L, *, out_dtype, n_rot_tiles, shift, head_norm, tn, name, tm=1024, tr=256):
    T, D = h.shape
    W = w.shape[1]
    tm = min(tm, L)
    tr = min(tr, tm)
    assert T % tm == 0 and L % tm == 0 and W % tn == 0 and tm % tr == 0
    lt = L // tm
    tab_spec = pl.BlockSpec((None, tm, LANE), lambda i, j: (jnp.where(j < n_rot_tiles, 0, 1), i % lt, 0))
    return pl.pallas_call(
        functools.partial(_proj_body, tr=tr, shift=shift, head_norm=head_norm),
        out_shape=jax.ShapeDtypeStruct((T, W), out_dtype),
        grid=(T // tm, W // tn),
        in_specs=[
            pl.BlockSpec((tm, D), lambda i, j: (i, 0)),
            pl.BlockSpec((D, tn), lambda i, j: (0, j)),
            pl.BlockSpec((1, tn), lambda i, j: (0, j)),
            tab_spec, tab_spec, tab_spec,
        ],
        out_specs=pl.BlockSpec((tm, tn), lambda i, j: (i, j)),
        compiler_params=_cparams(("parallel", "parallel")),
        name=name,
    )(h, w, cs, *tabs)


def _flash_init(m_ref, acc_ref):
    m_ref[...] = jnp.full(m_ref.shape, NEG, F32)
    acc_ref[...] = jnp.zeros(acc_ref.shape, F32)


def _flash_step(qq_ref, k_ref, vs, m_ref, acc_ref, l_ref=None):
    s = _dot_nt(qq_ref[...], k_ref[...])
    chunks = [s[:, c * LANE:(c + 1) * LANE] for c in range(s.shape[1] // LANE)]
    mx = chunks[0]
    for sc in chunks[1:]:
        mx = jnp.maximum(mx, sc)
    m_prev = m_ref[...]
    m_new = jnp.maximum(m_prev, jnp.max(mx, axis=-1, keepdims=True))
    alpha = jnp.exp2(m_prev - m_new)
    ps = [jnp.exp2(sc - m_new) for sc in chunks]
    if l_ref is not None:
        lsum = ps[0]
        for pc in ps[1:]:
            lsum = lsum + pc
        l_ref[...] = alpha * l_ref[...] + lsum
    p = jnp.concatenate([pc.astype(BF16) for pc in ps], axis=1)
    share = p.shape[0] // len(vs)
    for t, v in enumerate(vs):
        rows = slice(t * share, (t + 1) * share)
        acc_ref[rows, :] = alpha[rows] * acc_ref[rows, :] + _dot(p[rows], v)
    m_ref[...] = m_new


def _flash_out(acc_ref, l_ref=None, ones_lane=None):
    acc = acc_ref[...]
    if l_ref is not None:
        den = jnp.sum(l_ref[...], axis=-1, keepdims=True)
    else:
        lane = lax.broadcasted_iota(jnp.int32, acc.shape, 1)
        den = jnp.sum(jnp.where(lane == ones_lane, acc, 0.0), axis=-1, keepdims=True)
    return acc * (1.0 / den)


def _diff_body(q_ref, k_ref, v_ref, lam_ref, gn_ref, o_ref, qq_ref, m_ref, l_ref, acc_ref, *, tq, lam_init):
    ki = pl.program_id(3)

    @pl.when(ki == 0)
    def _():
        q = q_ref[...]
        lane = lax.broadcasted_iota(jnp.int32, q.shape, 1)
        zero = jnp.zeros_like(q)
        qq_ref[0:tq, :] = jnp.where(lane < HEAD_DIM, q, zero)
        qq_ref[tq:2 * tq, :] = jnp.where(lane >= HEAD_DIM, q, zero)
        _flash_init(m_ref, acc_ref)
        l_ref[...] = jnp.zeros(l_ref.shape, F32)

    _flash_step(qq_ref, k_ref, [v_ref[...]], m_ref, acc_ref, l_ref)

    @pl.when(ki == pl.num_programs(3) - 1)
    def _():
        o = _flash_out(acc_ref, l_ref=l_ref)
        lv = lam_ref[...]
        lam = (jnp.exp(jnp.sum(lv[0:1] * lv[1:2], axis=-1, keepdims=True))
               - jnp.exp(jnp.sum(lv[2:3] * lv[3:4], axis=-1, keepdims=True)) + lam_init)
        a = o[0:tq] - lam * o[tq:2 * tq]
        o_ref[...] = (_rms(a, gn_ref[...]) * (1.0 - lam_init)).astype(BF16)


def _diff_attention(u3, lam_vecs, gn, lam_init, *, tq=1024, tk=2048):
    B, L, _ = u3.shape
    tq = min(tq, L)
    tk = min(tk, L)
    qb, kb, vb = UA_AQ // LANE, UA_AK // LANE, UA_AV // LANE
    return pl.pallas_call(
        functools.partial(_diff_body, tq=tq, lam_init=lam_init),
        out_shape=jax.ShapeDtypeStruct((B, L, A_HEADS * LANE), BF16),
        grid=(B, A_HEADS, L // tq, L // tk),
        in_specs=[
            pl.BlockSpec((None, tq, LANE), lambda b, h, qi, ki: (b, qi, qb + h)),
            pl.BlockSpec((None, tk, LANE), lambda b, h, qi, ki: (b, ki, kb + h)),
            pl.BlockSpec((None, tk, LANE), lambda b, h, qi, ki: (b, ki, vb + h)),
            pl.BlockSpec((4, HEAD_DIM), lambda b, h, qi, ki: (0, 0)),
            pl.BlockSpec((1, LANE), lambda b, h, qi, ki: (0, 0)),
        ],
        out_specs=pl.BlockSpec((None, tq, LANE), lambda b, h, qi, ki: (b, qi, h)),
        scratch_shapes=[
            pltpu.VMEM((2 * tq, LANE), BF16),
            pltpu.VMEM((2 * tq, LANE), F32),
            pltpu.VMEM((2 * tq, LANE), F32),
            pltpu.VMEM((2 * tq, LANE), F32),
        ],
        compiler_params=_cparams(("parallel", "parallel", "parallel", "arbitrary")),
        name="diff_attn",
    )(u3, u3, u3, lam_vecs, gn)


def _gqa_body(q_ref, k_ref, v0_ref, v1_ref, o_ref, qq_ref, m_ref, acc_ref, *, tq):
    ki = pl.program_id(3)

    @pl.when(ki == 0)
    def _():
        for half in range(2):
            for j in range(D_GROUP):
                q = q_ref[:, j * LANE:(j + 1) * LANE]
                lane = lax.broadcasted_iota(jnp.int32, q.shape, 1)
                own = (lane >= HEAD_DIM) if half == 1 else (lane < HEAD_DIM)
                r0 = (half * D_GROUP + j) * tq
                qq_ref[r0:r0 + tq, :] = jnp.where(own, q, jnp.zeros_like(q))
        _flash_init(m_ref, acc_ref)

    def with_ones(v_ref):
        v = v_ref[...]
        lane_v = lax.broadcasted_iota(jnp.int32, v.shape, 1)
        return jnp.where(lane_v == HEAD_DIM, jnp.ones_like(v), v)

    _flash_step(qq_ref, k_ref, [with_ones(v0_ref), with_ones(v1_ref)], m_ref, acc_ref)

    @pl.when(ki == pl.num_programs(3) - 1)
    def _():
        o = _flash_out(acc_ref, ones_lane=HEAD_DIM)
        low = lax.broadcasted_iota(jnp.int32, (tq, LANE), 1) < HEAD_DIM
        for j in range(D_GROUP):
            even = o[j * tq:(j + 1) * tq]
            odd = pltpu.roll(o[(D_GROUP + j) * tq:(D_GROUP + j + 1) * tq], HEAD_DIM, 1)
            o_ref[:, j * LANE:(j + 1) * LANE] = jnp.where(low, even, odd).astype(BF16)


def _gqa_attention(ud3, ua3, *, tq=512, tk=2048):
    B, L, _ = ud3.shape
    tq = min(tq, L)
    tk = min(tk, L)
    gw = D_GROUP * LANE
    R = 2 * D_GROUP * tq
    qb, kb, vb = UD_Q // gw, UD_K // LANE, UA_DV // LANE
    return pl.pallas_call(
        functools.partial(_gqa_body, tq=tq),
        out_shape=jax.ShapeDtypeStruct((B, L, D_Q_HEADS * HEAD_DIM), BF16),
        grid=(B, D_KV_HEADS // 2, L // tq, L // tk),
        in_specs=[
            pl.BlockSpec((None, tq, gw), lambda b, gp, qi, ki: (b, qi, qb + gp)),
            pl.BlockSpec((None, tk, LANE), lambda b, gp, qi, ki: (b, ki, kb + gp)),
            pl.BlockSpec((None, tk, LANE), lambda b, gp, qi, ki: (b, ki, vb + 2 * gp)),
            pl.BlockSpec((None, tk, LANE), lambda b, gp, qi, ki: (b, ki, vb + 2 * gp + 1)),
        ],
        out_specs=pl.BlockSpec((None, tq, gw), lambda b, gp, qi, ki: (b, qi, gp)),
        scratch_shapes=[
            pltpu.VMEM((R, LANE), BF16),
            pltpu.VMEM((R, LANE), F32),
            pltpu.VMEM((R, LANE), F32),
        ],
        compiler_params=_cparams(("parallel", "parallel", "parallel", "arbitrary")),
        name="gqa_attn",
    )(ud3, ud3, ua3, ua3)


def _dil_body(q_ref, k_ref, kp_ref, kn_ref, v_ref, vp_ref, vn_ref, o_ref, kbuf, vbuf, m_ref, l_ref, acc_ref,
              *, TT, HALO, L):
    i = pl.program_id(2)
    H = B_SIDE
    kbuf[0:HALO, :] = kp_ref[...]
    kbuf[HALO:HALO + TT, :] = k_ref[...]
    kbuf[HALO + TT:HALO + TT + HALO, :] = kn_ref[...]
    vbuf[0:HALO, :] = vp_ref[...]
    vbuf[HALO:HALO + TT, :] = v_ref[...]
    vbuf[HALO + TT:HALO + TT + HALO, :] = vn_ref[...]

    tq = LANE
    tw = tq + 2 * H
    head0 = lax.broadcasted_iota(jnp.int32, (tq, LANE), 1) < HEAD_DIM
    row = lax.broadcasted_iota(jnp.int32, (2 * tq, tw), 0)
    col = lax.broadcasted_iota(jnp.int32, (2 * tq, tw), 1)
    band = jnp.abs(col - H - (row % tq)) <= H

    def per_head(x):
        return jnp.where(head0, jnp.broadcast_to(x[0:tq], (tq, LANE)), jnp.broadcast_to(x[tq:2 * tq], (tq, LANE)))

    for p, (_, d) in enumerate(reversed(B_PATTERNS)):
        n_sub = L // d

        def tile(idx, carry, d=d, p=p, n_sub=n_sub):
            r = idx % d
            j = idx // d
            q_start = r + d * (j * tq)
            k_start = HALO + r + d * (j * tq - H)
            if d == 1:
                q_rows, k_rows = pl.ds(q_start, tq), pl.ds(k_start, tw)
            else:
                q_rows, k_rows = pl.ds(q_start, tq, stride=d), pl.ds(k_start, tw, stride=d)
            q = q_ref[q_rows, :].astype(BF16)
            zero = jnp.zeros_like(q)
            qq = jnp.concatenate([jnp.where(head0, q, zero), jnp.where(head0, zero, q)], axis=0)
            kw = kbuf[k_rows, :].astype(BF16)
            vw = vbuf[k_rows, :].astype(BF16)
            n0 = (i * TT) // d + j * tq - H
            valid = band & (col >= -n0) & (col < n_sub - n0)
            s = jnp.where(valid, _dot_nt(qq, kw), NEG)
            m = jnp.max(s, axis=-1, keepdims=True)
            e = jnp.exp2(s - m)
            m_t = per_head(m)
            l_t = per_head(jnp.sum(e, axis=-1, keepdims=True))
            pv = _dot(e.astype(BF16), vw)
            pv_t = jnp.where(head0, pv[0:tq], pv[tq:2 * tq])
            if p == 0:
                m_ref[q_rows, :] = m_t
                l_ref[q_rows, :] = l_t
                acc_ref[q_rows, :] = pv_t
            else:
                m_old = m_ref[q_rows, :]
                m_new = jnp.maximum(m_old, m_t)
                a_old = jnp.exp2(m_old - m_new)
                a_t = jnp.exp2(m_t - m_new)
                m_ref[q_rows, :] = m_new
                l_ref[q_rows, :] = a_old * l_ref[q_rows, :] + a_t * l_t
                acc_ref[q_rows, :] = a_old * acc_ref[q_rows, :] + a_t * pv_t
            return carry

        lax.fori_loop(0, TT // tq, tile, 0, unroll=True)

    o_ref[...] = (acc_ref[...] * (1.0 / l_ref[...])).astype(BF16)


B_TILE = 2048
B_HALO = 1024


def _dilated_attention(ub3):
    B, L, _ = ub3.shape
    TT, HALO = B_TILE, B_HALO
    assert L % TT == 0 and TT % HALO == 0 and HALO >= B_SIDE * max(d for _, d in B_PATTERNS)
    pairs = B_HEADS // 2
    qb, kb, vb = UB_Q // LANE, UB_K // LANE, UB_V // LANE
    th = TT // HALO
    nh = L // HALO

    def main(base):
        return pl.BlockSpec((None, TT, LANE), lambda b, hp, i: (b, i, base + hp))

    def prev(base):
        return pl.BlockSpec((None, HALO, LANE), lambda b, hp, i: (b, jnp.maximum(i * th - 1, 0), base + hp))

    def nxt(base):
        return pl.BlockSpec((None, HALO, LANE), lambda b, hp, i: (b, jnp.minimum((i + 1) * th, nh - 1), base + hp))

    return pl.pallas_call(
        functools.partial(_dil_body, TT=TT, HALO=HALO, L=L),
        out_shape=jax.ShapeDtypeStruct((B, L, pairs * LANE), BF16),
        grid=(B, pairs, L // TT),
        in_specs=[main(qb), main(kb), prev(kb), nxt(kb), main(vb), prev(vb), nxt(vb)],
        out_specs=pl.BlockSpec((None, TT, LANE), lambda b, hp, i: (b, i, hp)),
        scratch_shapes=[pltpu.VMEM((TT + 2 * HALO, LANE), F32), pltpu.VMEM((TT + 2 * HALO, LANE), F32),
                        pltpu.VMEM((TT, LANE), F32), pltpu.VMEM((TT, LANE), F32), pltpu.VMEM((TT, LANE), F32)],
        compiler_params=_cparams(("parallel", "parallel", "parallel")),
        name="dilated_attn",
    )(ub3, ub3, ub3, ub3, ub3, ub3, ub3)


def _gla_body(*refs, R, backward):
    if backward:
        (q_ref, k_ref, v_ref, low_ref, wup_ref, bias_ref, fwd_ref, cg_ref, gn_ref, o_ref, s_ref) = refs
    else:
        (q_ref, k_ref, v_ref, low_ref, wup_ref, bias_ref, o_ref, s_ref) = refs
    C = C_CHUNK
    n = R // C

    @pl.when(pl.program_id(2) == 0)
    def _():
        s_ref[...] = jnp.zeros(s_ref.shape, F32)

    x = _dot(low_ref[...], wup_ref[...]) + bias_ref[...]
    gl = (jnp.minimum(x, 0.0) - jnp.log(1.0 + jnp.exp(-jnp.abs(x)))) * (1.0 / C_GATE_NORM)

    lane = lax.broadcasted_iota(jnp.int32, (C, LANE), 1)
    qrow = lax.broadcasted_iota(jnp.int32, (C, LANE), 0)
    key = lane % C
    tri_wide = (key >= qrow) if backward else (key <= qrow)
    head0 = lane < HEAD_DIM
    lane2 = lax.broadcasted_iota(jnp.int32, (C, 2 * LANE), 1)
    vhead0 = lane2 < LANE
    rr = lax.broadcasted_iota(jnp.int32, (LANE, 2 * LANE), 0)
    cc = lax.broadcasted_iota(jnp.int32, (LANE, 2 * LANE), 1)
    blockdiag = (rr < HEAD_DIM) == (cc < LANE)
    eye = (lax.broadcasted_iota(jnp.int32, (LANE, LANE), 0) == lax.broadcasted_iota(jnp.int32, (LANE, LANE), 1))
    zpad = jnp.zeros((C, LANE), F32)
    zpad_v = jnp.zeros((C, 2 * LANE), BF16)

    chunk_rows = [slice(c * C, (c + 1) * C) for c in range(n)]
    pos = lax.broadcasted_iota(jnp.int32, (R, LANE), 0) % C
    cum_all = gl
    for sft in (1, 2, 4, 8, 16, 32):
        if backward:
            cum_all = cum_all + jnp.where(pos < C - sft, pltpu.roll(cum_all, R - sft, 0), 0.0)
        else:
            cum_all = cum_all + jnp.where(pos >= sft, pltpu.roll(cum_all, sft, 0), 0.0)
    cums = [cum_all[rows] for rows in chunk_rows]
    lasts = [(cum[0:1] if backward else cum[C - 1:C]) for cum in cums]
    qds, atts, intra, upds, dec_cols = [], [], [], [], []
    for rows, cum, last in zip(chunk_rows, cums, lasts):
        kc = k_ref[rows, :].astype(F32)
        qds.append((q_ref[rows, :].astype(F32) * jnp.exp(cum)).astype(BF16))
        kinv = kc * jnp.exp(-cum)
        zero = jnp.zeros_like(kinv)
        kbd = jnp.concatenate([jnp.where(head0, kinv, zero), jnp.where(head0, zero, kinv)], axis=0).astype(BF16)
        atts.append(jnp.where(tri_wide, _dot_nt(qds[-1], kbd), 0.0).astype(BF16))
        kdec = kc * jnp.exp(last - cum)
        kdec_t = jnp.concatenate([kdec, zpad], axis=0).T.astype(BF16)
        vc = v_ref[rows, :]
        upd = _dot(kdec_t, jnp.concatenate([vc, zpad_v], axis=0))
        upds.append(jnp.where(blockdiag, upd, 0.0))
        dec_row = jnp.broadcast_to(jnp.exp(last), (LANE, LANE))
        dec_cols.append(jnp.sum(jnp.where(eye, dec_row, 0.0), axis=-1, keepdims=True))
    for rows, att in zip(chunk_rows, atts):
        vc = v_ref[rows, :]
        zv = jnp.zeros_like(vc)
        vbd = jnp.concatenate([jnp.where(vhead0, vc, zv), jnp.where(vhead0, zv, vc)], axis=0)
        intra.append(_dot(att, vbd))

    state = s_ref[...]
    entering = [None] * n
    for c in (range(n - 1, -1, -1) if backward else range(n)):
        entering[c] = state.astype(BF16)
        state = dec_cols[c] * state + upds[c]
    s_ref[...] = state

    if backward:
        gate = cg_ref[...].astype(F32)
        gate = gate * (1.0 / (1.0 + jnp.exp(-gate)))
    for c, rows in enumerate(chunk_rows):
        o = intra[c] + _dot(qds[c], entering[c])
        if backward:
            tot = o + fwd_ref[rows, :]
            for h in range(2):
                hs = slice(h * LANE, (h + 1) * LANE)
                o_ref[rows, hs] = (_rms(tot[:, hs], gn_ref[...]) * gate[rows, hs]).astype(BF16)
        else:
            o_ref[rows, :] = o


def _gla(u3, wup_f, bias_f, wup_b, bias_b, gn, *, R=512):
    B, L, _ = u3.shape
    R = min(R, L)
    nb = L // R
    pairs = C_HEADS // 2
    W = C_HEADS * LANE
    qb, kb = UA_CQ // LANE, UA_CK // LANE
    vb, gb, lb = UA_CV // (2 * LANE), UA_CG // (2 * LANE), UA_CLOW // (2 * LANE)

    def specs(rev):
        def ri(i):
            return nb - 1 - i if rev else i
        return dict(
            q=pl.BlockSpec((None, R, LANE), lambda b, p, i: (b, ri(i), qb + p)),
            k=pl.BlockSpec((None, R, LANE), lambda b, p, i: (b, ri(i), kb + p)),
            v=pl.BlockSpec((None, R, 2 * LANE), lambda b, p, i: (b, ri(i), vb + p)),
            low=pl.BlockSpec((None, R, 2 * LANE), lambda b, p, i: (b, ri(i), lb)),
            wup=pl.BlockSpec((2 * LANE, LANE), lambda b, p, i: (0, p)),
            bias=pl.BlockSpec((1, LANE), lambda b, p, i: (0, p)),
            out=pl.BlockSpec((None, R, 2 * LANE), lambda b, p, i: (b, ri(i), p)),
            cg=pl.BlockSpec((None, R, 2 * LANE), lambda b, p, i: (b, ri(i), gb + p)),
            gn=pl.BlockSpec((1, LANE), lambda b, p, i: (0, 0)),
        )

    sem = _cparams(("parallel", "parallel", "arbitrary"))
    sf = specs(False)
    fwd = pl.pallas_call(
        functools.partial(_gla_body, R=R, backward=False),
        out_shape=jax.ShapeDtypeStruct((B, L, W), F32),
        grid=(B, pairs, nb),
        in_specs=[sf["q"], sf["k"], sf["v"], sf["low"], sf["wup"], sf["bias"]],
        out_specs=sf["out"],
        scratch_shapes=[pltpu.VMEM((LANE, 2 * LANE), F32)],
        compiler_params=sem,
        name="gla_fwd",
    )(u3, u3, u3, u3, wup_f, bias_f)
    sb = specs(True)
    return pl.pallas_call(
        functools.partial(_gla_body, R=R, backward=True),
        out_shape=jax.ShapeDtypeStruct((B, L, W), BF16),
        grid=(B, pairs, nb),
        in_specs=[sb["q"], sb["k"], sb["v"], sb["low"], sb["wup"], sb["bias"], sb["out"], sb["cg"], sb["gn"]],
        out_specs=sb["out"],
        scratch_shapes=[pltpu.VMEM((LANE, 2 * LANE), F32)],
        compiler_params=sem,
        name="gla_bwd",
    )(u3, u3, u3, u3, wup_b, bias_b, fwd, u3, gn)


def _outproj_body(x_ref, a_ref, b_ref, c_ref, d_ref, wa_ref, wb_ref, wc_ref, wd_ref, ng_ref, o_ref, hn_ref):
    acc = _dot(a_ref[...], wa_ref[...]) + _dot(b_ref[...], wb_ref[...])
    acc = acc + _dot(c_ref[...], wc_ref[...]) + _dot(d_ref[...], wd_ref[...])
    y = x_ref[...] + acc
    o_ref[...] = y
    hn_ref[...] = _rms(y, ng_ref[...]).astype(BF16)


def _outproj(x, oa, ob, oc, od, wa, wb, wc, wd, ng, *, tm=512):
    T, D = x.shape
    tm = min(tm, T)
    row_spec = pl.BlockSpec((tm, D), lambda i: (i, 0))

    def act(w):
        return pl.BlockSpec((tm, w), lambda i: (i, 0))

    def wt(w):
        return pl.BlockSpec((w, D), lambda i: (0, 0), pipeline_mode=pl.Buffered(1))

    widths = [oa.shape[1], ob.shape[1], oc.shape[1], od.shape[1]]
    return pl.pallas_call(
        _outproj_body,
        out_shape=[jax.ShapeDtypeStruct((T, D), F32), jax.ShapeDtypeStruct((T, D), BF16)],
        grid=(T // tm,),
        in_specs=[row_spec] + [act(w) for w in widths] + [wt(w) for w in widths]
        + [pl.BlockSpec((1, D), lambda i: (0, 0))],
        out_specs=[row_spec, row_spec],
        compiler_params=_cparams(("parallel",)),
        name="outproj",
    )(x, oa, ob, oc, od, wa, wb, wc, wd, ng)


def _pad_heads(w, n):
    k = w.shape[0]
    return jnp.pad(w.reshape(k, n, HEAD_DIM), ((0, 0), (0, 0), (0, LANE - HEAD_DIM))).reshape(k, n * LANE)


def _prep_w_in(w):
    sizes = (768, 768, 768, 768, 768, 768, 384, 384, 768, 768, 32, 768, 256, 256)
    offs = [0]
    for s in sizes:
        offs.append(offs[-1] + s)
    (a_q, a_k, a_v, b_q, b_k, b_v, c_q, c_k, c_v, c_g, c_low, d_q, d_k, d_v) = [
        w[:, offs[i]:offs[i + 1]] for i in range(len(sizes))]
    low = jnp.pad(c_low, ((0, 0), (0, 2 * LANE - 2 * C_RANK)))
    wb = jnp.concatenate([b_q, b_k, b_v], axis=1).astype(BF16)
    wa = jnp.concatenate([a_q, a_k, a_v, _pad_heads(d_v, D_KV_HEADS),
                          c_q * QK_SCALE, c_k, c_v, c_g, low], axis=1).astype(BF16)
    order = jnp.asarray(D_HEAD_ORDER)
    d_q = jnp.take(d_q.reshape(-1, D_Q_HEADS, HEAD_DIM), order, axis=1).reshape(-1, D_Q_HEADS * HEAD_DIM)
    wd = jnp.concatenate([d_q, d_k], axis=1).astype(BF16)
    assert wb.shape[1] == UB_W and wa.shape[1] == UA_W and wd.shape[1] == UD_W
    return wb, wa, wd


def _prep_w_out(w):
    wa = w[0:768]
    wb = w[768:1536]
    wc = w[1536:2304]
    wd = w[2304:3072]
    n = w.shape[1]
    wd = jnp.take(wd.reshape(D_Q_HEADS, HEAD_DIM, n), jnp.asarray(D_HEAD_ORDER), axis=0)
    wd = wd.reshape(D_Q_HEADS * HEAD_DIM, n)
    return [t.astype(BF16) for t in (wa, wb, wc, wd)]


def _q_col_scale(width, q_lo, q_hi):
    col = jnp.arange(width)
    return jnp.where((col >= q_lo) & (col < q_hi), QK_SCALE * LOG2E, 1.0).astype(F32).reshape(1, -1)


def _rope_tables(L):
    t = jnp.arange(L, dtype=F32)
    lane = jnp.arange(LANE)
    l64 = lane % HEAD_DIM
    half = ROPE_DIMS // 2
    inv = ROPE_THETA ** (-jnp.arange(0, ROPE_DIMS, 2, dtype=F32) / ROPE_DIMS)
    ang = t[:, None] * inv[None, :]
    ang_l = ang[:, l64 % half]
    in_rot = (l64 < ROPE_DIMS)[None, :]
    c8 = jnp.where(in_rot, jnp.cos(ang_l), 1.0)
    sa8 = jnp.where(((l64 >= half) & (l64 < ROPE_DIMS))[None, :], jnp.sin(ang_l), 0.0)
    sb8 = jnp.where((l64 < half)[None, :], -jnp.sin(ang_l), 0.0)

    q = HEAD_DIM // 4
    inv2 = AXIAL_THETA ** (-jnp.arange(0, HEAD_DIM // 2, 2, dtype=F32) / (HEAD_DIM // 2))
    rows = L // GRID_W
    row_pos = jnp.repeat(jnp.arange(rows, dtype=F32), GRID_W)
    col_pos = jnp.tile(jnp.arange(GRID_W, dtype=F32), rows)
    ang_r = row_pos[:, None] * inv2[None, :]
    ang_c = col_pos[:, None] * inv2[None, :]
    ang_x = jnp.where((l64 < 2 * q)[None, :], ang_r[:, l64 % q], ang_c[:, l64 % q])
    cx = jnp.cos(ang_x)
    upper = ((l64 % (2 * q)) >= q)[None, :]
    sax = jnp.where(upper, jnp.sin(ang_x), 0.0)
    sbx = jnp.where(upper, 0.0, -jnp.sin(ang_x))
    one, zero = jnp.ones_like(c8), jnp.zeros_like(c8)
    rope8 = [jnp.stack([c8, one]), jnp.stack([sa8, zero]), jnp.stack([sb8, zero])]
    axial = [jnp.stack([cx, one]), jnp.stack([sax, zero]), jnp.stack([sbx, zero])]
    return rope8, axial


def _prep_layer(l, ffn1_norm, ffn1_w_gate, ffn1_w_up, ffn1_w_down, mix_norm, w_in, w_out,
                diff_lambda_q1, diff_lambda_k1, diff_lambda_q2, diff_lambda_k2, diff_out_norm,
                gla_gate_up_f, gla_gate_bias_f, gla_gate_up_b, gla_gate_bias_b, gla_out_norm,
                gqa_q_norm, gqa_k_norm, ffn2_norm, ffn2_w_gate, ffn2_w_up, ffn2_w_down):
    def row(v):
        return v.astype(F32).reshape(1, -1)

    gq = jnp.tile(gqa_q_norm[l].astype(F32) * (QK_SCALE * LOG2E), D_Q_HEADS)
    gk = jnp.tile(gqa_k_norm[l].astype(F32), D_KV_HEADS)
    wup_f = jnp.zeros((2 * LANE, C_HEADS * HEAD_DIM), F32).at[0:C_RANK].set(gla_gate_up_f[l])
    wup_b = jnp.zeros((2 * LANE, C_HEADS * HEAD_DIM), F32).at[C_RANK:2 * C_RANK].set(gla_gate_up_b[l])
    return dict(
        n1=row(ffn1_norm[l]), wgu1=_fuse_gate_up(ffn1_w_gate[l].astype(BF16), ffn1_w_up[l].astype(BF16)),
        wd1=ffn1_w_down[l].astype(BF16),
        nmix=row(mix_norm[l]), w_in=_prep_w_in(w_in[l]), w_out=_prep_w_out(w_out[l]),
        gd=jnp.concatenate([gq, gk]).reshape(1, -1),
        lam=jnp.stack([diff_lambda_q1[l], diff_lambda_k1[l], diff_lambda_q2[l], diff_lambda_k2[l]]).astype(F32),
        lam_init=0.8 - 0.6 * math.exp(-0.3 * l),
        gdiff=row(diff_out_norm[l]),
        wup_f=wup_f.astype(BF16), bias_f=row(gla_gate_bias_f[l]),
        wup_b=wup_b.astype(BF16), bias_b=row(gla_gate_bias_b[l]),
        ggla=row(gla_out_norm[l]),
        n2=row(ffn2_norm[l]), wgu2=_fuse_gate_up(ffn2_w_gate[l].astype(BF16), ffn2_w_up[l].astype(BF16)),
        wd2=ffn2_w_down[l].astype(BF16),
    )


def _in_projections(h, p, tabs, L):
    rope8, axial = tabs
    w_b, w_a, w_d = p["w_in"]
    half = ROPE_DIMS // 2
    ub = _proj(h, w_b, _q_col_scale(UB_W, UB_Q, UB_K), rope8, L, out_dtype=F32,
               n_rot_tiles=UB_ROPE_TILES, shift=half, head_norm=False, tn=PROJ_TN, name="inproj_b")
    ua = _proj(h, w_a, _q_col_scale(UA_W, UA_AQ, UA_AK), rope8, L, out_dtype=BF16,
               n_rot_tiles=UA_ROPE_TILES, shift=half, head_norm=False, tn=PROJ_TN, name="inproj_a")
    ud = _proj(h, w_d, p["gd"], axial, L, out_dtype=BF16, n_rot_tiles=UD_W // 1024,
               shift=HEAD_DIM // 4, head_norm=True, tn=1024, name="inproj_d")
    return ub, ua, ud


def _trunk(x, layers, final_g):
    B, L, D = x.shape
    xt = x.reshape(B * L, D)
    tabs = _rope_tables(L)
    h_ffn1 = None
    for l, p in enumerate(layers):
        xt, h = _ffn(xt, p["n1"] if h_ffn1 is None else h_ffn1, p["wgu1"], p["wd1"], p["nmix"],
                     tail="next_norm")
        ub, ua, ud = _in_projections(h, p, tabs, L)
        ua3 = ua.reshape(B, L, UA_W)
        oa = _diff_attention(ua3, p["lam"], p["gdiff"], p["lam_init"]).reshape(B * L, -1)
        ob = _dilated_attention(ub.reshape(B, L, UB_W)).reshape(B * L, -1)
        oc = _gla(ua3, p["wup_f"], p["bias_f"], p["wup_b"], p["bias_b"], p["ggla"]).reshape(B * L, -1)
        od = _gqa_attention(ud.reshape(B, L, UD_W), ua3).reshape(B * L, -1)
        xt, h2 = _outproj(xt, oa, ob, oc, od, *p["w_out"], p["n2"])
        if l == len(layers) - 1:
            xt = _ffn(xt, h2, p["wgu2"], p["wd2"], final_g, tail="final_norm")
        else:
            xt, h_ffn1 = _ffn(xt, h2, p["wgu2"], p["wd2"], layers[l + 1]["n1"], tail="next_norm")
    return xt.reshape(B, L, D)


def kernel(x_prompt, x_sample, ffn1_norm, ffn1_w_gate, ffn1_w_up, ffn1_w_down, mix_norm, w_in, w_out, diff_lambda_q1, diff_lambda_k1, diff_lambda_q2, diff_lambda_k2, diff_out_norm, gla_gate_up_f, gla_gate_bias_f, gla_gate_up_b, gla_gate_bias_b, gla_out_norm, gqa_q_norm, gqa_k_norm, ffn2_norm, ffn2_w_gate, ffn2_w_up, ffn2_w_down, final_norm):
    depth = w_in.shape[0]
    layers = [_prep_layer(l, ffn1_norm, ffn1_w_gate, ffn1_w_up, ffn1_w_down, mix_norm, w_in, w_out,
                          diff_lambda_q1, diff_lambda_k1, diff_lambda_q2, diff_lambda_k2, diff_out_norm,
                          gla_gate_up_f, gla_gate_bias_f, gla_gate_up_b, gla_gate_bias_b, gla_out_norm,
                          gqa_q_norm, gqa_k_norm, ffn2_norm, ffn2_w_gate, ffn2_w_up, ffn2_w_down)
              for l in range(depth)]
    final_g = final_norm.astype(F32).reshape(1, -1)
    return (_trunk(x_prompt, layers, final_g), _trunk(x_sample, layers, final_g))
```

```python
import functools
import math

import jax
import jax.numpy as jnp
from jax import lax
from jax.experimental import pallas as pl
from jax.experimental.pallas import tpu as pltpu

F32 = jnp.float32
BF16 = jnp.bfloat16

HEAD_DIM = 64
EPS = 1e-6
ROPE_THETA = 500000.0
ROPE_DIMS = HEAD_DIM // 4
AXIAL_THETA = 10000.0
GRID_W = 64
A_HEADS = 6
B_HEADS = 12
B_PATTERNS = ((128, 1), (512, 4), (2048, 16))
B_SIDE = 64
C_HEADS = 6
C_RANK = 16
C_CHUNK = 64
C_GATE_NORM = 16.0
D_Q_HEADS = 12
D_KV_HEADS = 4
D_GROUP = D_Q_HEADS // D_KV_HEADS
QK_SCALE = HEAD_DIM ** -0.5
LOG2E = math.log2(math.e)

LANE = 128
NEG = -1e30

PROJ_TN = 768
UB_Q = 0
UB_K = 768
UB_V = 1536
UB_W = 2304
UB_ROPE_TILES = 2
UA_AQ = 0
UA_AK = 768
UA_AV = 1536
UA_DV = 2304
UA_CQ = 2816
UA_CK = 3200
UA_CV = 3584
UA_CG = 4352
UA_CLOW = 5120
UA_W = 5376
UA_ROPE_TILES = 2
UD_Q = 0
UD_K = 768
UD_W = 1024
D_HEAD_ORDER = tuple(D_GROUP * (2 * gp + half) + j
                     for gp in range(D_KV_HEADS // 2) for j in range(D_GROUP) for half in range(2))


def _cparams(sem, vmem_mb=56):
    return pltpu.CompilerParams(dimension_semantics=sem, vmem_limit_bytes=vmem_mb * 1024 * 1024)


def _dot(a, b):
    return jnp.dot(a, b, preferred_element_type=F32)


def _dot_nt(a, b):
    return lax.dot_general(a, b, (((1,), (1,)), ((), ())), preferred_element_type=F32)


def _rms(x, g):
    ms = jnp.mean(x * x, axis=-1, keepdims=True)
    return x * lax.rsqrt(ms + EPS) * g


def _ffn_body(x_ref, hg_ref, wg_ref, wu_ref, wd_ref, ng_ref, *rest, tail, h_given):
    rest = list(rest)
    o_ref = rest.pop(0)
    hn_ref = rest.pop(0) if tail == "next_norm" else None
    h_ref = hg_ref if h_given else rest.pop(0)
    acc_ref = rest.pop(0)
    j = pl.program_id(1)
    nj = pl.num_programs(1)

    @pl.when(j == 0)
    def _():
        if not h_given:
            h_ref[...] = _rms(x_ref[...], hg_ref[...]).astype(BF16)
        acc_ref[...] = jnp.zeros(acc_ref.shape, F32)

    h = h_ref[...]
    gate = _dot(h, wg_ref[...])
    up = _dot(h, wu_ref[...])
    act = (gate * (1.0 / (1.0 + jnp.exp(-gate))) * up).astype(BF16)
    acc_ref[...] += _dot(act, wd_ref[...])

    @pl.when(j == nj - 1)
    def _():
        y = x_ref[...] + 0.5 * acc_ref[...]
        if tail == "final_norm":
            y = _rms(y, ng_ref[...])
        o_ref[...] = y
        if tail == "next_norm":
            hn_ref[...] = _rms(y, ng_ref[...]).astype(BF16)


FFN_TF = 512


def _ffn(x, hg, wg, wu, wd, ng, *, tail, tm=512):
    T, D = x.shape
    FF = wd.shape[0]
    tm = min(tm, T)
    tf = min(FFN_TF, FF)
    assert T % tm == 0 and FF % tf == 0
    h_given = hg.shape[0] == T
    row_spec = pl.BlockSpec((tm, D), lambda i, j: (i, 0))
    scratch = [] if h_given else [pltpu.VMEM((tm, D), BF16)]
    if tail == "next_norm":
        out_shape = [jax.ShapeDtypeStruct((T, D), F32), jax.ShapeDtypeStruct((T, D), BF16)]
        out_specs = [row_spec, row_spec]
    else:
        out_shape = jax.ShapeDtypeStruct((T, D), F32)
        out_specs = row_spec
    return pl.pallas_call(
        functools.partial(_ffn_body, tail=tail, h_given=h_given),
        out_shape=out_shape,
        grid=(T // tm, FF // tf),
        in_specs=[
            row_spec,
            row_spec if h_given else pl.BlockSpec((1, D), lambda i, j: (0, 0)),
            pl.BlockSpec((D, tf), lambda i, j: (0, j)),
            pl.BlockSpec((D, tf), lambda i, j: (0, j)),
            pl.BlockSpec((tf, D), lambda i, j: (j, 0)),
            pl.BlockSpec((1, D), lambda i, j: (0, 0)),
        ],
        out_specs=out_specs,
        scratch_shapes=scratch + [pltpu.VMEM((tm, D), F32)],
        compiler_params=_cparams(("parallel", "arbitrary")),
        name="ffn",
    )(x, hg, wg, wu, wd, ng)


def _rot(x, c, sa, sb, shift):
    return x * c + pltpu.roll(x, shift, 1) * sa + pltpu.roll(x, LANE - shift, 1) * sb


def _proj_body(h_ref, w_ref, cs_ref, c_ref, sa_ref, sb_ref, o_ref, *, tr, shift, head_norm):
    tm, tn = o_ref.shape
    for rc in range(tm // tr):
        rows = slice(rc * tr, (rc + 1) * tr)
        acc = _dot(h_ref[rows, :], w_ref[...])
        c_t, sa_t, sb_t = c_ref[rows, :], sa_ref[rows, :], sb_ref[rows, :]
        for c in range(tn // LANE):
            sl = slice(c * LANE, (c + 1) * LANE)
            y = acc[:, sl]
            if head_norm:
                low = lax.broadcasted_iota(jnp.int32, y.shape, 1) < HEAD_DIM
                sq = y * y
                ss_low = jnp.sum(jnp.where(low, sq, 0.0), axis=-1, keepdims=True)
                ss_high = jnp.sum(jnp.where(low, 0.0, sq), axis=-1, keepdims=True)
                ms = jnp.where(low, ss_low, ss_high) * (1.0 / HEAD_DIM)
                y = y * lax.rsqrt(ms + EPS) * cs_ref[:, sl]
                y = _rot(y, c_t, sa_t, sb_t, shift)
            else:
                y = _rot(y, c_t, sa_t, sb_t, shift) * cs_ref[:, sl]
            o_ref[rows, sl] = y.astype(o_ref.dtype)


def _proj(h, w, cs, tabs, L, *, out_dtype, n_rot_tiles, shift, head_norm, tn, name, tm=1024, tr=256):
    T, D = h.shape
    W = w.shape[1]
    tm = min(tm, L)
    tr = min(tr, tm)
    assert T % tm == 0 and L % tm == 0 and W % tn == 0 and tm % tr == 0
    lt = L // tm
    tab_spec = pl.BlockSpec((None, tm, LANE), lambda i, j: (jnp.where(j < n_rot_tiles, 0, 1), i % lt, 0))
    return pl.pallas_call(
        functools.partial(_proj_body, tr=tr, shift=shift, head_norm=head_norm),
        out_shape=jax.ShapeDtypeStruct((T, W), out_dtype),
        grid=(T // tm, W // tn),
        in_specs=[
            pl.BlockSpec((tm, D), lambda i, j: (i, 0)),
            pl.BlockSpec((D, tn), lambda i, j: (0, j)),
            pl.BlockSpec((1, tn), lambda i, j: (0, j)),
            tab_spec, tab_spec, tab_spec,
        ],
        out_specs=pl.BlockSpec((tm, tn), lambda i, j: (i, j)),
        compiler_params=_cparams(("parallel", "parallel")),
        name=name,
    )(h, w, cs, *tabs)


def _flash_init(m_ref, acc_ref):
    m_ref[...] = jnp.full(m_ref.shape, NEG, F32)
    acc_ref[...] = jnp.zeros(acc_ref.shape, F32)


def _flash_step(qq_ref, k_ref, vs, m_ref, acc_ref, l_ref=None):
    s = _dot_nt(qq_ref[...], k_ref[...])
    chunks = [s[:, c * LANE:(c + 1) * LANE] for c in range(s.shape[1] // LANE)]
    mx = chunks[0]
    for sc in chunks[1:]:
        mx = jnp.maximum(mx, sc)
    m_prev = m_ref[...]
    m_new = jnp.maximum(m_prev, jnp.max(mx, axis=-1, keepdims=True))
    alpha = jnp.exp2(m_prev - m_new)
    ps = [jnp.exp2(sc - m_new) for sc in chunks]
    if l_ref is not None:
        lsum = ps[0]
        for pc in ps[1:]:
            lsum = lsum + pc
        l_ref[...] = alpha * l_ref[...] + lsum
    p = jnp.concatenate([pc.astype(BF16) for pc in ps], axis=1)
    share = p.shape[0] // len(vs)
    for t, v in enumerate(vs):
        rows = slice(t * share, (t + 1) * share)
        acc_ref[rows, :] = alpha[rows] * acc_ref[rows, :] + _dot(p[rows], v)
    m_ref[...] = m_new


def _flash_out(acc_ref, l_ref=None, ones_lane=None):
    acc = acc_ref[...]
    if l_ref is not None:
        den = jnp.sum(l_ref[...], axis=-1, keepdims=True)
    else:
        lane = lax.broadcasted_iota(jnp.int32, acc.shape, 1)
        den = jnp.sum(jnp.where(lane == ones_lane, acc, 0.0), axis=-1, keepdims=True)
    return acc * (1.0 / den)


def _diff_body(q_ref, k_ref, v_ref, lam_ref, gn_ref, o_ref, qq_ref, m_ref, l_ref, acc_ref, *, tq, lam_init):
    ki = pl.program_id(3)

    @pl.when(ki == 0)
    def _():
        q = q_ref[...]
        lane = lax.broadcasted_iota(jnp.int32, q.shape, 1)
        zero = jnp.zeros_like(q)
        qq_ref[0:tq, :] = jnp.where(lane < HEAD_DIM, q, zero)
        qq_ref[tq:2 * tq, :] = jnp.where(lane >= HEAD_DIM, q, zero)
        _flash_init(m_ref, acc_ref)
        l_ref[...] = jnp.zeros(l_ref.shape, F32)

    _flash_step(qq_ref, k_ref, [v_ref[...]], m_ref, acc_ref, l_ref)

    @pl.when(ki == pl.num_programs(3) - 1)
    def _():
        o = _flash_out(acc_ref, l_ref=l_ref)
        lv = lam_ref[...]
        lam = (jnp.exp(jnp.sum(lv[0:1] * lv[1:2], axis=-1, keepdims=True))
               - jnp.exp(jnp.sum(lv[2:3] * lv[3:4], axis=-1, keepdims=True)) + lam_init)
        a = o[0:tq] - lam * o[tq:2 * tq]
        o_ref[...] = (_rms(a, gn_ref[...]) * (1.0 - lam_init)).astype(BF16)


def _diff_attention(u3, lam_vecs, gn, lam_init, *, tq=1024, tk=2048):
    B, L, _ = u3.shape
    tq = min(tq, L)
    tk = min(tk, L)
    qb, kb, vb = UA_AQ // LANE, UA_AK // LANE, UA_AV // LANE
    return pl.pallas_call(
        functools.partial(_diff_body, tq=tq, lam_init=lam_init),
        out_shape=jax.ShapeDtypeStruct((B, L, A_HEADS * LANE), BF16),
        grid=(B, A_HEADS, L // tq, L // tk),
        in_specs=[
            pl.BlockSpec((None, tq, LANE), lambda b, h, qi, ki: (b, qi, qb + h)),
            pl.BlockSpec((None, tk, LANE), lambda b, h, qi, ki: (b, ki, kb + h)),
            pl.BlockSpec((None, tk, LANE), lambda b, h, qi, ki: (b, ki, vb + h)),
            pl.BlockSpec((4, HEAD_DIM), lambda b, h, qi, ki: (0, 0)),
            pl.BlockSpec((1, LANE), lambda b, h, qi, ki: (0, 0)),
        ],
        out_specs=pl.BlockSpec((None, tq, LANE), lambda b, h, qi, ki: (b, qi, h)),
        scratch_shapes=[
            pltpu.VMEM((2 * tq, LANE), BF16),
            pltpu.VMEM((2 * tq, LANE), F32),
            pltpu.VMEM((2 * tq, LANE), F32),
            pltpu.VMEM((2 * tq, LANE), F32),
        ],
        compiler_params=_cparams(("parallel", "parallel", "parallel", "arbitrary")),
        name="diff_attn",
    )(u3, u3, u3, lam_vecs, gn)


def _gqa_body(q_ref, k_ref, v0_ref, v1_ref, o_ref, qq_ref, m_ref, acc_ref, *, tq):
    ki = pl.program_id(3)

    @pl.when(ki == 0)
    def _():
        for half in range(2):
            for j in range(D_GROUP):
                q = q_ref[:, j * LANE:(j + 1) * LANE]
                lane = lax.broadcasted_iota(jnp.int32, q.shape, 1)
                own = (lane >= HEAD_DIM) if half == 1 else (lane < HEAD_DIM)
                r0 = (half * D_GROUP + j) * tq
                qq_ref[r0:r0 + tq, :] = jnp.where(own, q, jnp.zeros_like(q))
        _flash_init(m_ref, acc_ref)

    def with_ones(v_ref):
        v = v_ref[...]
        lane_v = lax.broadcasted_iota(jnp.int32, v.shape, 1)
        return jnp.where(lane_v == HEAD_DIM, jnp.ones_like(v), v)

    _flash_step(qq_ref, k_ref, [with_ones(v0_ref), with_ones(v1_ref)], m_ref, acc_ref)

    @pl.when(ki == pl.num_programs(3) - 1)
    def _():
        o = _flash_out(acc_ref, ones_lane=HEAD_DIM)
        low = lax.broadcasted_iota(jnp.int32, (tq, LANE), 1) < HEAD_DIM
        for j in range(D_GROUP):
            even = o[j * tq:(j + 1) * tq]
            odd = pltpu.roll(o[(D_GROUP + j) * tq:(D_GROUP + j + 1) * tq], HEAD_DIM, 1)
            o_ref[:, j * LANE:(j + 1) * LANE] = jnp.where(low, even, odd).astype(BF16)


def _gqa_attention(ud3, ua3, *, tq=512, tk=2048):
    B, L, _ = ud3.shape
    tq = min(tq, L)
    tk = min(tk, L)
    gw = D_GROUP * LANE
    R = 2 * D_GROUP * tq
    qb, kb, vb = UD_Q // gw, UD_K // LANE, UA_DV // LANE
    return pl.pallas_call(
        functools.partial(_gqa_body, tq=tq),
        out_shape=jax.ShapeDtypeStruct((B, L, D_Q_HEADS * HEAD_DIM), BF16),
        grid=(B, D_KV_HEADS // 2, L // tq, L // tk),
        in_specs=[
            pl.BlockSpec((None, tq, gw), lambda b, gp, qi, ki: (b, qi, qb + gp)),
            pl.BlockSpec((None, tk, LANE), lambda b, gp, qi, ki: (b, ki, kb + gp)),
            pl.BlockSpec((None, tk, LANE), lambda b, gp, qi, ki: (b, ki, vb + 2 * gp)),
            pl.BlockSpec((None, tk, LANE), lambda b, gp, qi, ki: (b, ki, vb + 2 * gp + 1)),
        ],
        out_specs=pl.BlockSpec((None, tq, gw), lambda b, gp, qi, ki: (b, qi, gp)),
        scratch_shapes=[
            pltpu.VMEM((R, LANE), BF16),
            pltpu.VMEM((R, LANE), F32),
            pltpu.VMEM((R, LANE), F32),
        ],
        compiler_params=_cparams(("parallel", "parallel", "parallel", "arbitrary")),
        name="gqa_attn",
    )(ud3, ud3, ua3, ua3)


def _dil_body(q_ref, k_ref, kp_ref, kn_ref, v_ref, vp_ref, vn_ref, o_ref, kbuf, vbuf, m_ref, l_ref, acc_ref,
              *, TT, HALO, L):
    i = pl.program_id(2)
    H = B_SIDE
    kbuf[0:HALO, :] = kp_ref[...]
    kbuf[HALO:HALO + TT, :] = k_ref[...]
    kbuf[HALO + TT:HALO + TT + HALO, :] = kn_ref[...]
    vbuf[0:HALO, :] = vp_ref[...]
    vbuf[HALO:HALO + TT, :] = v_ref[...]
    vbuf[HALO + TT:HALO + TT + HALO, :] = vn_ref[...]

    tq = LANE
    tw = tq + 2 * H
    head0 = lax.broadcasted_iota(jnp.int32, (tq, LANE), 1) < HEAD_DIM
    row = lax.broadcasted_iota(jnp.int32, (2 * tq, tw), 0)
    col = lax.broadcasted_iota(jnp.int32, (2 * tq, tw), 1)
    band = jnp.abs(col - H - (row % tq)) <= H

    def per_head(x):
        return jnp.where(head0, jnp.broadcast_to(x[0:tq], (tq, LANE)), jnp.broadcast_to(x[tq:2 * tq], (tq, LANE)))

    for p, (_, d) in enumerate(reversed(B_PATTERNS)):
        n_sub = L // d

        def tile(idx, carry, d=d, p=p, n_sub=n_sub):
            r = idx % d
            j = idx // d
            q_start = r + d * (j * tq)
            k_start = HALO + r + d * (j * tq - H)
            if d == 1:
                q_rows, k_rows = pl.ds(q_start, tq), pl.ds(k_start, tw)
            else:
                q_rows, k_rows = pl.ds(q_start, tq, stride=d), pl.ds(k_start, tw, stride=d)
            q = q_ref[q_rows, :].astype(BF16)
            zero = jnp.zeros_like(q)
            qq = jnp.concatenate([jnp.where(head0, q, zero), jnp.where(head0, zero, q)], axis=0)
            kw = kbuf[k_rows, :].astype(BF16)
            vw = vbuf[k_rows, :].astype(BF16)
            n0 = (i * TT) // d + j * tq - H
            valid = band & (col >= -n0) & (col < n_sub - n0)
            s = jnp.where(valid, _dot_nt(qq, kw), NEG)
            m = jnp.max(s, axis=-1, keepdims=True)
            e = jnp.exp2(s - m)
            m_t = per_head(m)
            l_t = per_head(jnp.sum(e, axis=-1, keepdims=True))
            pv = _dot(e.astype(BF16), vw)
            pv_t = jnp.where(head0, pv[0:tq], pv[tq:2 * tq])
            if p == 0:
                m_ref[q_rows, :] = m_t
                l_ref[q_rows, :] = l_t
                acc_ref[q_rows, :] = pv_t
            else:
                m_old = m_ref[q_rows, :]
                m_new = jnp.maximum(m_old, m_t)
                a_old = jnp.exp2(m_old - m_new)
                a_t = jnp.exp2(m_t - m_new)
                m_ref[q_rows, :] = m_new
                l_ref[q_rows, :] = a_old * l_ref[q_rows, :] + a_t * l_t
                acc_ref[q_rows, :] = a_old * acc_ref[q_rows, :] + a_t * pv_t
            return carry

        lax.fori_loop(0, TT // tq, tile, 0, unroll=True)

    o_ref[...] = (acc_ref[...] * (1.0 / l_ref[...])).astype(BF16)


B_TILE = 2048
B_HALO = 1024


def _dilated_attention(ub3):
    B, L, _ = ub3.shape
    TT, HALO = B_TILE, B_HALO
    assert L % TT == 0 and TT % HALO == 0 and HALO >= B_SIDE * max(d for _, d in B_PATTERNS)
    pairs = B_HEADS // 2
    qb, kb, vb = UB_Q // LANE, UB_K // LANE, UB_V // LANE
    th = TT // HALO
    nh = L // HALO

    def main(base):
        return pl.BlockSpec((None, TT, LANE), lambda b, hp, i: (b, i, base + hp))

    def prev(base):
        return pl.BlockSpec((None, HALO, LANE), lambda b, hp, i: (b, jnp.maximum(i * th - 1, 0), base + hp))

    def nxt(base):
        return pl.BlockSpec((None, HALO, LANE), lambda b, hp, i: (b, jnp.minimum((i + 1) * th, nh - 1), base + hp))

    return pl.pallas_call(
        functools.partial(_dil_body, TT=TT, HALO=HALO, L=L),
        out_shape=jax.ShapeDtypeStruct((B, L, pairs * LANE), BF16),
        grid=(B, pairs, L // TT),
        in_specs=[main(qb), main(kb), prev(kb), nxt(kb), main(vb), prev(vb), nxt(vb)],
        out_specs=pl.BlockSpec((None, TT, LANE), lambda b, hp, i: (b, i, hp)),
        scratch_shapes=[pltpu.VMEM((TT + 2 * HALO, LANE), F32), pltpu.VMEM((TT + 2 * HALO, LANE), F32),
                        pltpu.VMEM((TT, LANE), F32), pltpu.VMEM((TT, LANE), F32), pltpu.VMEM((TT, LANE), F32)],
        compiler_params=_cparams(("parallel", "parallel", "parallel")),
        name="dilated_attn",
    )(ub3, ub3, ub3, ub3, ub3, ub3, ub3)


def _gla_body(*refs, R, backward):
    if backward:
        (q_ref, k_ref, v_ref, low_ref, wup_ref, bias_ref, fwd_ref, cg_ref, gn_ref, o_ref, s_ref) = refs
    else:
        (q_ref, k_ref, v_ref, low_ref, wup_ref, bias_ref, o_ref, s_ref) = refs
    C = C_CHUNK
    n = R // C

    @pl.when(pl.program_id(2) == 0)
    def _():
        s_ref[...] = jnp.zeros(s_ref.shape, F32)

    x = _dot(low_ref[...], wup_ref[...]) + bias_ref[...]
    gl = (jnp.minimum(x, 0.0) - jnp.log(1.0 + jnp.exp(-jnp.abs(x)))) * (1.0 / C_GATE_NORM)

    lane = lax.broadcasted_iota(jnp.int32, (C, LANE), 1)
    qrow = lax.broadcasted_iota(jnp.int32, (C, LANE), 0)
    key = lane % C
    tri_wide = (key >= qrow) if backward else (key <= qrow)
    head0 = lane < HEAD_DIM
    lane2 = lax.broadcasted_iota(jnp.int32, (C, 2 * LANE), 1)
    vhead0 = lane2 < LANE
    rr = lax.broadcasted_iota(jnp.int32, (LANE, 2 * LANE), 0)
    cc = lax.broadcasted_iota(jnp.int32, (LANE, 2 * LANE), 1)
    blockdiag = (rr < HEAD_DIM) == (cc < LANE)
    eye = (lax.broadcasted_iota(jnp.int32, (LANE, LANE), 0) == lax.broadcasted_iota(jnp.int32, (LANE, LANE), 1))
    zpad = jnp.zeros((C, LANE), F32)
    zpad_v = jnp.zeros((C, 2 * LANE), BF16)

    chunk_rows = [slice(c * C, (c + 1) * C) for c in range(n)]
    pos = lax.broadcasted_iota(jnp.int32, (R, LANE), 0) % C
    cum_all = gl
    for sft in (1, 2, 4, 8, 16, 32):
        if backward:
            cum_all = cum_all + jnp.where(pos < C - sft, pltpu.roll(cum_all, R - sft, 0), 0.0)
        else:
            cum_all = cum_all + jnp.where(pos >= sft, pltpu.roll(cum_all, sft, 0), 0.0)
    cums = [cum_all[rows] for rows in chunk_rows]
    lasts = [(cum[0:1] if backward else cum[C - 1:C]) for cum in cums]
    qds, atts, intra, upds, dec_cols = [], [], [], [], []
    for rows, cum, last in zip(chunk_rows, cums, lasts):
        kc = k_ref[rows, :].astype(F32)
        qds.append((q_ref[rows, :].astype(F32) * jnp.exp(cum)).astype(BF16))
        kinv = kc * jnp.exp(-cum)
        zero = jnp.zeros_like(kinv)
        kbd = jnp.concatenate([jnp.where(head0, kinv, zero), jnp.where(head0, zero, kinv)], axis=0).astype(BF16)
        atts.append(jnp.where(tri_wide, _dot_nt(qds[-1], kbd), 0.0).astype(BF16))
        kdec = kc * jnp.exp(last - cum)
        kdec_t = jnp.concatenate([kdec, zpad], axis=0).T.astype(BF16)
        vc = v_ref[rows, :]
        upd = _dot(kdec_t, jnp.concatenate([vc, zpad_v], axis=0))
        upds.append(jnp.where(blockdiag, upd, 0.0))
        dec_row = jnp.broadcast_to(jnp.exp(last), (LANE, LANE))
        dec_cols.append(jnp.sum(jnp.where(eye, dec_row, 0.0), axis=-1, keepdims=True))
    for rows, att in zip(chunk_rows, atts):
        vc = v_ref[rows, :]
        zv = jnp.zeros_like(vc)
        vbd = jnp.concatenate([jnp.where(vhead0, vc, zv), jnp.where(vhead0, zv, vc)], axis=0)
        intra.append(_dot(att, vbd))

    state = s_ref[...]
    entering = [None] * n
    for c in (range(n - 1, -1, -1) if backward else range(n)):
        entering[c] = state.astype(BF16)
        state = dec_cols[c] * state + upds[c]
    s_ref[...] = state

    if backward:
        gate = cg_ref[...].astype(F32)
        gate = gate * (1.0 / (1.0 + jnp.exp(-gate)))
    for c, rows in enumerate(chunk_rows):
        o = intra[c] + _dot(qds[c], entering[c])
        if backward:
            tot = o + fwd_ref[rows, :]
            for h in range(2):
                hs = slice(h * LANE, (h + 1) * LANE)
                o_ref[rows, hs] = (_rms(tot[:, hs], gn_ref[...]) * gate[rows, hs]).astype(BF16)
        else:
            o_ref[rows, :] = o


def _gla(u3, wup_f, bias_f, wup_b, bias_b, gn, *, R=512):
    B, L, _ = u3.shape
    R = min(R, L)
    nb = L // R
    pairs = C_HEADS // 2
    W = C_HEADS * LANE
    qb, kb = UA_CQ // LANE, UA_CK // LANE
    vb, gb, lb = UA_CV // (2 * LANE), UA_CG // (2 * LANE), UA_CLOW // (2 * LANE)

    def specs(rev):
        def ri(i):
            return nb - 1 - i if rev else i
        return dict(
            q=pl.BlockSpec((None, R, LANE), lambda b, p, i: (b, ri(i), qb + p)),
            k=pl.BlockSpec((None, R, LANE), lambda b, p, i: (b, ri(i), kb + p)),
            v=pl.BlockSpec((None, R, 2 * LANE), lambda b, p, i: (b, ri(i), vb + p)),
            low=pl.BlockSpec((None, R, 2 * LANE), lambda b, p, i: (b, ri(i), lb)),
            wup=pl.BlockSpec((2 * LANE, LANE), lambda b, p, i: (0, p)),
            bias=pl.BlockSpec((1, LANE), lambda b, p, i: (0, p)),
            out=pl.BlockSpec((None, R, 2 * LANE), lambda b, p, i: (b, ri(i), p)),
            cg=pl.BlockSpec((None, R, 2 * LANE), lambda b, p, i: (b, ri(i), gb + p)),
            gn=pl.BlockSpec((1, LANE), lambda b, p, i: (0, 0)),
        )

    sem = _cparams(("parallel", "parallel", "arbitrary"))
    sf = specs(False)
    fwd = pl.pallas_call(
        functools.partial(_gla_body, R=R, backward=False),
        out_shape=jax.ShapeDtypeStruct((B, L, W), F32),
        grid=(B, pairs, nb),
        in_specs=[sf["q"], sf["k"], sf["v"], sf["low"], sf["wup"], sf["bias"]],
        out_specs=sf["out"],
        scratch_shapes=[pltpu.VMEM((LANE, 2 * LANE), F32)],
        compiler_params=sem,
        name="gla_fwd",
    )(u3, u3, u3, u3, wup_f, bias_f)
    sb = specs(True)
    return pl.pallas_call(
        functools.partial(_gla_body, R=R, backward=True),
        out_shape=jax.ShapeDtypeStruct((B, L, W), BF16),
        grid=(B, pairs, nb),
        in_specs=[sb["q"], sb["k"], sb["v"], sb["low"], sb["wup"], sb["bias"], sb["out"], sb["cg"], sb["gn"]],
        out_specs=sb["out"],
        scratch_shapes=[pltpu.VMEM((LANE, 2 * LANE), F32)],
        compiler_params=sem,
        name="gla_bwd",
    )(u3, u3, u3, u3, wup_b, bias_b, fwd, u3, gn)


def _outproj_body(x_ref, a_ref, b_ref, c_ref, d_ref, wa_ref, wb_ref, wc_ref, wd_ref, ng_ref, o_ref, hn_ref):
    acc = _dot(a_ref[...], wa_ref[...]) + _dot(b_ref[...], wb_ref[...])
    acc = acc + _dot(c_ref[...], wc_ref[...]) + _dot(d_ref[...], wd_ref[...])
    y = x_ref[...] + acc
    o_ref[...] = y
    hn_ref[...] = _rms(y, ng_ref[...]).astype(BF16)


def _outproj(x, oa, ob, oc, od, wa, wb, wc, wd, ng, *, tm=512):
    T, D = x.shape
    tm = min(tm, T)
    row_spec = pl.BlockSpec((tm, D), lambda i: (i, 0))

    def act(w):
        return pl.BlockSpec((tm, w), lambda i: (i, 0))

    def wt(w):
        return pl.BlockSpec((w, D), lambda i: (0, 0), pipeline_mode=pl.Buffered(1))

    widths = [oa.shape[1], ob.shape[1], oc.shape[1], od.shape[1]]
    return pl.pallas_call(
        _outproj_body,
        out_shape=[jax.ShapeDtypeStruct((T, D), F32), jax.ShapeDtypeStruct((T, D), BF16)],
        grid=(T // tm,),
        in_specs=[row_spec] + [act(w) for w in widths] + [wt(w) for w in widths]
        + [pl.BlockSpec((1, D), lambda i: (0, 0))],
        out_specs=[row_spec, row_spec],
        compiler_params=_cparams(("parallel",)),
        name="outproj",
    )(x, oa, ob, oc, od, wa, wb, wc, wd, ng)


def _pad_heads(w, n):
    k = w.shape[0]
    return jnp.pad(w.reshape(k, n, HEAD_DIM), ((0, 0), (0, 0), (0, LANE - HEAD_DIM))).reshape(k, n * LANE)


def _prep_w_in(w):
    sizes = (768, 768, 768, 768, 768, 768, 384, 384, 768, 768, 32, 768, 256, 256)
    offs = [0]
    for s in sizes:
        offs.append(offs[-1] + s)
    (a_q, a_k, a_v, b_q, b_k, b_v, c_q, c_k, c_v, c_g, c_low, d_q, d_k, d_v) = [
        w[:, offs[i]:offs[i + 1]] for i in range(len(sizes))]
    low = jnp.pad(c_low, ((0, 0), (0, 2 * LANE - 2 * C_RANK)))
    wb = jnp.concatenate([b_q, b_k, b_v], axis=1).astype(BF16)
    wa = jnp.concatenate([a_q, a_k, a_v, _pad_heads(d_v, D_KV_HEADS),
                          c_q * QK_SCALE, c_k, c_v, c_g, low], axis=1).astype(BF16)
    order = jnp.asarray(D_HEAD_ORDER)
    d_q = jnp.take(d_q.reshape(-1, D_Q_HEADS, HEAD_DIM), order, axis=1).reshape(-1, D_Q_HEADS * HEAD_DIM)
    wd = jnp.concatenate([d_q, d_k], axis=1).astype(BF16)
    assert wb.shape[1] == UB_W and wa.shape[1] == UA_W and wd.shape[1] == UD_W
    return wb, wa, wd


def _prep_w_out(w):
    wa = w[0:768]
    wb = w[768:1536]
    wc = w[1536:2304]
    wd = w[2304:3072]
    n = w.shape[1]
    wd = jnp.take(wd.reshape(D_Q_HEADS, HEAD_DIM, n), jnp.asarray(D_HEAD_ORDER), axis=0)
    wd = wd.reshape(D_Q_HEADS * HEAD_DIM, n)
    return [t.astype(BF16) for t in (wa, wb, wc, wd)]


def _q_col_scale(width, q_lo, q_hi):
    col = jnp.arange(width)
    return jnp.where((col >= q_lo) & (col < q_hi), QK_SCALE * LOG2E, 1.0).astype(F32).reshape(1, -1)


def _rope_tables(L):
    t = jnp.arange(L, dtype=F32)
    lane = jnp.arange(LANE)
    l64 = lane % HEAD_DIM
    half = ROPE_DIMS // 2
    inv = ROPE_THETA ** (-jnp.arange(0, ROPE_DIMS, 2, dtype=F32) / ROPE_DIMS)
    ang = t[:, None] * inv[None, :]
    ang_l = ang[:, l64 % half]
    in_rot = (l64 < ROPE_DIMS)[None, :]
    c8 = jnp.where(in_rot, jnp.cos(ang_l), 1.0)
    sa8 = jnp.where(((l64 >= half) & (l64 < ROPE_DIMS))[None, :], jnp.sin(ang_l), 0.0)
    sb8 = jnp.where((l64 < half)[None, :], -jnp.sin(ang_l), 0.0)

    q = HEAD_DIM // 4
    inv2 = AXIAL_THETA ** (-jnp.arange(0, HEAD_DIM // 2, 2, dtype=F32) / (HEAD_DIM // 2))
    rows = L // GRID_W
    row_pos = jnp.repeat(jnp.arange(rows, dtype=F32), GRID_W)
    col_pos = jnp.tile(jnp.arange(GRID_W, dtype=F32), rows)
    ang_r = row_pos[:, None] * inv2[None, :]
    ang_c = col_pos[:, None] * inv2[None, :]
    ang_x = jnp.where((l64 < 2 * q)[None, :], ang_r[:, l64 % q], ang_c[:, l64 % q])
    cx = jnp.cos(ang_x)
    upper = ((l64 % (2 * q)) >= q)[None, :]
    sax = jnp.where(upper, jnp.sin(ang_x), 0.0)
    sbx = jnp.where(upper, 0.0, -jnp.sin(ang_x))
    one, zero = jnp.ones_like(c8), jnp.zeros_like(c8)
    rope8 = [jnp.stack([c8, one]), jnp.stack([sa8, zero]), jnp.stack([sb8, zero])]
    axial = [jnp.stack([cx, one]), jnp.stack([sax, zero]), jnp.stack([sbx, zero])]
    return rope8, axial


def _prep_layer(l, ffn1_norm, ffn1_w_gate, ffn1_w_up, ffn1_w_down, mix_norm, w_in, w_out,
                diff_lambda_q1, diff_lambda_k1, diff_lambda_q2, diff_lambda_k2, diff_out_norm,
                gla_gate_up_f, gla_gate_bias_f, gla_gate_up_b, gla_gate_bias_b, gla_out_norm,
                gqa_q_norm, gqa_k_norm, ffn2_norm, ffn2_w_gate, ffn2_w_up, ffn2_w_down):
    def row(v):
        return v.astype(F32).reshape(1, -1)

    gq = jnp.tile(gqa_q_norm[l].astype(F32) * (QK_SCALE * LOG2E), D_Q_HEADS)
    gk = jnp.tile(gqa_k_norm[l].astype(F32), D_KV_HEADS)
    wup_f = jnp.zeros((2 * LANE, C_HEADS * HEAD_DIM), F32).at[0:C_RANK].set(gla_gate_up_f[l])
    wup_b = jnp.zeros((2 * LANE, C_HEADS * HEAD_DIM), F32).at[C_RANK:2 * C_RANK].set(gla_gate_up_b[l])
    return dict(
        n1=row(ffn1_norm[l]), wg1=ffn1_w_gate[l].astype(BF16), wu1=ffn1_w_up[l].astype(BF16),
        wd1=ffn1_w_down[l].astype(BF16),
        nmix=row(mix_norm[l]), w_in=_prep_w_in(w_in[l]), w_out=_prep_w_out(w_out[l]),
        gd=jnp.concatenate([gq, gk]).reshape(1, -1),
        lam=jnp.stack([diff_lambda_q1[l], diff_lambda_k1[l], diff_lambda_q2[l], diff_lambda_k2[l]]).astype(F32),
        lam_init=0.8 - 0.6 * math.exp(-0.3 * l),
        gdiff=row(diff_out_norm[l]),
        wup_f=wup_f.astype(BF16), bias_f=row(gla_gate_bias_f[l]),
        wup_b=wup_b.astype(BF16), bias_b=row(gla_gate_bias_b[l]),
        ggla=row(gla_out_norm[l]),
        n2=row(ffn2_norm[l]), wg2=ffn2_w_gate[l].astype(BF16), wu2=ffn2_w_up[l].astype(BF16),
        wd2=ffn2_w_down[l].astype(BF16),
    )


def _in_projections(h, p, tabs, L):
    rope8, axial = tabs
    w_b, w_a, w_d = p["w_in"]
    half = ROPE_DIMS // 2
    ub = _proj(h, w_b, _q_col_scale(UB_W, UB_Q, UB_K), rope8, L, out_dtype=F32,
               n_rot_tiles=UB_ROPE_TILES, shift=half, head_norm=False, tn=PROJ_TN, name="inproj_b")
    ua = _proj(h, w_a, _q_col_scale(UA_W, UA_AQ, UA_AK), rope8, L, out_dtype=BF16,
               n_rot_tiles=UA_ROPE_TILES, shift=half, head_norm=False, tn=PROJ_TN, name="inproj_a")
    ud = _proj(h, w_d, p["gd"], axial, L, out_dtype=BF16, n_rot_tiles=UD_W // 1024,
               shift=HEAD_DIM // 4, head_norm=True, tn=1024, name="inproj_d")
    return ub, ua, ud


def _trunk(x, layers, final_g):
    B, L, D = x.shape
    xt = x.reshape(B * L, D)
    tabs = _rope_tables(L)
    h_ffn1 = None
    for l, p in enumerate(layers):
        xt, h = _ffn(xt, p["n1"] if h_ffn1 is None else h_ffn1, p["wg1"], p["wu1"], p["wd1"], p["nmix"],
                     tail="next_norm")
        ub, ua, ud = _in_projections(h, p, tabs, L)
        ua3 = ua.reshape(B, L, UA_W)
        oa = _diff_attention(ua3, p["lam"], p["gdiff"], p["lam_init"]).reshape(B * L, -1)
        ob = _dilated_attention(ub.reshape(B, L, UB_W)).reshape(B * L, -1)
        oc = _gla(ua3, p["wup_f"], p["bias_f"], p["wup_b"], p["bias_b"], p["ggla"]).reshape(B * L, -1)
        od = _gqa_attention(ud.reshape(B, L, UD_W), ua3).reshape(B * L, -1)
        xt, h2 = _outproj(xt, oa, ob, oc, od, *p["w_out"], p["n2"])
        if l == len(layers) - 1:
            xt = _ffn(xt, h2, p["wg2"], p["wu2"], p["wd2"], final_g, tail="final_norm")
        else:
            xt, h_ffn1 = _ffn(xt, h2, p["wg2"], p["wu2"], p["wd2"], layers[l + 1]["n1"], tail="next_norm")
    return xt.reshape(B, L, D)


def kernel(x_prompt, x_sample, ffn1_norm, ffn1_w_gate, ffn1_w_up, ffn1_w_down, mix_norm, w_in, w_out, diff_lambda_q1, diff_lambda_k1, diff_lambda_q2, diff_lambda_k2, diff_out_norm, gla_gate_up_f, gla_gate_bias_f, gla_gate_up_b, gla_gate_bias_b, gla_out_norm, gqa_q_norm, gqa_k_norm, ffn2_norm, ffn2_w_gate, ffn2_w_up, ffn2_w_down, final_norm):
    depth = w_in.shape[0]
    layers = [_prep_layer(l, ffn1_norm, ffn1_w_gate, ffn1_w_up, ffn1_w_down, mix_norm, w_in, w_out,
                          diff_lambda_q1, diff_lambda_k1, diff_lambda_q2, diff_lambda_k2, diff_out_norm,
                          gla_gate_up_f, gla_gate_bias_f, gla_gate_up_b, gla_gate_bias_b, gla_out_norm,
                          gqa_q_norm, gqa_k_norm, ffn2_norm, ffn2_w_gate, ffn2_w_up, ffn2_w_down)
              for l in range(depth)]
    final_g = final_norm.astype(F32).reshape(1, -1)
    return (_trunk(x_prompt, layers, final_g), _trunk(x_sample, layers, final_g))
```

```python
import functools
import math

import jax
import jax.numpy as jnp
from jax import lax
from jax.experimental import pallas as pl
from jax.experimental.pallas import tpu as pltpu

F32 = jnp.float32
BF16 = jnp.bfloat16

HEAD_DIM = 64
EPS = 1e-6
ROPE_THETA = 500000.0
ROPE_DIMS = HEAD_DIM // 4
AXIAL_THETA = 10000.0
GRID_W = 64
A_HEADS = 6
B_HEADS = 12
B_PATTERNS = ((128, 1), (512, 4), (2048, 16))
B_SIDE = 64
C_HEADS = 6
C_RANK = 16
C_CHUNK = 64
C_GATE_NORM = 16.0
D_Q_HEADS = 12
D_KV_HEADS = 4
D_GROUP = D_Q_HEADS // D_KV_HEADS
QK_SCALE = HEAD_DIM ** -0.5
LOG2E = math.log2(math.e)

LANE = 128
NEG = -1e30

PROJ_TN = 768
UB_Q = 0
UB_K = 768
UB_V = 1536
UB_W = 2304
UB_ROPE_TILES = 2
UA_AQ = 0
UA_AK = 768
UA_AV = 1536
UA_DV = 2304
UA_CQ = 2816
UA_CK = 3200
UA_CV = 3584
UA_CG = 4352
UA_CLOW = 5120
UA_W = 5376
UA_ROPE_TILES = 2
UD_Q = 0
UD_K = 768
UD_W = 1024
D_HEAD_ORDER = tuple(D_GROUP * (2 * gp + half) + j
                     for gp in range(D_KV_HEADS // 2) for j in range(D_GROUP) for half in range(2))


V7X_VMEM_MIB = 64
VMEM_LIMIT_MIB = V7X_VMEM_MIB - 8


def _cparams(sem):
    return pltpu.CompilerParams(dimension_semantics=sem, vmem_limit_bytes=VMEM_LIMIT_MIB * 1024 * 1024)


def _dot(a, b):
    return jnp.dot(a, b, preferred_element_type=F32)


def _dot_nt(a, b):
    return lax.dot_general(a, b, (((1,), (1,)), ((), ())), preferred_element_type=F32)


def _rms(x, g):
    ms = jnp.mean(x * x, axis=-1, keepdims=True)
    return x * lax.rsqrt(ms + EPS) * g


def _ffn_body(x_ref, hg_ref, wg_ref, wu_ref, wd_ref, ng_ref, *rest, tail, h_given):
    rest = list(rest)
    o_ref = rest.pop(0)
    hn_ref = rest.pop(0) if tail == "next_norm" else None
    h_ref = hg_ref if h_given else rest.pop(0)
    acc_ref = rest.pop(0)
    j = pl.program_id(1)
    nj = pl.num_programs(1)

    @pl.when(j == 0)
    def _():
        if not h_given:
            h_ref[...] = _rms(x_ref[...], hg_ref[...]).astype(BF16)
        acc_ref[...] = jnp.zeros(acc_ref.shape, F32)

    h = h_ref[...]
    gate = _dot(h, wg_ref[...])
    up = _dot(h, wu_ref[...])
    act = (gate * (1.0 / (1.0 + jnp.exp(-gate))) * up).astype(BF16)
    acc_ref[...] += _dot(act, wd_ref[...])

    @pl.when(j == nj - 1)
    def _():
        y = x_ref[...] + 0.5 * acc_ref[...]
        if tail == "final_norm":
            y = _rms(y, ng_ref[...])
        o_ref[...] = y
        if tail == "next_norm":
            hn_ref[...] = _rms(y, ng_ref[...]).astype(BF16)


FFN_TF = 512


def _ffn(x, hg, wg, wu, wd, ng, *, tail, tm=512):
    T, D = x.shape
    FF = wd.shape[0]
    tm = min(tm, T)
    tf = min(FFN_TF, FF)
    assert T % tm == 0 and FF % tf == 0
    h_given = hg.shape[0] == T
    row_spec = pl.BlockSpec((tm, D), lambda i, j: (i, 0))
    scratch = [] if h_given else [pltpu.VMEM((tm, D), BF16)]
    if tail == "next_norm":
        out_shape = [jax.ShapeDtypeStruct((T, D), F32), jax.ShapeDtypeStruct((T, D), BF16)]
        out_specs = [row_spec, row_spec]
    else:
        out_shape = jax.ShapeDtypeStruct((T, D), F32)
        out_specs = row_spec
    return pl.pallas_call(
        functools.partial(_ffn_body, tail=tail, h_given=h_given),
        out_shape=out_shape,
        grid=(T // tm, FF // tf),
        in_specs=[
            row_spec,
            row_spec if h_given else pl.BlockSpec((1, D), lambda i, j: (0, 0)),
            pl.BlockSpec((D, tf), lambda i, j: (0, j)),
            pl.BlockSpec((D, tf), lambda i, j: (0, j)),
            pl.BlockSpec((tf, D), lambda i, j: (j, 0)),
            pl.BlockSpec((1, D), lambda i, j: (0, 0)),
        ],
        out_specs=out_specs,
        scratch_shapes=scratch + [pltpu.VMEM((tm, D), F32)],
        compiler_params=_cparams(("parallel", "arbitrary")),
        name="ffn",
    )(x, hg, wg, wu, wd, ng)


def _rot(x, c, sa, sb, shift):
    return x * c + pltpu.roll(x, shift, 1) * sa + pltpu.roll(x, LANE - shift, 1) * sb


def _proj_body(h_ref, w_ref, cs_ref, c_ref, sa_ref, sb_ref, o_ref, *, tr, shift, head_norm):
    tm, tn = o_ref.shape
    for rc in range(tm // tr):
        rows = slice(rc * tr, (rc + 1) * tr)
        acc = _dot(h_ref[rows, :], w_ref[...])
        c_t, sa_t, sb_t = c_ref[rows, :], sa_ref[rows, :], sb_ref[rows, :]
        for c in range(tn // LANE):
            sl = slice(c * LANE, (c + 1) * LANE)
            y = acc[:, sl]
            if head_norm:
                low = lax.broadcasted_iota(jnp.int32, y.shape, 1) < HEAD_DIM
                sq = y * y
                ss_low = jnp.sum(jnp.where(low, sq, 0.0), axis=-1, keepdims=True)
                ss_high = jnp.sum(jnp.where(low, 0.0, sq), axis=-1, keepdims=True)
                ms = jnp.where(low, ss_low, ss_high) * (1.0 / HEAD_DIM)
                y = y * lax.rsqrt(ms + EPS) * cs_ref[:, sl]
                y = _rot(y, c_t, sa_t, sb_t, shift)
            else:
                y = _rot(y, c_t, sa_t, sb_t, shift) * cs_ref[:, sl]
            o_ref[rows, sl] = y.astype(o_ref.dtype)


def _proj(h, w, cs, tabs, L, *, out_dtype, n_rot_tiles, shift, head_norm, tn, name, tm=1024, tr=256):
    T, D = h.shape
    W = w.shape[1]
    tm = min(tm, L)
    tr = min(tr, tm)
    assert T % tm == 0 and L % tm == 0 and W % tn == 0 and tm % tr == 0
    lt = L // tm
    tab_spec = pl.BlockSpec((None, tm, LANE), lambda i, j: (jnp.where(j < n_rot_tiles, 0, 1), i % lt, 0))
    return pl.pallas_call(
        functools.partial(_proj_body, tr=tr, shift=shift, head_norm=head_norm),
        out_shape=jax.ShapeDtypeStruct((T, W), out_dtype),
        grid=(T // tm, W // tn),
        in_specs=[
            pl.BlockSpec((tm, D), lambda i, j: (i, 0)),
            pl.BlockSpec((D, tn), lambda i, j: (0, j)),
            pl.BlockSpec((1, tn), lambda i, j: (0, j)),
            tab_spec, tab_spec, tab_spec,
        ],
        out_specs=pl.BlockSpec((tm, tn), lambda i, j: (i, j)),
        compiler_params=_cparams(("parallel", "parallel")),
        name=name,
    )(h, w, cs, *tabs)


def _flash_init(m_ref, acc_ref):
    m_ref[...] = jnp.full(m_ref.shape, NEG, F32)
    acc_ref[...] = jnp.zeros(acc_ref.shape, F32)


def _flash_step(qq_ref, k_ref, vs, m_ref, acc_ref, l_ref=None):
    s = _dot_nt(qq_ref[...], k_ref[...])
    chunks = [s[:, c * LANE:(c + 1) * LANE] for c in range(s.shape[1] // LANE)]
    mx = chunks[0]
    for sc in chunks[1:]:
        mx = jnp.maximum(mx, sc)
    m_prev = m_ref[...]
    m_new = jnp.maximum(m_prev, jnp.max(mx, axis=-1, keepdims=True))
    alpha = jnp.exp2(m_prev - m_new)
    ps = [jnp.exp2(sc - m_new) for sc in chunks]
    if l_ref is not None:
        lsum = ps[0]
        for pc in ps[1:]:
            lsum = lsum + pc
        l_ref[...] = alpha * l_ref[...] + lsum
    p = jnp.concatenate([pc.astype(BF16) for pc in ps], axis=1)
    share = p.shape[0] // len(vs)
    for t, v in enumerate(vs):
        rows = slice(t * share, (t + 1) * share)
        acc_ref[rows, :] = alpha[rows] * acc_ref[rows, :] + _dot(p[rows], v)
    m_ref[...] = m_new


def _flash_out(acc_ref, l_ref=None, ones_lane=None):
    acc = acc_ref[...]
    if l_ref is not None:
        den = jnp.sum(l_ref[...], axis=-1, keepdims=True)
    else:
        lane = lax.broadcasted_iota(jnp.int32, acc.shape, 1)
        den = jnp.sum(jnp.where(lane == ones_lane, acc, 0.0), axis=-1, keepdims=True)
    return acc * (1.0 / den)


def _diff_body(q_ref, k_ref, v_ref, lam_ref, gn_ref, o_ref, qq_ref, m_ref, l_ref, acc_ref, *, tq, lam_init):
    ki = pl.program_id(3)

    @pl.when(ki == 0)
    def _():
        q = q_ref[...]
        lane = lax.broadcasted_iota(jnp.int32, q.shape, 1)
        zero = jnp.zeros_like(q)
        qq_ref[0:tq, :] = jnp.where(lane < HEAD_DIM, q, zero)
        qq_ref[tq:2 * tq, :] = jnp.where(lane >= HEAD_DIM, q, zero)
        _flash_init(m_ref, acc_ref)
        l_ref[...] = jnp.zeros(l_ref.shape, F32)

    _flash_step(qq_ref, k_ref, [v_ref[...]], m_ref, acc_ref, l_ref)

    @pl.when(ki == pl.num_programs(3) - 1)
    def _():
        o = _flash_out(acc_ref, l_ref=l_ref)
        lv = lam_ref[...]
        lam = (jnp.exp(jnp.sum(lv[0:1] * lv[1:2], axis=-1, keepdims=True))
               - jnp.exp(jnp.sum(lv[2:3] * lv[3:4], axis=-1, keepdims=True)) + lam_init)
        a = o[0:tq] - lam * o[tq:2 * tq]
        o_ref[...] = (_rms(a, gn_ref[...]) * (1.0 - lam_init)).astype(BF16)


def _diff_attention(u3, lam_vecs, gn, lam_init, *, tq=1024, tk=2048):
    B, L, _ = u3.shape
    tq = min(tq, L)
    tk = min(tk, L)
    qb, kb, vb = UA_AQ // LANE, UA_AK // LANE, UA_AV // LANE
    return pl.pallas_call(
        functools.partial(_diff_body, tq=tq, lam_init=lam_init),
        out_shape=jax.ShapeDtypeStruct((B, L, A_HEADS * LANE), BF16),
        grid=(B, A_HEADS, L // tq, L // tk),
        in_specs=[
            pl.BlockSpec((None, tq, LANE), lambda b, h, qi, ki: (b, qi, qb + h)),
            pl.BlockSpec((None, tk, LANE), lambda b, h, qi, ki: (b, ki, kb + h)),
            pl.BlockSpec((None, tk, LANE), lambda b, h, qi, ki: (b, ki, vb + h)),
            pl.BlockSpec((4, HEAD_DIM), lambda b, h, qi, ki: (0, 0)),
            pl.BlockSpec((1, LANE), lambda b, h, qi, ki: (0, 0)),
        ],
        out_specs=pl.BlockSpec((None, tq, LANE), lambda b, h, qi, ki: (b, qi, h)),
        scratch_shapes=[
            pltpu.VMEM((2 * tq, LANE), BF16),
            pltpu.VMEM((2 * tq, LANE), F32),
            pltpu.VMEM((2 * tq, LANE), F32),
            pltpu.VMEM((2 * tq, LANE), F32),
        ],
        compiler_params=_cparams(("parallel", "parallel", "parallel", "arbitrary")),
        name="diff_attn",
    )(u3, u3, u3, lam_vecs, gn)


def _gqa_body(q_ref, k_ref, v0_ref, v1_ref, o_ref, qq_ref, m_ref, acc_ref, *, tq):
    ki = pl.program_id(3)

    @pl.when(ki == 0)
    def _():
        for half in range(2):
            for j in range(D_GROUP):
                q = q_ref[:, j * LANE:(j + 1) * LANE]
                lane = lax.broadcasted_iota(jnp.int32, q.shape, 1)
                own = (lane >= HEAD_DIM) if half == 1 else (lane < HEAD_DIM)
                r0 = (half * D_GROUP + j) * tq
                qq_ref[r0:r0 + tq, :] = jnp.where(own, q, jnp.zeros_like(q))
        _flash_init(m_ref, acc_ref)

    def with_ones(v_ref):
        v = v_ref[...]
        lane_v = lax.broadcasted_iota(jnp.int32, v.shape, 1)
        return jnp.where(lane_v == HEAD_DIM, jnp.ones_like(v), v)

    _flash_step(qq_ref, k_ref, [with_ones(v0_ref), with_ones(v1_ref)], m_ref, acc_ref)

    @pl.when(ki == pl.num_programs(3) - 1)
    def _():
        o = _flash_out(acc_ref, ones_lane=HEAD_DIM)
        low = lax.broadcasted_iota(jnp.int32, (tq, LANE), 1) < HEAD_DIM
        for j in range(D_GROUP):
            even = o[j * tq:(j + 1) * tq]
            odd = pltpu.roll(o[(D_GROUP + j) * tq:(D_GROUP + j + 1) * tq], HEAD_DIM, 1)
            o_ref[:, j * LANE:(j + 1) * LANE] = jnp.where(low, even, odd).astype(BF16)


def _gqa_attention(ud3, ua3, *, tq=512, tk=2048):
    B, L, _ = ud3.shape
    tq = min(tq, L)
    tk = min(tk, L)
    gw = D_GROUP * LANE
    R = 2 * D_GROUP * tq
    qb, kb, vb = UD_Q // gw, UD_K // LANE, UA_DV // LANE
    return pl.pallas_call(
        functools.partial(_gqa_body, tq=tq),
        out_shape=jax.ShapeDtypeStruct((B, L, D_Q_HEADS * HEAD_DIM), BF16),
        grid=(B, D_KV_HEADS // 2, L // tq, L // tk),
        in_specs=[
            pl.BlockSpec((None, tq, gw), lambda b, gp, qi, ki: (b, qi, qb + gp)),
            pl.BlockSpec((None, tk, LANE), lambda b, gp, qi, ki: (b, ki, kb + gp)),
            pl.BlockSpec((None, tk, LANE), lambda b, gp, qi, ki: (b, ki, vb + 2 * gp)),
            pl.BlockSpec((None, tk, LANE), lambda b, gp, qi, ki: (b, ki, vb + 2 * gp + 1)),
        ],
        out_specs=pl.BlockSpec((None, tq, gw), lambda b, gp, qi, ki: (b, qi, gp)),
        scratch_shapes=[
            pltpu.VMEM((R, LANE), BF16),
            pltpu.VMEM((R, LANE), F32),
            pltpu.VMEM((R, LANE), F32),
        ],
        compiler_params=_cparams(("parallel", "parallel", "parallel", "arbitrary")),
        name="gqa_attn",
    )(ud3, ud3, ua3, ua3)


def _dil_body(q_ref, k_ref, kp_ref, kn_ref, v_ref, vp_ref, vn_ref, o_ref, kbuf, vbuf, m_ref, l_ref, acc_ref,
              *, TT, HALO, L):
    i = pl.program_id(2)
    H = B_SIDE
    kbuf[0:HALO, :] = kp_ref[...]
    kbuf[HALO:HALO + TT, :] = k_ref[...]
    kbuf[HALO + TT:HALO + TT + HALO, :] = kn_ref[...]
    vbuf[0:HALO, :] = vp_ref[...]
    vbuf[HALO:HALO + TT, :] = v_ref[...]
    vbuf[HALO + TT:HALO + TT + HALO, :] = vn_ref[...]

    tq = LANE
    tw = tq + 2 * H
    head0 = lax.broadcasted_iota(jnp.int32, (tq, LANE), 1) < HEAD_DIM
    row = lax.broadcasted_iota(jnp.int32, (2 * tq, tw), 0)
    col = lax.broadcasted_iota(jnp.int32, (2 * tq, tw), 1)
    band_bias = jnp.where(jnp.abs(col - H - (row % tq)) <= H, 0.0, NEG)
    col1 = lax.broadcasted_iota(jnp.int32, (1, tw), 1)

    def per_head(x):
        return jnp.where(head0, jnp.broadcast_to(x[0:tq], (tq, LANE)), jnp.broadcast_to(x[tq:2 * tq], (tq, LANE)))

    for p, (_, d) in enumerate(reversed(B_PATTERNS)):
        n_sub = L // d

        def tile(idx, carry, d=d, p=p, n_sub=n_sub):
            r = idx % d
            j = idx // d
            q_start = r + d * (j * tq)
            k_start = HALO + r + d * (j * tq - H)
            if d == 1:
                q_rows, k_rows = pl.ds(q_start, tq), pl.ds(k_start, tw)
            else:
                q_rows, k_rows = pl.ds(q_start, tq, stride=d), pl.ds(k_start, tw, stride=d)
            q = q_ref[q_rows, :].astype(BF16)
            zero = jnp.zeros_like(q)
            qq = jnp.concatenate([jnp.where(head0, q, zero), jnp.where(head0, zero, q)], axis=0)
            kw = kbuf[k_rows, :].astype(BF16)
            vw = vbuf[k_rows, :].astype(BF16)
            n0 = (i * TT) // d + j * tq - H
            in_seq = jnp.where((col1 >= -n0) & (col1 < n_sub - n0), 0.0, NEG)
            s = _dot_nt(qq, kw) + band_bias + in_seq
            m = jnp.max(s, axis=-1, keepdims=True)
            e = jnp.exp2(s - m)
            m_t = per_head(m)
            l_t = per_head(jnp.sum(e, axis=-1, keepdims=True))
            pv = _dot(e.astype(BF16), vw)
            pv_t = jnp.where(head0, pv[0:tq], pv[tq:2 * tq])
            if p == 0:
                m_ref[q_rows, :] = m_t
                l_ref[q_rows, :] = l_t
                acc_ref[q_rows, :] = pv_t
            else:
                m_old = m_ref[q_rows, :]
                m_new = jnp.maximum(m_old, m_t)
                a_old = jnp.exp2(m_old - m_new)
                a_t = jnp.exp2(m_t - m_new)
                m_ref[q_rows, :] = m_new
                l_ref[q_rows, :] = a_old * l_ref[q_rows, :] + a_t * l_t
                acc_ref[q_rows, :] = a_old * acc_ref[q_rows, :] + a_t * pv_t
            return carry

        lax.fori_loop(0, TT // tq, tile, 0, unroll=True)

    o_ref[...] = (acc_ref[...] * (1.0 / l_ref[...])).astype(BF16)


B_TILE = 2048
B_HALO = 1024


def _dilated_attention(ub3):
    B, L, _ = ub3.shape
    TT, HALO = B_TILE, B_HALO
    assert L % TT == 0 and TT % HALO == 0 and HALO >= B_SIDE * max(d for _, d in B_PATTERNS)
    pairs = B_HEADS // 2
    qb, kb, vb = UB_Q // LANE, UB_K // LANE, UB_V // LANE
    th = TT // HALO
    nh = L // HALO

    def main(base):
        return pl.BlockSpec((None, TT, LANE), lambda b, hp, i: (b, i, base + hp))

    def prev(base):
        return pl.BlockSpec((None, HALO, LANE), lambda b, hp, i: (b, jnp.maximum(i * th - 1, 0), base + hp))

    def nxt(base):
        return pl.BlockSpec((None, HALO, LANE), lambda b, hp, i: (b, jnp.minimum((i + 1) * th, nh - 1), base + hp))

    return pl.pallas_call(
        functools.partial(_dil_body, TT=TT, HALO=HALO, L=L),
        out_shape=jax.ShapeDtypeStruct((B, L, pairs * LANE), BF16),
        grid=(B, pairs, L // TT),
        in_specs=[main(qb), main(kb), prev(kb), nxt(kb), main(vb), prev(vb), nxt(vb)],
        out_specs=pl.BlockSpec((None, TT, LANE), lambda b, hp, i: (b, i, hp)),
        scratch_shapes=[pltpu.VMEM((TT + 2 * HALO, LANE), F32), pltpu.VMEM((TT + 2 * HALO, LANE), F32),
                        pltpu.VMEM((TT, LANE), F32), pltpu.VMEM((TT, LANE), F32), pltpu.VMEM((TT, LANE), F32)],
        compiler_params=_cparams(("parallel", "parallel", "parallel")),
        name="dilated_attn",
    )(ub3, ub3, ub3, ub3, ub3, ub3, ub3)


def _gla_body(*refs, R, backward):
    if backward:
        (q_ref, k_ref, v_ref, low_ref, wup_ref, bias_ref, fwd_ref, cg_ref, gn_ref, o_ref, s_ref) = refs
    else:
        (q_ref, k_ref, v_ref, low_ref, wup_ref, bias_ref, o_ref, s_ref) = refs
    C = C_CHUNK
    n = R // C

    @pl.when(pl.program_id(2) == 0)
    def _():
        s_ref[...] = jnp.zeros(s_ref.shape, F32)

    x = _dot(low_ref[...], wup_ref[...]) + bias_ref[...]
    gl = (jnp.minimum(x, 0.0) - jnp.log(1.0 + jnp.exp(-jnp.abs(x)))) * (1.0 / C_GATE_NORM)

    lane = lax.broadcasted_iota(jnp.int32, (C, LANE), 1)
    qrow = lax.broadcasted_iota(jnp.int32, (C, LANE), 0)
    key = lane % C
    tri_wide = (key >= qrow) if backward else (key <= qrow)
    head0 = lane < HEAD_DIM
    lane2 = lax.broadcasted_iota(jnp.int32, (C, 2 * LANE), 1)
    vhead0 = lane2 < LANE
    rr = lax.broadcasted_iota(jnp.int32, (LANE, 2 * LANE), 0)
    cc = lax.broadcasted_iota(jnp.int32, (LANE, 2 * LANE), 1)
    blockdiag = (rr < HEAD_DIM) == (cc < LANE)
    eye = (lax.broadcasted_iota(jnp.int32, (LANE, LANE), 0) == lax.broadcasted_iota(jnp.int32, (LANE, LANE), 1))
    zpad = jnp.zeros((C, LANE), F32)
    zpad_v = jnp.zeros((C, 2 * LANE), BF16)

    chunk_rows = [slice(c * C, (c + 1) * C) for c in range(n)]
    pos = lax.broadcasted_iota(jnp.int32, (R, LANE), 0) % C
    cum_all = gl
    for sft in (1, 2, 4, 8, 16, 32):
        if backward:
            cum_all = cum_all + jnp.where(pos < C - sft, pltpu.roll(cum_all, R - sft, 0), 0.0)
        else:
            cum_all = cum_all + jnp.where(pos >= sft, pltpu.roll(cum_all, sft, 0), 0.0)
    cums = [cum_all[rows] for rows in chunk_rows]
    lasts = [(cum[0:1] if backward else cum[C - 1:C]) for cum in cums]
    qds, atts, intra, upds, dec_cols = [], [], [], [], []
    for rows, cum, last in zip(chunk_rows, cums, lasts):
        kc = k_ref[rows, :].astype(F32)
        qds.append((q_ref[rows, :].astype(F32) * jnp.exp(cum)).astype(BF16))
        kinv = kc * jnp.exp(-cum)
        zero = jnp.zeros_like(kinv)
        kbd = jnp.concatenate([jnp.where(head0, kinv, zero), jnp.where(head0, zero, kinv)], axis=0).astype(BF16)
        atts.append(jnp.where(tri_wide, _dot_nt(qds[-1], kbd), 0.0).astype(BF16))
        kdec = kc * jnp.exp(last - cum)
        kdec_t = jnp.concatenate([kdec, zpad], axis=0).T.astype(BF16)
        vc = v_ref[rows, :]
        upd = _dot(kdec_t, jnp.concatenate([vc, zpad_v], axis=0))
        upds.append(jnp.where(blockdiag, upd, 0.0))
        dec_row = jnp.broadcast_to(jnp.exp(last), (LANE, LANE))
        dec_cols.append(jnp.sum(jnp.where(eye, dec_row, 0.0), axis=-1, keepdims=True))
    for rows, att in zip(chunk_rows, atts):
        vc = v_ref[rows, :]
        zv = jnp.zeros_like(vc)
        vbd = jnp.concatenate([jnp.where(vhead0, vc, zv), jnp.where(vhead0, zv, vc)], axis=0)
        intra.append(_dot(att, vbd))

    state = s_ref[...]
    entering = [None] * n
    for c in (range(n - 1, -1, -1) if backward else range(n)):
        entering[c] = state.astype(BF16)
        state = dec_cols[c] * state + upds[c]
    s_ref[...] = state

    if backward:
        gate = cg_ref[...].astype(F32)
        gate = gate * (1.0 / (1.0 + jnp.exp(-gate)))
    for c, rows in enumerate(chunk_rows):
        o = intra[c] + _dot(qds[c], entering[c])
        if backward:
            tot = o + fwd_ref[rows, :]
            for h in range(2):
                hs = slice(h * LANE, (h + 1) * LANE)
                o_ref[rows, hs] = (_rms(tot[:, hs], gn_ref[...]) * gate[rows, hs]).astype(BF16)
        else:
            o_ref[rows, :] = o


def _gla(u3, wup_f, bias_f, wup_b, bias_b, gn, *, R=512):
    B, L, _ = u3.shape
    R = min(R, L)
    nb = L // R
    pairs = C_HEADS // 2
    W = C_HEADS * LANE
    qb, kb = UA_CQ // LANE, UA_CK // LANE
    vb, gb, lb = UA_CV // (2 * LANE), UA_CG // (2 * LANE), UA_CLOW // (2 * LANE)

    def specs(rev):
        def ri(i):
            return nb - 1 - i if rev else i
        return dict(
            q=pl.BlockSpec((None, R, LANE), lambda b, p, i: (b, ri(i), qb + p)),
            k=pl.BlockSpec((None, R, LANE), lambda b, p, i: (b, ri(i), kb + p)),
            v=pl.BlockSpec((None, R, 2 * LANE), lambda b, p, i: (b, ri(i), vb + p)),
            low=pl.BlockSpec((None, R, 2 * LANE), lambda b, p, i: (b, ri(i), lb)),
            wup=pl.BlockSpec((2 * LANE, LANE), lambda b, p, i: (0, p)),
            bias=pl.BlockSpec((1, LANE), lambda b, p, i: (0, p)),
            out=pl.BlockSpec((None, R, 2 * LANE), lambda b, p, i: (b, ri(i), p)),
            cg=pl.BlockSpec((None, R, 2 * LANE), lambda b, p, i: (b, ri(i), gb + p)),
            gn=pl.BlockSpec((1, LANE), lambda b, p, i: (0, 0)),
        )

    sem = _cparams(("parallel", "parallel", "arbitrary"))
    sf = specs(False)
    fwd = pl.pallas_call(
        functools.partial(_gla_body, R=R, backward=False),
        out_shape=jax.ShapeDtypeStruct((B, L, W), F32),
        grid=(B, pairs, nb),
        in_specs=[sf["q"], sf["k"], sf["v"], sf["low"], sf["wup"], sf["bias"]],
        out_specs=sf["out"],
        scratch_shapes=[pltpu.VMEM((LANE, 2 * LANE), F32)],
        compiler_params=sem,
        name="gla_fwd",
    )(u3, u3, u3, u3, wup_f, bias_f)
    sb = specs(True)
    return pl.pallas_call(
        functools.partial(_gla_body, R=R, backward=True),
        out_shape=jax.ShapeDtypeStruct((B, L, W), BF16),
        grid=(B, pairs, nb),
        in_specs=[sb["q"], sb["k"], sb["v"], sb["low"], sb["wup"], sb["bias"], sb["out"], sb["cg"], sb["gn"]],
        out_specs=sb["out"],
        scratch_shapes=[pltpu.VMEM((LANE, 2 * LANE), F32)],
        compiler_params=sem,
        name="gla_bwd",
    )(u3, u3, u3, u3, wup_b, bias_b, fwd, u3, gn)


def _outproj_body(x_ref, a_ref, b_ref, c_ref, d_ref, wa_ref, wb_ref, wc_ref, wd_ref, ng_ref, o_ref, hn_ref):
    acc = _dot(a_ref[...], wa_ref[...]) + _dot(b_ref[...], wb_ref[...])
    acc = acc + _dot(c_ref[...], wc_ref[...]) + _dot(d_ref[...], wd_ref[...])
    y = x_ref[...] + acc
    o_ref[...] = y
    hn_ref[...] = _rms(y, ng_ref[...]).astype(BF16)


def _outproj(x, oa, ob, oc, od, wa, wb, wc, wd, ng, *, tm=512):
    T, D = x.shape
    tm = min(tm, T)
    row_spec = pl.BlockSpec((tm, D), lambda i: (i, 0))

    def act(w):
        return pl.BlockSpec((tm, w), lambda i: (i, 0))

    def wt(w):
        return pl.BlockSpec((w, D), lambda i: (0, 0), pipeline_mode=pl.Buffered(1))

    widths = [oa.shape[1], ob.shape[1], oc.shape[1], od.shape[1]]
    return pl.pallas_call(
        _outproj_body,
        out_shape=[jax.ShapeDtypeStruct((T, D), F32), jax.ShapeDtypeStruct((T, D), BF16)],
        grid=(T // tm,),
        in_specs=[row_spec] + [act(w) for w in widths] + [wt(w) for w in widths]
        + [pl.BlockSpec((1, D), lambda i: (0, 0))],
        out_specs=[row_spec, row_spec],
        compiler_params=_cparams(("parallel",)),
        name="outproj",
    )(x, oa, ob, oc, od, wa, wb, wc, wd, ng)


def _pad_heads(w, n):
    k = w.shape[0]
    return jnp.pad(w.reshape(k, n, HEAD_DIM), ((0, 0), (0, 0), (0, LANE - HEAD_DIM))).reshape(k, n * LANE)


def _prep_w_in(w):
    sizes = (768, 768, 768, 768, 768, 768, 384, 384, 768, 768, 32, 768, 256, 256)
    offs = [0]
    for s in sizes:
        offs.append(offs[-1] + s)
    (a_q, a_k, a_v, b_q, b_k, b_v, c_q, c_k, c_v, c_g, c_low, d_q, d_k, d_v) = [
        w[:, offs[i]:offs[i + 1]] for i in range(len(sizes))]
    low = jnp.pad(c_low, ((0, 0), (0, 2 * LANE - 2 * C_RANK)))
    wb = jnp.concatenate([b_q, b_k, b_v], axis=1).astype(BF16)
    wa = jnp.concatenate([a_q, a_k, a_v, _pad_heads(d_v, D_KV_HEADS),
                          c_q * QK_SCALE, c_k, c_v, c_g, low], axis=1).astype(BF16)
    order = jnp.asarray(D_HEAD_ORDER)
    d_q = jnp.take(d_q.reshape(-1, D_Q_HEADS, HEAD_DIM), order, axis=1).reshape(-1, D_Q_HEADS * HEAD_DIM)
    wd = jnp.concatenate([d_q, d_k], axis=1).astype(BF16)
    assert wb.shape[1] == UB_W and wa.shape[1] == UA_W and wd.shape[1] == UD_W
    return wb, wa, wd


def _prep_w_out(w):
    wa = w[0:768]
    wb = w[768:1536]
    wc = w[1536:2304]
    wd = w[2304:3072]
    n = w.shape[1]
    wd = jnp.take(wd.reshape(D_Q_HEADS, HEAD_DIM, n), jnp.asarray(D_HEAD_ORDER), axis=0)
    wd = wd.reshape(D_Q_HEADS * HEAD_DIM, n)
    return [t.astype(BF16) for t in (wa, wb, wc, wd)]


def _q_col_scale(width, q_lo, q_hi):
    col = jnp.arange(width)
    return jnp.where((col >= q_lo) & (col < q_hi), QK_SCALE * LOG2E, 1.0).astype(F32).reshape(1, -1)


def _rope_tables(L):
    t = jnp.arange(L, dtype=F32)
    lane = jnp.arange(LANE)
    l64 = lane % HEAD_DIM
    half = ROPE_DIMS // 2
    inv = ROPE_THETA ** (-jnp.arange(0, ROPE_DIMS, 2, dtype=F32) / ROPE_DIMS)
    ang = t[:, None] * inv[None, :]
    ang_l = ang[:, l64 % half]
    in_rot = (l64 < ROPE_DIMS)[None, :]
    c8 = jnp.where(in_rot, jnp.cos(ang_l), 1.0)
    sa8 = jnp.where(((l64 >= half) & (l64 < ROPE_DIMS))[None, :], jnp.sin(ang_l), 0.0)
    sb8 = jnp.where((l64 < half)[None, :], -jnp.sin(ang_l), 0.0)

    q = HEAD_DIM // 4
    inv2 = AXIAL_THETA ** (-jnp.arange(0, HEAD_DIM // 2, 2, dtype=F32) / (HEAD_DIM // 2))
    rows = L // GRID_W
    row_pos = jnp.repeat(jnp.arange(rows, dtype=F32), GRID_W)
    col_pos = jnp.tile(jnp.arange(GRID_W, dtype=F32), rows)
    ang_r = row_pos[:, None] * inv2[None, :]
    ang_c = col_pos[:, None] * inv2[None, :]
    ang_x = jnp.where((l64 < 2 * q)[None, :], ang_r[:, l64 % q], ang_c[:, l64 % q])
    cx = jnp.cos(ang_x)
    upper = ((l64 % (2 * q)) >= q)[None, :]
    sax = jnp.where(upper, jnp.sin(ang_x), 0.0)
    sbx = jnp.where(upper, 0.0, -jnp.sin(ang_x))
    one, zero = jnp.ones_like(c8), jnp.zeros_like(c8)
    rope8 = [jnp.stack([c8, one]), jnp.stack([sa8, zero]), jnp.stack([sb8, zero])]
    axial = [jnp.stack([cx, one]), jnp.stack([sax, zero]), jnp.stack([sbx, zero])]
    return rope8, axial


def _prep_layer(l, ffn1_norm, ffn1_w_gate, ffn1_w_up, ffn1_w_down, mix_norm, w_in, w_out,
                diff_lambda_q1, diff_lambda_k1, diff_lambda_q2, diff_lambda_k2, diff_out_norm,
                gla_gate_up_f, gla_gate_bias_f, gla_gate_up_b, gla_gate_bias_b, gla_out_norm,
                gqa_q_norm, gqa_k_norm, ffn2_norm, ffn2_w_gate, ffn2_w_up, ffn2_w_down):
    def row(v):
        return v.astype(F32).reshape(1, -1)

    gq = jnp.tile(gqa_q_norm[l].astype(F32) * (QK_SCALE * LOG2E), D_Q_HEADS)
    gk = jnp.tile(gqa_k_norm[l].astype(F32), D_KV_HEADS)
    wup_f = jnp.zeros((2 * LANE, C_HEADS * HEAD_DIM), F32).at[0:C_RANK].set(gla_gate_up_f[l])
    wup_b = jnp.zeros((2 * LANE, C_HEADS * HEAD_DIM), F32).at[C_RANK:2 * C_RANK].set(gla_gate_up_b[l])
    return dict(
        n1=row(ffn1_norm[l]), wg1=ffn1_w_gate[l].astype(BF16), wu1=ffn1_w_up[l].astype(BF16),
        wd1=ffn1_w_down[l].astype(BF16),
        nmix=row(mix_norm[l]), w_in=_prep_w_in(w_in[l]), w_out=_prep_w_out(w_out[l]),
        gd=jnp.concatenate([gq, gk]).reshape(1, -1),
        lam=jnp.stack([diff_lambda_q1[l], diff_lambda_k1[l], diff_lambda_q2[l], diff_lambda_k2[l]]).astype(F32),
        lam_init=0.8 - 0.6 * math.exp(-0.3 * l),
        gdiff=row(diff_out_norm[l]),
        wup_f=wup_f.astype(BF16), bias_f=row(gla_gate_bias_f[l]),
        wup_b=wup_b.astype(BF16), bias_b=row(gla_gate_bias_b[l]),
        ggla=row(gla_out_norm[l]),
        n2=row(ffn2_norm[l]), wg2=ffn2_w_gate[l].astype(BF16), wu2=ffn2_w_up[l].astype(BF16),
        wd2=ffn2_w_down[l].astype(BF16),
    )


def _in_projections(h, p, tabs, L):
    rope8, axial = tabs
    w_b, w_a, w_d = p["w_in"]
    half = ROPE_DIMS // 2
    ub = _proj(h, w_b, _q_col_scale(UB_W, UB_Q, UB_K), rope8, L, out_dtype=F32,
               n_rot_tiles=UB_ROPE_TILES, shift=half, head_norm=False, tn=PROJ_TN, name="inproj_b")
    ua = _proj(h, w_a, _q_col_scale(UA_W, UA_AQ, UA_AK), rope8, L, out_dtype=BF16,
               n_rot_tiles=UA_ROPE_TILES, shift=half, head_norm=False, tn=PROJ_TN, name="inproj_a")
    ud = _proj(h, w_d, p["gd"], axial, L, out_dtype=BF16, n_rot_tiles=UD_W // 1024,
               shift=HEAD_DIM // 4, head_norm=True, tn=1024, name="inproj_d")
    return ub, ua, ud


def _trunk(x, layers, final_g):
    B, L, D = x.shape
    xt = x.reshape(B * L, D)
    tabs = _rope_tables(L)
    h_ffn1 = None
    for l, p in enumerate(layers):
        xt, h = _ffn(xt, p["n1"] if h_ffn1 is None else h_ffn1, p["wg1"], p["wu1"], p["wd1"], p["nmix"],
                     tail="next_norm")
        ub, ua, ud = _in_projections(h, p, tabs, L)
        ua3 = ua.reshape(B, L, UA_W)
        oa = _diff_attention(ua3, p["lam"], p["gdiff"], p["lam_init"]).reshape(B * L, -1)
        ob = _dilated_attention(ub.reshape(B, L, UB_W)).reshape(B * L, -1)
        oc = _gla(ua3, p["wup_f"], p["bias_f"], p["wup_b"], p["bias_b"], p["ggla"]).reshape(B * L, -1)
        od = _gqa_attention(ud.reshape(B, L, UD_W), ua3).reshape(B * L, -1)
        xt, h2 = _outproj(xt, oa, ob, oc, od, *p["w_out"], p["n2"])
        if l == len(layers) - 1:
            xt = _ffn(xt, h2, p["wg2"], p["wu2"], p["wd2"], final_g, tail="final_norm")
        else:
            xt, h_ffn1 = _ffn(xt, h2, p["wg2"], p["wu2"], p["wd2"], layers[l + 1]["n1"], tail="next_norm")
    return xt.reshape(B, L, D)


def kernel(x_prompt, x_sample, ffn1_norm, ffn1_w_gate, ffn1_w_up, ffn1_w_down, mix_norm, w_in, w_out, diff_lambda_q1, diff_lambda_k1, diff_lambda_q2, diff_lambda_k2, diff_out_norm, gla_gate_up_f, gla_gate_bias_f, gla_gate_up_b, gla_gate_bias_b, gla_out_norm, gqa_q_norm, gqa_k_norm, ffn2_norm, ffn2_w_gate, ffn2_w_up, ffn2_w_down, final_norm):
    depth = w_in.shape[0]
    layers = [_prep_layer(l, ffn1_norm, ffn1_w_gate, ffn1_w_up, ffn1_w_down, mix_norm, w_in, w_out,
                          diff_lambda_q1, diff_lambda_k1, diff_lambda_q2, diff_lambda_k2, diff_out_norm,
                          gla_gate_up_f, gla_gate_bias_f, gla_gate_up_b, gla_gate_bias_b, gla_out_norm,
                          gqa_q_norm, gqa_k_norm, ffn2_norm, ffn2_w_gate, ffn2_w_up, ffn2_w_down)
              for l in range(depth)]
    final_g = final_norm.astype(F32).reshape(1, -1)
    return (_trunk(x_prompt, layers, final_g), _trunk(x_sample, layers, final_g))
```

```python
import functools
import math

import jax
import jax.numpy as jnp
from jax import lax
from jax.experimental import pallas as pl
from jax.experimental.pallas import tpu as pltpu

F32 = jnp.float32
BF16 = jnp.bfloat16

HEAD_DIM = 64
EPS = 1e-6
ROPE_THETA = 500000.0
ROPE_DIMS = HEAD_DIM // 4
AXIAL_THETA = 10000.0
GRID_W = 64
A_HEADS = 6
B_HEADS = 12
B_PATTERNS = ((128, 1), (512, 4), (2048, 16))
B_SIDE = 64
C_HEADS = 6
C_RANK = 16
C_CHUNK = 64
C_GATE_NORM = 16.0
D_Q_HEADS = 12
D_KV_HEADS = 4
D_GROUP = D_Q_HEADS // D_KV_HEADS
QK_SCALE = HEAD_DIM ** -0.5
LOG2E = math.log2(math.e)

LANE = 128
NEG = -1e30

PROJ_TN = 768
UB_Q = 0
UB_K = 768
UB_V = 1536
UB_W = 2304
UB_ROPE_TILES = 2
UA_AQ = 0
UA_AK = 768
UA_AV = 1536
UA_DV = 2304
UA_CQ = 2816
UA_CK = 3200
UA_CV = 3584
UA_CG = 4352
UA_CLOW = 5120
UA_W = 5376
UA_ROPE_TILES = 2
UD_Q = 0
UD_K = 768
UD_W = 1024
D_HEAD_ORDER = tuple(D_GROUP * (2 * gp + half) + j
                     for gp in range(D_KV_HEADS // 2) for j in range(D_GROUP) for half in range(2))


V7X_VMEM_MIB = 64
VMEM_LIMIT_MIB = V7X_VMEM_MIB - 8


def _cparams(sem):
    return pltpu.CompilerParams(dimension_semantics=sem, vmem_limit_bytes=VMEM_LIMIT_MIB * 1024 * 1024)


def _dot(a, b):
    return jnp.dot(a, b, preferred_element_type=F32)


def _dot_nt(a, b):
    return lax.dot_general(a, b, (((1,), (1,)), ((), ())), preferred_element_type=F32)


def _rms(x, g):
    ms = jnp.mean(x * x, axis=-1, keepdims=True)
    return x * lax.rsqrt(ms + EPS) * g


def _ffn_body(x_ref, hg_ref, wg_ref, wu_ref, wd_ref, ng_ref, *rest, tail, h_given):
    rest = list(rest)
    o_ref = rest.pop(0)
    hn_ref = rest.pop(0) if tail == "next_norm" else None
    h_ref = hg_ref if h_given else rest.pop(0)
    acc_ref = rest.pop(0)
    j = pl.program_id(1)
    nj = pl.num_programs(1)

    @pl.when(j == 0)
    def _():
        if not h_given:
            h_ref[...] = _rms(x_ref[...], hg_ref[...]).astype(BF16)
        acc_ref[...] = jnp.zeros(acc_ref.shape, F32)

    h = h_ref[...]
    gate = _dot(h, wg_ref[...])
    up = _dot(h, wu_ref[...])
    act = (gate * (1.0 / (1.0 + jnp.exp(-gate))) * up).astype(BF16)
    acc_ref[...] += _dot(act, wd_ref[...])

    @pl.when(j == nj - 1)
    def _():
        y = x_ref[...] + 0.5 * acc_ref[...]
        if tail == "final_norm":
            y = _rms(y, ng_ref[...])
        o_ref[...] = y
        if tail == "next_norm":
            hn_ref[...] = _rms(y, ng_ref[...]).astype(BF16)


FFN_TF = 512


def _ffn(x, hg, wg, wu, wd, ng, *, tail, tm=512):
    T, D = x.shape
    FF = wd.shape[0]
    tm = min(tm, T)
    tf = min(FFN_TF, FF)
    assert T % tm == 0 and FF % tf == 0
    h_given = hg.shape[0] == T
    row_spec = pl.BlockSpec((tm, D), lambda i, j: (i, 0))
    scratch = [] if h_given else [pltpu.VMEM((tm, D), BF16)]
    if tail == "next_norm":
        out_shape = [jax.ShapeDtypeStruct((T, D), F32), jax.ShapeDtypeStruct((T, D), BF16)]
        out_specs = [row_spec, row_spec]
    else:
        out_shape = jax.ShapeDtypeStruct((T, D), F32)
        out_specs = row_spec
    return pl.pallas_call(
        functools.partial(_ffn_body, tail=tail, h_given=h_given),
        out_shape=out_shape,
        grid=(T // tm, FF // tf),
        in_specs=[
            row_spec,
            row_spec if h_given else pl.BlockSpec((1, D), lambda i, j: (0, 0)),
            pl.BlockSpec((D, tf), lambda i, j: (0, j)),
            pl.BlockSpec((D, tf), lambda i, j: (0, j)),
            pl.BlockSpec((tf, D), lambda i, j: (j, 0)),
            pl.BlockSpec((1, D), lambda i, j: (0, 0)),
        ],
        out_specs=out_specs,
        scratch_shapes=scratch + [pltpu.VMEM((tm, D), F32)],
        compiler_params=_cparams(("parallel", "arbitrary")),
        name="ffn",
    )(x, hg, wg, wu, wd, ng)


def _rot(x, c, sa, sb, shift):
    return x * c + pltpu.roll(x, shift, 1) * sa + pltpu.roll(x, LANE - shift, 1) * sb


def _proj_body(h_ref, w_ref, cs_ref, c_ref, sa_ref, sb_ref, o_ref, *, tr, shift, head_norm):
    tm, tn = o_ref.shape
    for rc in range(tm // tr):
        rows = slice(rc * tr, (rc + 1) * tr)
        acc = _dot(h_ref[rows, :], w_ref[...])
        c_t, sa_t, sb_t = c_ref[rows, :], sa_ref[rows, :], sb_ref[rows, :]
        for c in range(tn // LANE):
            sl = slice(c * LANE, (c + 1) * LANE)
            y = acc[:, sl]
            if head_norm:
                low = lax.broadcasted_iota(jnp.int32, y.shape, 1) < HEAD_DIM
                sq = y * y
                ss_low = jnp.sum(jnp.where(low, sq, 0.0), axis=-1, keepdims=True)
                ss_high = jnp.sum(jnp.where(low, 0.0, sq), axis=-1, keepdims=True)
                ms = jnp.where(low, ss_low, ss_high) * (1.0 / HEAD_DIM)
                y = y * lax.rsqrt(ms + EPS) * cs_ref[:, sl]
                y = _rot(y, c_t, sa_t, sb_t, shift)
            else:
                y = _rot(y, c_t, sa_t, sb_t, shift) * cs_ref[:, sl]
            o_ref[rows, sl] = y.astype(o_ref.dtype)


def _proj(h, w, cs, tabs, L, *, out_dtype, n_rot_tiles, shift, head_norm, tn, name, tm=1024, tr=256):
    T, D = h.shape
    W = w.shape[1]
    tm = min(tm, L)
    tr = min(tr, tm)
    assert T % tm == 0 and L % tm == 0 and W % tn == 0 and tm % tr == 0
    lt = L // tm
    tab_spec = pl.BlockSpec((None, tm, LANE), lambda i, j: (jnp.where(j < n_rot_tiles, 0, 1), i % lt, 0))
    return pl.pallas_call(
        functools.partial(_proj_body, tr=tr, shift=shift, head_norm=head_norm),
        out_shape=jax.ShapeDtypeStruct((T, W), out_dtype),
        grid=(T // tm, W // tn),
        in_specs=[
            pl.BlockSpec((tm, D), lambda i, j: (i, 0)),
            pl.BlockSpec((D, tn), lambda i, j: (0, j)),
            pl.BlockSpec((1, tn), lambda i, j: (0, j)),
            tab_spec, tab_spec, tab_spec,
        ],
        out_specs=pl.BlockSpec((tm, tn), lambda i, j: (i, j)),
        compiler_params=_cparams(("parallel", "parallel")),
        name=name,
    )(h, w, cs, *tabs)


def _flash_init(m_ref, acc_ref):
    m_ref[...] = jnp.full(m_ref.shape, NEG, F32)
    acc_ref[...] = jnp.zeros(acc_ref.shape, F32)


def _flash_step(qq_ref, k_ref, vs, m_ref, acc_ref, l_ref=None):
    s = _dot_nt(qq_ref[...], k_ref[...])
    chunks = [s[:, c * LANE:(c + 1) * LANE] for c in range(s.shape[1] // LANE)]
    mx = chunks[0]
    for sc in chunks[1:]:
        mx = jnp.maximum(mx, sc)
    m_prev = m_ref[...]
    m_new = jnp.maximum(m_prev, jnp.max(mx, axis=-1, keepdims=True))
    alpha = jnp.exp2(m_prev - m_new)
    ps = [jnp.exp2(sc - m_new) for sc in chunks]
    if l_ref is not None:
        lsum = ps[0]
        for pc in ps[1:]:
            lsum = lsum + pc
        l_ref[...] = alpha * l_ref[...] + lsum
    p = jnp.concatenate([pc.astype(BF16) for pc in ps], axis=1)
    share = p.shape[0] // len(vs)
    for t, v in enumerate(vs):
        rows = slice(t * share, (t + 1) * share)
        acc_ref[rows, :] = alpha[rows] * acc_ref[rows, :] + _dot(p[rows], v)
    m_ref[...] = m_new


def _flash_out(acc_ref, l_ref=None, ones_lane=None):
    acc = acc_ref[...]
    if l_ref is not None:
        den = jnp.sum(l_ref[...], axis=-1, keepdims=True)
    else:
        lane = lax.broadcasted_iota(jnp.int32, acc.shape, 1)
        den = jnp.sum(jnp.where(lane == ones_lane, acc, 0.0), axis=-1, keepdims=True)
    return acc * (1.0 / den)


def _diff_body(q_ref, k_ref, v_ref, lam_ref, gn_ref, o_ref, qq_ref, m_ref, l_ref, acc_ref, *, tq, lam_init):
    ki = pl.program_id(3)

    @pl.when(ki == 0)
    def _():
        q = q_ref[...]
        lane = lax.broadcasted_iota(jnp.int32, q.shape, 1)
        zero = jnp.zeros_like(q)
        qq_ref[0:tq, :] = jnp.where(lane < HEAD_DIM, q, zero)
        qq_ref[tq:2 * tq, :] = jnp.where(lane >= HEAD_DIM, q, zero)
        _flash_init(m_ref, acc_ref)
        l_ref[...] = jnp.zeros(l_ref.shape, F32)

    _flash_step(qq_ref, k_ref, [v_ref[...]], m_ref, acc_ref, l_ref)

    @pl.when(ki == pl.num_programs(3) - 1)
    def _():
        o = _flash_out(acc_ref, l_ref=l_ref)
        lv = lam_ref[...]
        lam = (jnp.exp(jnp.sum(lv[0:1] * lv[1:2], axis=-1, keepdims=True))
               - jnp.exp(jnp.sum(lv[2:3] * lv[3:4], axis=-1, keepdims=True)) + lam_init)
        a = o[0:tq] - lam * o[tq:2 * tq]
        o_ref[...] = (_rms(a, gn_ref[...]) * (1.0 - lam_init)).astype(BF16)


def _diff_attention(u3, lam_vecs, gn, lam_init, *, tq=1024, tk=2048):
    B, L, _ = u3.shape
    tq = min(tq, L)
    tk = min(tk, L)
    qb, kb, vb = UA_AQ // LANE, UA_AK // LANE, UA_AV // LANE
    return pl.pallas_call(
        functools.partial(_diff_body, tq=tq, lam_init=lam_init),
        out_shape=jax.ShapeDtypeStruct((B, L, A_HEADS * LANE), BF16),
        grid=(B, A_HEADS, L // tq, L // tk),
        in_specs=[
            pl.BlockSpec((None, tq, LANE), lambda b, h, qi, ki: (b, qi, qb + h)),
            pl.BlockSpec((None, tk, LANE), lambda b, h, qi, ki: (b, ki, kb + h)),
            pl.BlockSpec((None, tk, LANE), lambda b, h, qi, ki: (b, ki, vb + h)),
            pl.BlockSpec((4, HEAD_DIM), lambda b, h, qi, ki: (0, 0)),
            pl.BlockSpec((1, LANE), lambda b, h, qi, ki: (0, 0)),
        ],
        out_specs=pl.BlockSpec((None, tq, LANE), lambda b, h, qi, ki: (b, qi, h)),
        scratch_shapes=[
            pltpu.VMEM((2 * tq, LANE), BF16),
            pltpu.VMEM((2 * tq, LANE), F32),
            pltpu.VMEM((2 * tq, LANE), F32),
            pltpu.VMEM((2 * tq, LANE), F32),
        ],
        compiler_params=_cparams(("parallel", "parallel", "parallel", "arbitrary")),
        name="diff_attn",
    )(u3, u3, u3, lam_vecs, gn)


def _gqa_body(q_ref, k_ref, v0_ref, v1_ref, o_ref, qq_ref, m_ref, acc_ref, *, tq):
    ki = pl.program_id(3)

    @pl.when(ki == 0)
    def _():
        for half in range(2):
            for j in range(D_GROUP):
                q = q_ref[:, j * LANE:(j + 1) * LANE]
                lane = lax.broadcasted_iota(jnp.int32, q.shape, 1)
                own = (lane >= HEAD_DIM) if half == 1 else (lane < HEAD_DIM)
                r0 = (half * D_GROUP + j) * tq
                qq_ref[r0:r0 + tq, :] = jnp.where(own, q, jnp.zeros_like(q))
        _flash_init(m_ref, acc_ref)

    def with_ones(v_ref):
        v = v_ref[...]
        lane_v = lax.broadcasted_iota(jnp.int32, v.shape, 1)
        return jnp.where(lane_v == HEAD_DIM, jnp.ones_like(v), v)

    _flash_step(qq_ref, k_ref, [with_ones(v0_ref), with_ones(v1_ref)], m_ref, acc_ref)

    @pl.when(ki == pl.num_programs(3) - 1)
    def _():
        o = _flash_out(acc_ref, ones_lane=HEAD_DIM)
        low = lax.broadcasted_iota(jnp.int32, (tq, LANE), 1) < HEAD_DIM
        for j in range(D_GROUP):
            even = o[j * tq:(j + 1) * tq]
            odd = pltpu.roll(o[(D_GROUP + j) * tq:(D_GROUP + j + 1) * tq], HEAD_DIM, 1)
            o_ref[:, j * LANE:(j + 1) * LANE] = jnp.where(low, even, odd).astype(BF16)


def _gqa_attention(ud3, ua3, *, tq=512, tk=2048):
    B, L, _ = ud3.shape
    tq = min(tq, L)
    tk = min(tk, L)
    gw = D_GROUP * LANE
    R = 2 * D_GROUP * tq
    qb, kb, vb = UD_Q // gw, UD_K // LANE, UA_DV // LANE
    return pl.pallas_call(
        functools.partial(_gqa_body, tq=tq),
        out_shape=jax.ShapeDtypeStruct((B, L, D_Q_HEADS * HEAD_DIM), BF16),
        grid=(B, D_KV_HEADS // 2, L // tq, L // tk),
        in_specs=[
            pl.BlockSpec((None, tq, gw), lambda b, gp, qi, ki: (b, qi, qb + gp)),
            pl.BlockSpec((None, tk, LANE), lambda b, gp, qi, ki: (b, ki, kb + gp)),
            pl.BlockSpec((None, tk, LANE), lambda b, gp, qi, ki: (b, ki, vb + 2 * gp)),
            pl.BlockSpec((None, tk, LANE), lambda b, gp, qi, ki: (b, ki, vb + 2 * gp + 1)),
        ],
        out_specs=pl.BlockSpec((None, tq, gw), lambda b, gp, qi, ki: (b, qi, gp)),
        scratch_shapes=[
            pltpu.VMEM((R, LANE), BF16),
            pltpu.VMEM((R, LANE), F32),
            pltpu.VMEM((R, LANE), F32),
        ],
        compiler_params=_cparams(("parallel", "parallel", "parallel", "arbitrary")),
        name="gqa_attn",
    )(ud3, ud3, ua3, ua3)


def _dil_body(q_ref, k_ref, kp_ref, kn_ref, v_ref, vp_ref, vn_ref, o_ref, kbuf, vbuf, m_ref, l_ref, acc_ref,
              *, TT, HALO, L):
    i = pl.program_id(2)
    H = B_SIDE
    kbuf[0:HALO, :] = kp_ref[...]
    kbuf[HALO:HALO + TT, :] = k_ref[...]
    kbuf[HALO + TT:HALO + TT + HALO, :] = kn_ref[...]
    vbuf[0:HALO, :] = vp_ref[...]
    vbuf[HALO:HALO + TT, :] = v_ref[...]
    vbuf[HALO + TT:HALO + TT + HALO, :] = vn_ref[...]

    tq = LANE
    tw = tq + 2 * H
    head0 = lax.broadcasted_iota(jnp.int32, (tq, LANE), 1) < HEAD_DIM
    row = lax.broadcasted_iota(jnp.int32, (2 * tq, tw), 0)
    col = lax.broadcasted_iota(jnp.int32, (2 * tq, tw), 1)
    band_bias = jnp.where(jnp.abs(col - H - (row % tq)) <= H, 0.0, NEG)
    col1 = lax.broadcasted_iota(jnp.int32, (1, tw), 1)

    def per_head(x):
        return jnp.where(head0, jnp.broadcast_to(x[0:tq], (tq, LANE)), jnp.broadcast_to(x[tq:2 * tq], (tq, LANE)))

    for p, (_, d) in enumerate(reversed(B_PATTERNS)):
        n_sub = L // d

        def tile(idx, carry, d=d, p=p, n_sub=n_sub):
            r = idx % d
            j = idx // d
            q_start = r + d * (j * tq)
            k_start = HALO + r + d * (j * tq - H)
            if d == 1:
                q_rows, k_rows = pl.ds(q_start, tq), pl.ds(k_start, tw)
            else:
                q_rows, k_rows = pl.ds(q_start, tq, stride=d), pl.ds(k_start, tw, stride=d)
            q = q_ref[q_rows, :].astype(BF16)
            zero = jnp.zeros_like(q)
            qq = jnp.concatenate([jnp.where(head0, q, zero), jnp.where(head0, zero, q)], axis=0)
            kw = kbuf[k_rows, :].astype(BF16)
            vw = vbuf[k_rows, :].astype(BF16)
            n0 = (i * TT) // d + j * tq - H
            in_seq = jnp.where((col1 >= -n0) & (col1 < n_sub - n0), 0.0, NEG)
            s = _dot_nt(qq, kw) + band_bias + in_seq
            m = jnp.max(s, axis=-1, keepdims=True)
            e = jnp.exp2(s - m)
            m_t = per_head(m)
            l_t = per_head(jnp.sum(e, axis=-1, keepdims=True))
            pv = _dot(e.astype(BF16), vw)
            pv_t = jnp.where(head0, pv[0:tq], pv[tq:2 * tq])
            if p == 0:
                m_ref[q_rows, :] = m_t
                l_ref[q_rows, :] = l_t
                acc_ref[q_rows, :] = pv_t
            else:
                m_old = m_ref[q_rows, :]
                m_new = jnp.maximum(m_old, m_t)
                a_old = jnp.exp2(m_old - m_new)
                a_t = jnp.exp2(m_t - m_new)
                m_ref[q_rows, :] = m_new
                l_ref[q_rows, :] = a_old * l_ref[q_rows, :] + a_t * l_t
                acc_ref[q_rows, :] = a_old * acc_ref[q_rows, :] + a_t * pv_t
            return carry

        lax.fori_loop(0, TT // tq, tile, 0, unroll=True)

    o_ref[...] = (acc_ref[...] * (1.0 / l_ref[...])).astype(BF16)


B_TILE = 2048
B_HALO = 1024


def _dilated_attention(ub3):
    B, L, _ = ub3.shape
    TT, HALO = B_TILE, B_HALO
    assert L % TT == 0 and TT % HALO == 0 and HALO >= B_SIDE * max(d for _, d in B_PATTERNS)
    pairs = B_HEADS // 2
    qb, kb, vb = UB_Q // LANE, UB_K // LANE, UB_V // LANE
    th = TT // HALO
    nh = L // HALO

    def main(base):
        return pl.BlockSpec((None, TT, LANE), lambda b, hp, i: (b, i, base + hp))

    def prev(base):
        return pl.BlockSpec((None, HALO, LANE), lambda b, hp, i: (b, jnp.maximum(i * th - 1, 0), base + hp))

    def nxt(base):
        return pl.BlockSpec((None, HALO, LANE), lambda b, hp, i: (b, jnp.minimum((i + 1) * th, nh - 1), base + hp))

    return pl.pallas_call(
        functools.partial(_dil_body, TT=TT, HALO=HALO, L=L),
        out_shape=jax.ShapeDtypeStruct((B, L, pairs * LANE), BF16),
        grid=(B, pairs, L // TT),
        in_specs=[main(qb), main(kb), prev(kb), nxt(kb), main(vb), prev(vb), nxt(vb)],
        out_specs=pl.BlockSpec((None, TT, LANE), lambda b, hp, i: (b, i, hp)),
        scratch_shapes=[pltpu.VMEM((TT + 2 * HALO, LANE), F32), pltpu.VMEM((TT + 2 * HALO, LANE), F32),
                        pltpu.VMEM((TT, LANE), F32), pltpu.VMEM((TT, LANE), F32), pltpu.VMEM((TT, LANE), F32)],
        compiler_params=_cparams(("parallel", "parallel", "parallel")),
        name="dilated_attn",
    )(ub3, ub3, ub3, ub3, ub3, ub3, ub3)


def _gla_body(*refs, R, backward):
    if backward:
        (q_ref, k_ref, v_ref, low_ref, wup_ref, bias_ref, fwd_ref, cg_ref, gn_ref, o_ref, s_ref) = refs
    else:
        (q_ref, k_ref, v_ref, low_ref, wup_ref, bias_ref, o_ref, s_ref) = refs
    C = C_CHUNK
    n = R // C

    @pl.when(pl.program_id(2) == 0)
    def _():
        s_ref[...] = jnp.zeros(s_ref.shape, F32)

    x = _dot(low_ref[...], wup_ref[...]) + bias_ref[...]
    gl = (jnp.minimum(x, 0.0) - jnp.log(1.0 + jnp.exp(-jnp.abs(x)))) * (1.0 / C_GATE_NORM)

    lane = lax.broadcasted_iota(jnp.int32, (C, LANE), 1)
    qrow = lax.broadcasted_iota(jnp.int32, (C, LANE), 0)
    key = lane % C
    tri_wide = (key >= qrow) if backward else (key <= qrow)
    head0 = lane < HEAD_DIM
    lane2 = lax.broadcasted_iota(jnp.int32, (C, 2 * LANE), 1)
    vhead0 = lane2 < LANE
    rr = lax.broadcasted_iota(jnp.int32, (LANE, 2 * LANE), 0)
    cc = lax.broadcasted_iota(jnp.int32, (LANE, 2 * LANE), 1)
    blockdiag = (rr < HEAD_DIM) == (cc < LANE)
    eye = (lax.broadcasted_iota(jnp.int32, (LANE, LANE), 0) == lax.broadcasted_iota(jnp.int32, (LANE, LANE), 1))
    zpad = jnp.zeros((C, LANE), F32)
    zpad_v = jnp.zeros((C, 2 * LANE), BF16)

    chunk_rows = [slice(c * C, (c + 1) * C) for c in range(n)]
    pos = lax.broadcasted_iota(jnp.int32, (R, LANE), 0) % C
    cum_all = gl
    for sft in (1, 2, 4, 8, 16, 32):
        if backward:
            cum_all = cum_all + jnp.where(pos < C - sft, pltpu.roll(cum_all, R - sft, 0), 0.0)
        else:
            cum_all = cum_all + jnp.where(pos >= sft, pltpu.roll(cum_all, sft, 0), 0.0)
    cums = [cum_all[rows] for rows in chunk_rows]
    lasts = [(cum[0:1] if backward else cum[C - 1:C]) for cum in cums]
    qds, atts, intra, upds, dec_cols = [], [], [], [], []
    for rows, cum, last in zip(chunk_rows, cums, lasts):
        kc = k_ref[rows, :].astype(F32)
        qds.append((q_ref[rows, :].astype(F32) * jnp.exp(cum)).astype(BF16))
        kinv = kc * jnp.exp(-cum)
        zero = jnp.zeros_like(kinv)
        kbd = jnp.concatenate([jnp.where(head0, kinv, zero), jnp.where(head0, zero, kinv)], axis=0).astype(BF16)
        atts.append(jnp.where(tri_wide, _dot_nt(qds[-1], kbd), 0.0).astype(BF16))
        kdec = kc * jnp.exp(last - cum)
        kdec_t = jnp.concatenate([kdec, zpad], axis=0).T.astype(BF16)
        vc = v_ref[rows, :]
        upd = _dot(kdec_t, jnp.concatenate([vc, zpad_v], axis=0))
        upds.append(jnp.where(blockdiag, upd, 0.0))
        dec_row = jnp.broadcast_to(jnp.exp(last), (LANE, LANE))
        dec_cols.append(jnp.sum(jnp.where(eye, dec_row, 0.0), axis=-1, keepdims=True))
    for rows, att in zip(chunk_rows, atts):
        vc = v_ref[rows, :]
        zv = jnp.zeros_like(vc)
        vbd = jnp.concatenate([jnp.where(vhead0, vc, zv), jnp.where(vhead0, zv, vc)], axis=0)
        intra.append(_dot(att, vbd))

    state = s_ref[...]
    entering = [None] * n
    for c in (range(n - 1, -1, -1) if backward else range(n)):
        entering[c] = state.astype(BF16)
        state = dec_cols[c] * state + upds[c]
    s_ref[...] = state

    if backward:
        gate = cg_ref[...].astype(F32)
        gate = gate * (1.0 / (1.0 + jnp.exp(-gate)))
    for c, rows in enumerate(chunk_rows):
        o = intra[c] + _dot(qds[c], entering[c])
        if backward:
            tot = o + fwd_ref[rows, :]
            for h in range(2):
                hs = slice(h * LANE, (h + 1) * LANE)
                o_ref[rows, hs] = (_rms(tot[:, hs], gn_ref[...]) * gate[rows, hs]).astype(BF16)
        else:
            o_ref[rows, :] = o


def _gla(u3, wup_f, bias_f, wup_b, bias_b, gn, *, R=2048):
    B, L, _ = u3.shape
    R = min(R, L)
    nb = L // R
    pairs = C_HEADS // 2
    W = C_HEADS * LANE
    qb, kb = UA_CQ // LANE, UA_CK // LANE
    vb, gb, lb = UA_CV // (2 * LANE), UA_CG // (2 * LANE), UA_CLOW // (2 * LANE)

    def specs(rev):
        def ri(i):
            return nb - 1 - i if rev else i
        return dict(
            q=pl.BlockSpec((None, R, LANE), lambda b, p, i: (b, ri(i), qb + p)),
            k=pl.BlockSpec((None, R, LANE), lambda b, p, i: (b, ri(i), kb + p)),
            v=pl.BlockSpec((None, R, 2 * LANE), lambda b, p, i: (b, ri(i), vb + p)),
            low=pl.BlockSpec((None, R, 2 * LANE), lambda b, p, i: (b, ri(i), lb)),
            wup=pl.BlockSpec((2 * LANE, LANE), lambda b, p, i: (0, p)),
            bias=pl.BlockSpec((1, LANE), lambda b, p, i: (0, p)),
            out=pl.BlockSpec((None, R, 2 * LANE), lambda b, p, i: (b, ri(i), p)),
            cg=pl.BlockSpec((None, R, 2 * LANE), lambda b, p, i: (b, ri(i), gb + p)),
            gn=pl.BlockSpec((1, LANE), lambda b, p, i: (0, 0)),
        )

    sem = _cparams(("parallel", "parallel", "arbitrary"))
    sf = specs(False)
    fwd = pl.pallas_call(
        functools.partial(_gla_body, R=R, backward=False),
        out_shape=jax.ShapeDtypeStruct((B, L, W), F32),
        grid=(B, pairs, nb),
        in_specs=[sf["q"], sf["k"], sf["v"], sf["low"], sf["wup"], sf["bias"]],
        out_specs=sf["out"],
        scratch_shapes=[pltpu.VMEM((LANE, 2 * LANE), F32)],
        compiler_params=sem,
        name="gla_fwd",
    )(u3, u3, u3, u3, wup_f, bias_f)
    sb = specs(True)
    return pl.pallas_call(
        functools.partial(_gla_body, R=R, backward=True),
        out_shape=jax.ShapeDtypeStruct((B, L, W), BF16),
        grid=(B, pairs, nb),
        in_specs=[sb["q"], sb["k"], sb["v"], sb["low"], sb["wup"], sb["bias"], sb["out"], sb["cg"], sb["gn"]],
        out_specs=sb["out"],
        scratch_shapes=[pltpu.VMEM((LANE, 2 * LANE), F32)],
        compiler_params=sem,
        name="gla_bwd",
    )(u3, u3, u3, u3, wup_b, bias_b, fwd, u3, gn)


def _outproj_body(x_ref, a_ref, b_ref, c_ref, d_ref, wa_ref, wb_ref, wc_ref, wd_ref, ng_ref, o_ref, hn_ref):
    acc = _dot(a_ref[...], wa_ref[...]) + _dot(b_ref[...], wb_ref[...])
    acc = acc + _dot(c_ref[...], wc_ref[...]) + _dot(d_ref[...], wd_ref[...])
    y = x_ref[...] + acc
    o_ref[...] = y
    hn_ref[...] = _rms(y, ng_ref[...]).astype(BF16)


def _outproj(x, oa, ob, oc, od, wa, wb, wc, wd, ng, *, tm=512):
    T, D = x.shape
    tm = min(tm, T)
    row_spec = pl.BlockSpec((tm, D), lambda i: (i, 0))

    def act(w):
        return pl.BlockSpec((tm, w), lambda i: (i, 0))

    def wt(w):
        return pl.BlockSpec((w, D), lambda i: (0, 0), pipeline_mode=pl.Buffered(1))

    widths = [oa.shape[1], ob.shape[1], oc.shape[1], od.shape[1]]
    return pl.pallas_call(
        _outproj_body,
        out_shape=[jax.ShapeDtypeStruct((T, D), F32), jax.ShapeDtypeStruct((T, D), BF16)],
        grid=(T // tm,),
        in_specs=[row_spec] + [act(w) for w in widths] + [wt(w) for w in widths]
        + [pl.BlockSpec((1, D), lambda i: (0, 0))],
        out_specs=[row_spec, row_spec],
        compiler_params=_cparams(("parallel",)),
        name="outproj",
    )(x, oa, ob, oc, od, wa, wb, wc, wd, ng)


def _pad_heads(w, n):
    k = w.shape[0]
    return jnp.pad(w.reshape(k, n, HEAD_DIM), ((0, 0), (0, 0), (0, LANE - HEAD_DIM))).reshape(k, n * LANE)


def _prep_w_in(w):
    sizes = (768, 768, 768, 768, 768, 768, 384, 384, 768, 768, 32, 768, 256, 256)
    offs = [0]
    for s in sizes:
        offs.append(offs[-1] + s)
    (a_q, a_k, a_v, b_q, b_k, b_v, c_q, c_k, c_v, c_g, c_low, d_q, d_k, d_v) = [
        w[:, offs[i]:offs[i + 1]] for i in range(len(sizes))]
    low = jnp.pad(c_low, ((0, 0), (0, 2 * LANE - 2 * C_RANK)))
    wb = jnp.concatenate([b_q, b_k, b_v], axis=1).astype(BF16)
    wa = jnp.concatenate([a_q, a_k, a_v, _pad_heads(d_v, D_KV_HEADS),
                          c_q * QK_SCALE, c_k, c_v, c_g, low], axis=1).astype(BF16)
    order = jnp.asarray(D_HEAD_ORDER)
    d_q = jnp.take(d_q.reshape(-1, D_Q_HEADS, HEAD_DIM), order, axis=1).reshape(-1, D_Q_HEADS * HEAD_DIM)
    wd = jnp.concatenate([d_q, d_k], axis=1).astype(BF16)
    assert wb.shape[1] == UB_W and wa.shape[1] == UA_W and wd.shape[1] == UD_W
    return wb, wa, wd


def _prep_w_out(w):
    wa = w[0:768]
    wb = w[768:1536]
    wc = w[1536:2304]
    wd = w[2304:3072]
    n = w.shape[1]
    wd = jnp.take(wd.reshape(D_Q_HEADS, HEAD_DIM, n), jnp.asarray(D_HEAD_ORDER), axis=0)
    wd = wd.reshape(D_Q_HEADS * HEAD_DIM, n)
    return [t.astype(BF16) for t in (wa, wb, wc, wd)]


def _q_col_scale(width, q_lo, q_hi):
    col = jnp.arange(width)
    return jnp.where((col >= q_lo) & (col < q_hi), QK_SCALE * LOG2E, 1.0).astype(F32).reshape(1, -1)


def _rope_tables(L):
    t = jnp.arange(L, dtype=F32)
    lane = jnp.arange(LANE)
    l64 = lane % HEAD_DIM
    half = ROPE_DIMS // 2
    inv = ROPE_THETA ** (-jnp.arange(0, ROPE_DIMS, 2, dtype=F32) / ROPE_DIMS)
    ang = t[:, None] * inv[None, :]
    ang_l = ang[:, l64 % half]
    in_rot = (l64 < ROPE_DIMS)[None, :]
    c8 = jnp.where(in_rot, jnp.cos(ang_l), 1.0)
    sa8 = jnp.where(((l64 >= half) & (l64 < ROPE_DIMS))[None, :], jnp.sin(ang_l), 0.0)
    sb8 = jnp.where((l64 < half)[None, :], -jnp.sin(ang_l), 0.0)

    q = HEAD_DIM // 4
    inv2 = AXIAL_THETA ** (-jnp.arange(0, HEAD_DIM // 2, 2, dtype=F32) / (HEAD_DIM // 2))
    rows = L // GRID_W
    row_pos = jnp.repeat(jnp.arange(rows, dtype=F32), GRID_W)
    col_pos = jnp.tile(jnp.arange(GRID_W, dtype=F32), rows)
    ang_r = row_pos[:, None] * inv2[None, :]
    ang_c = col_pos[:, None] * inv2[None, :]
    ang_x = jnp.where((l64 < 2 * q)[None, :], ang_r[:, l64 % q], ang_c[:, l64 % q])
    cx = jnp.cos(ang_x)
    upper = ((l64 % (2 * q)) >= q)[None, :]
    sax = jnp.where(upper, jnp.sin(ang_x), 0.0)
    sbx = jnp.where(upper, 0.0, -jnp.sin(ang_x))
    one, zero = jnp.ones_like(c8), jnp.zeros_like(c8)
    rope8 = [jnp.stack([c8, one]), jnp.stack([sa8, zero]), jnp.stack([sb8, zero])]
    axial = [jnp.stack([cx, one]), jnp.stack([sax, zero]), jnp.stack([sbx, zero])]
    return rope8, axial


def _prep_layer(l, ffn1_norm, ffn1_w_gate, ffn1_w_up, ffn1_w_down, mix_norm, w_in, w_out,
                diff_lambda_q1, diff_lambda_k1, diff_lambda_q2, diff_lambda_k2, diff_out_norm,
                gla_gate_up_f, gla_gate_bias_f, gla_gate_up_b, gla_gate_bias_b, gla_out_norm,
                gqa_q_norm, gqa_k_norm, ffn2_norm, ffn2_w_gate, ffn2_w_up, ffn2_w_down):
    def row(v):
        return v.astype(F32).reshape(1, -1)

    gq = jnp.tile(gqa_q_norm[l].astype(F32) * (QK_SCALE * LOG2E), D_Q_HEADS)
    gk = jnp.tile(gqa_k_norm[l].astype(F32), D_KV_HEADS)
    wup_f = jnp.zeros((2 * LANE, C_HEADS * HEAD_DIM), F32).at[0:C_RANK].set(gla_gate_up_f[l])
    wup_b = jnp.zeros((2 * LANE, C_HEADS * HEAD_DIM), F32).at[C_RANK:2 * C_RANK].set(gla_gate_up_b[l])
    return dict(
        n1=row(ffn1_norm[l]), wg1=ffn1_w_gate[l].astype(BF16), wu1=ffn1_w_up[l].astype(BF16),
        wd1=ffn1_w_down[l].astype(BF16),
        nmix=row(mix_norm[l]), w_in=_prep_w_in(w_in[l]), w_out=_prep_w_out(w_out[l]),
        gd=jnp.concatenate([gq, gk]).reshape(1, -1),
        lam=jnp.stack([diff_lambda_q1[l], diff_lambda_k1[l], diff_lambda_q2[l], diff_lambda_k2[l]]).astype(F32),
        lam_init=0.8 - 0.6 * math.exp(-0.3 * l),
        gdiff=row(diff_out_norm[l]),
        wup_f=wup_f.astype(BF16), bias_f=row(gla_gate_bias_f[l]),
        wup_b=wup_b.astype(BF16), bias_b=row(gla_gate_bias_b[l]),
        ggla=row(gla_out_norm[l]),
        n2=row(ffn2_norm[l]), wg2=ffn2_w_gate[l].astype(BF16), wu2=ffn2_w_up[l].astype(BF16),
        wd2=ffn2_w_down[l].astype(BF16),
    )


def _in_projections(h, p, tabs, L):
    rope8, axial = tabs
    w_b, w_a, w_d = p["w_in"]
    half = ROPE_DIMS // 2
    ub = _proj(h, w_b, _q_col_scale(UB_W, UB_Q, UB_K), rope8, L, out_dtype=F32,
               n_rot_tiles=UB_ROPE_TILES, shift=half, head_norm=False, tn=PROJ_TN, name="inproj_b")
    ua = _proj(h, w_a, _q_col_scale(UA_W, UA_AQ, UA_AK), rope8, L, out_dtype=BF16,
               n_rot_tiles=UA_ROPE_TILES, shift=half, head_norm=False, tn=PROJ_TN, name="inproj_a")
    ud = _proj(h, w_d, p["gd"], axial, L, out_dtype=BF16, n_rot_tiles=UD_W // 1024,
               shift=HEAD_DIM // 4, head_norm=True, tn=1024, name="inproj_d")
    return ub, ua, ud


def _trunk(x, layers, final_g):
    B, L, D = x.shape
    xt = x.reshape(B * L, D)
    tabs = _rope_tables(L)
    h_ffn1 = None
    for l, p in enumerate(layers):
        xt, h = _ffn(xt, p["n1"] if h_ffn1 is None else h_ffn1, p["wg1"], p["wu1"], p["wd1"], p["nmix"],
                     tail="next_norm")
        ub, ua, ud = _in_projections(h, p, tabs, L)
        ua3 = ua.reshape(B, L, UA_W)
        oa = _diff_attention(ua3, p["lam"], p["gdiff"], p["lam_init"]).reshape(B * L, -1)
        ob = _dilated_attention(ub.reshape(B, L, UB_W)).reshape(B * L, -1)
        oc = _gla(ua3, p["wup_f"], p["bias_f"], p["wup_b"], p["bias_b"], p["ggla"]).reshape(B * L, -1)
        od = _gqa_attention(ud.reshape(B, L, UD_W), ua3).reshape(B * L, -1)
        xt, h2 = _outproj(xt, oa, ob, oc, od, *p["w_out"], p["n2"])
        if l == len(layers) - 1:
            xt = _ffn(xt, h2, p["wg2"], p["wu2"], p["wd2"], final_g, tail="final_norm")
        else:
            xt, h_ffn1 = _ffn(xt, h2, p["wg2"], p["wu2"], p["wd2"], layers[l + 1]["n1"], tail="next_norm")
    return xt.reshape(B, L, D)


def kernel(x_prompt, x_sample, ffn1_norm, ffn1_w_gate, ffn1_w_up, ffn1_w_down, mix_norm, w_in, w_out, diff_lambda_q1, diff_lambda_k1, diff_lambda_q2, diff_lambda_k2, diff_out_norm, gla_gate_up_f, gla_gate_bias_f, gla_gate_up_b, gla_gate_bias_b, gla_out_norm, gqa_q_norm, gqa_k_norm, ffn2_norm, ffn2_w_gate, ffn2_w_up, ffn2_w_down, final_norm):
    depth = w_in.shape[0]
    layers = [_prep_layer(l, ffn1_norm, ffn1_w_gate, ffn1_w_up, ffn1_w_down, mix_norm, w_in, w_out,
                          diff_lambda_q1, diff_lambda_k1, diff_lambda_q2, diff_lambda_k2, diff_out_norm,
                          gla_gate_up_f, gla_gate_bias_f, gla_gate_up_b, gla_gate_bias_b, gla_out_norm,
                          gqa_q_norm, gqa_k_norm, ffn2_norm, ffn2_w_gate, ffn2_w_up, ffn2_w_down)
              for l in range(depth)]
    final_g = final_norm.astype(F32).reshape(1, -1)
    return (_trunk(x_prompt, layers, final_g), _trunk(x_sample, layers, final_g))
```

```python
import functools
import math

import jax
import jax.numpy as jnp
from jax import lax
from jax.experimental import pallas as pl
from jax.experimental.pallas import tpu as pltpu

F32 = jnp.float32
BF16 = jnp.bfloat16

HEAD_DIM = 64
EPS = 1e-6
ROPE_THETA = 500000.0
ROPE_DIMS = HEAD_DIM // 4
AXIAL_THETA = 10000.0
GRID_W = 64
A_HEADS = 6
B_HEADS = 12
B_PATTERNS = ((128, 1), (512, 4), (2048, 16))
B_SIDE = 64
C_HEADS = 6
C_RANK = 16
C_CHUNK = 64
C_GATE_NORM = 16.0
D_Q_HEADS = 12
D_KV_HEADS = 4
D_GROUP = D_Q_HEADS // D_KV_HEADS
QK_SCALE = HEAD_DIM ** -0.5
LOG2E = math.log2(math.e)

LANE = 128
NEG = -1e30

PROJ_TN = 768
UB_Q = 0
UB_K = 768
UB_V = 1536
UB_W = 2304
UB_ROPE_TILES = 2
UA_AQ = 0
UA_AK = 768
UA_AV = 1536
UA_DV = 2304
UA_CQ = 2816
UA_CK = 3200
UA_CV = 3584
UA_CG = 4352
UA_CLOW = 5120
UA_W = 5376
UA_ROPE_TILES = 2
UD_Q = 0
UD_K = 768
UD_W = 1024
D_HEAD_ORDER = tuple(D_GROUP * (2 * gp + half) + j
                     for gp in range(D_KV_HEADS // 2) for j in range(D_GROUP) for half in range(2))


V7X_VMEM_MIB = 64
VMEM_LIMIT_MIB = V7X_VMEM_MIB - 8


def _cparams(sem):
    return pltpu.CompilerParams(dimension_semantics=sem, vmem_limit_bytes=VMEM_LIMIT_MIB * 1024 * 1024)


def _dot(a, b):
    return jnp.dot(a, b, preferred_element_type=F32)


def _dot_nt(a, b):
    return lax.dot_general(a, b, (((1,), (1,)), ((), ())), preferred_element_type=F32)


def _rms(x, g):
    ms = jnp.mean(x * x, axis=-1, keepdims=True)
    return x * lax.rsqrt(ms + EPS) * g


def _ffn_body(x_ref, hg_ref, wg_ref, wu_ref, wd_ref, ng_ref, *rest, tail, h_given):
    rest = list(rest)
    o_ref = rest.pop(0)
    hn_ref = rest.pop(0) if tail == "next_norm" else None
    h_ref = hg_ref if h_given else rest.pop(0)
    acc_ref = rest.pop(0)
    j = pl.program_id(1)
    nj = pl.num_programs(1)

    @pl.when(j == 0)
    def _():
        if not h_given:
            h_ref[...] = _rms(x_ref[...], hg_ref[...]).astype(BF16)
        acc_ref[...] = jnp.zeros(acc_ref.shape, F32)

    h = h_ref[...]
    gate = _dot(h, wg_ref[...])
    up = _dot(h, wu_ref[...])
    act = (gate * (1.0 / (1.0 + jnp.exp(-gate))) * up).astype(BF16)
    acc_ref[...] += _dot(act, wd_ref[...])

    @pl.when(j == nj - 1)
    def _():
        y = x_ref[...] + 0.5 * acc_ref[...]
        if tail == "final_norm":
            y = _rms(y, ng_ref[...])
        o_ref[...] = y
        if tail == "next_norm":
            hn_ref[...] = _rms(y, ng_ref[...]).astype(BF16)


FFN_TF = 512


def _ffn(x, hg, wg, wu, wd, ng, *, tail, tm=512):
    T, D = x.shape
    FF = wd.shape[0]
    tm = min(tm, T)
    tf = min(FFN_TF, FF)
    assert T % tm == 0 and FF % tf == 0
    h_given = hg.shape[0] == T
    row_spec = pl.BlockSpec((tm, D), lambda i, j: (i, 0))
    scratch = [] if h_given else [pltpu.VMEM((tm, D), BF16)]
    if tail == "next_norm":
        out_shape = [jax.ShapeDtypeStruct((T, D), F32), jax.ShapeDtypeStruct((T, D), BF16)]
        out_specs = [row_spec, row_spec]
    else:
        out_shape = jax.ShapeDtypeStruct((T, D), F32)
        out_specs = row_spec
    return pl.pallas_call(
        functools.partial(_ffn_body, tail=tail, h_given=h_given),
        out_shape=out_shape,
        grid=(T // tm, FF // tf),
        in_specs=[
            row_spec,
            row_spec if h_given else pl.BlockSpec((1, D), lambda i, j: (0, 0)),
            pl.BlockSpec((D, tf), lambda i, j: (0, j)),
            pl.BlockSpec((D, tf), lambda i, j: (0, j)),
            pl.BlockSpec((tf, D), lambda i, j: (j, 0)),
            pl.BlockSpec((1, D), lambda i, j: (0, 0)),
        ],
        out_specs=out_specs,
        scratch_shapes=scratch + [pltpu.VMEM((tm, D), F32)],
        compiler_params=_cparams(("parallel", "arbitrary")),
        name="ffn",
    )(x, hg, wg, wu, wd, ng)


def _rot(x, c, sa, sb, shift):
    return x * c + pltpu.roll(x, shift, 1) * sa + pltpu.roll(x, LANE - shift, 1) * sb


def _proj_body(h_ref, w_ref, cs_ref, c_ref, sa_ref, sb_ref, o_ref, *, tr, shift, head_norm):
    tm, tn = o_ref.shape
    for rc in range(tm // tr):
        rows = slice(rc * tr, (rc + 1) * tr)
        acc = _dot(h_ref[rows, :], w_ref[...])
        c_t, sa_t, sb_t = c_ref[rows, :], sa_ref[rows, :], sb_ref[rows, :]
        for c in range(tn // LANE):
            sl = slice(c * LANE, (c + 1) * LANE)
            y = acc[:, sl]
            if head_norm:
                low = lax.broadcasted_iota(jnp.int32, y.shape, 1) < HEAD_DIM
                sq = y * y
                ss_low = jnp.sum(jnp.where(low, sq, 0.0), axis=-1, keepdims=True)
                ss_high = jnp.sum(jnp.where(low, 0.0, sq), axis=-1, keepdims=True)
                ms = jnp.where(low, ss_low, ss_high) * (1.0 / HEAD_DIM)
                y = y * lax.rsqrt(ms + EPS) * cs_ref[:, sl]
                y = _rot(y, c_t, sa_t, sb_t, shift)
            else:
                y = _rot(y, c_t, sa_t, sb_t, shift) * cs_ref[:, sl]
            o_ref[rows, sl] = y.astype(o_ref.dtype)


def _proj(h, w, cs, tabs, L, *, out_dtype, n_rot_tiles, shift, head_norm, tn, name, tm=2048, tr=256):
    T, D = h.shape
    W = w.shape[1]
    tm = min(tm, L)
    tr = min(tr, tm)
    assert T % tm == 0 and L % tm == 0 and W % tn == 0 and tm % tr == 0
    lt = L // tm
    tab_spec = pl.BlockSpec((None, tm, LANE), lambda i, j: (jnp.where(j < n_rot_tiles, 0, 1), i % lt, 0))
    return pl.pallas_call(
        functools.partial(_proj_body, tr=tr, shift=shift, head_norm=head_norm),
        out_shape=jax.ShapeDtypeStruct((T, W), out_dtype),
        grid=(T // tm, W // tn),
        in_specs=[
            pl.BlockSpec((tm, D), lambda i, j: (i, 0)),
            pl.BlockSpec((D, tn), lambda i, j: (0, j)),
            pl.BlockSpec((1, tn), lambda i, j: (0, j)),
            tab_spec, tab_spec, tab_spec,
        ],
        out_specs=pl.BlockSpec((tm, tn), lambda i, j: (i, j)),
        compiler_params=_cparams(("parallel", "parallel")),
        name=name,
    )(h, w, cs, *tabs)


def _flash_init(m_ref, acc_ref):
    m_ref[...] = jnp.full(m_ref.shape, NEG, F32)
    acc_ref[...] = jnp.zeros(acc_ref.shape, F32)


def _flash_step(qq_ref, k_ref, vs, m_ref, acc_ref, l_ref=None):
    s = _dot_nt(qq_ref[...], k_ref[...])
    chunks = [s[:, c * LANE:(c + 1) * LANE] for c in range(s.shape[1] // LANE)]
    mx = chunks[0]
    for sc in chunks[1:]:
        mx = jnp.maximum(mx, sc)
    m_prev = m_ref[...]
    m_new = jnp.maximum(m_prev, jnp.max(mx, axis=-1, keepdims=True))
    alpha = jnp.exp2(m_prev - m_new)
    ps = [jnp.exp2(sc - m_new) for sc in chunks]
    if l_ref is not None:
        lsum = ps[0]
        for pc in ps[1:]:
            lsum = lsum + pc
        l_ref[...] = alpha * l_ref[...] + lsum
    p = jnp.concatenate([pc.astype(BF16) for pc in ps], axis=1)
    share = p.shape[0] // len(vs)
    for t, v in enumerate(vs):
        rows = slice(t * share, (t + 1) * share)
        acc_ref[rows, :] = alpha[rows] * acc_ref[rows, :] + _dot(p[rows], v)
    m_ref[...] = m_new


def _flash_out(acc_ref, l_ref=None, ones_lane=None):
    acc = acc_ref[...]
    if l_ref is not None:
        den = jnp.sum(l_ref[...], axis=-1, keepdims=True)
    else:
        lane = lax.broadcasted_iota(jnp.int32, acc.shape, 1)
        den = jnp.sum(jnp.where(lane == ones_lane, acc, 0.0), axis=-1, keepdims=True)
    return acc * (1.0 / den)


def _diff_body(q_ref, k_ref, v_ref, lam_ref, gn_ref, o_ref, qq_ref, m_ref, l_ref, acc_ref, *, tq, lam_init):
    ki = pl.program_id(3)

    @pl.when(ki == 0)
    def _():
        q = q_ref[...]
        lane = lax.broadcasted_iota(jnp.int32, q.shape, 1)
        zero = jnp.zeros_like(q)
        qq_ref[0:tq, :] = jnp.where(lane < HEAD_DIM, q, zero)
        qq_ref[tq:2 * tq, :] = jnp.where(lane >= HEAD_DIM, q, zero)
        _flash_init(m_ref, acc_ref)
        l_ref[...] = jnp.zeros(l_ref.shape, F32)

    _flash_step(qq_ref, k_ref, [v_ref[...]], m_ref, acc_ref, l_ref)

    @pl.when(ki == pl.num_programs(3) - 1)
    def _():
        o = _flash_out(acc_ref, l_ref=l_ref)
        lv = lam_ref[...]
        lam = (jnp.exp(jnp.sum(lv[0:1] * lv[1:2], axis=-1, keepdims=True))
               - jnp.exp(jnp.sum(lv[2:3] * lv[3:4], axis=-1, keepdims=True)) + lam_init)
        a = o[0:tq] - lam * o[tq:2 * tq]
        o_ref[...] = (_rms(a, gn_ref[...]) * (1.0 - lam_init)).astype(BF16)


def _diff_attention(u3, lam_vecs, gn, lam_init, *, tq=1024, tk=2048):
    B, L, _ = u3.shape
    tq = min(tq, L)
    tk = min(tk, L)
    qb, kb, vb = UA_AQ // LANE, UA_AK // LANE, UA_AV // LANE
    return pl.pallas_call(
        functools.partial(_diff_body, tq=tq, lam_init=lam_init),
        out_shape=jax.ShapeDtypeStruct((B, L, A_HEADS * LANE), BF16),
        grid=(B, A_HEADS, L // tq, L // tk),
        in_specs=[
            pl.BlockSpec((None, tq, LANE), lambda b, h, qi, ki: (b, qi, qb + h)),
            pl.BlockSpec((None, tk, LANE), lambda b, h, qi, ki: (b, ki, kb + h)),
            pl.BlockSpec((None, tk, LANE), lambda b, h, qi, ki: (b, ki, vb + h)),
            pl.BlockSpec((4, HEAD_DIM), lambda b, h, qi, ki: (0, 0)),
            pl.BlockSpec((1, LANE), lambda b, h, qi, ki: (0, 0)),
        ],
        out_specs=pl.BlockSpec((None, tq, LANE), lambda b, h, qi, ki: (b, qi, h)),
        scratch_shapes=[
            pltpu.VMEM((2 * tq, LANE), BF16),
            pltpu.VMEM((2 * tq, LANE), F32),
            pltpu.VMEM((2 * tq, LANE), F32),
            pltpu.VMEM((2 * tq, LANE), F32),
        ],
        compiler_params=_cparams(("parallel", "parallel", "parallel", "arbitrary")),
        name="diff_attn",
    )(u3, u3, u3, lam_vecs, gn)


def _gqa_body(q_ref, k_ref, v0_ref, v1_ref, o_ref, qq_ref, m_ref, acc_ref, *, tq):
    ki = pl.program_id(3)

    @pl.when(ki == 0)
    def _():
        for half in range(2):
            for j in range(D_GROUP):
                q = q_ref[:, j * LANE:(j + 1) * LANE]
                lane = lax.broadcasted_iota(jnp.int32, q.shape, 1)
                own = (lane >= HEAD_DIM) if half == 1 else (lane < HEAD_DIM)
                r0 = (half * D_GROUP + j) * tq
                qq_ref[r0:r0 + tq, :] = jnp.where(own, q, jnp.zeros_like(q))
        _flash_init(m_ref, acc_ref)

    def with_ones(v_ref):
        v = v_ref[...]
        lane_v = lax.broadcasted_iota(jnp.int32, v.shape, 1)
        return jnp.where(lane_v == HEAD_DIM, jnp.ones_like(v), v)

    _flash_step(qq_ref, k_ref, [with_ones(v0_ref), with_ones(v1_ref)], m_ref, acc_ref)

    @pl.when(ki == pl.num_programs(3) - 1)
    def _():
        o = _flash_out(acc_ref, ones_lane=HEAD_DIM)
        low = lax.broadcasted_iota(jnp.int32, (tq, LANE), 1) < HEAD_DIM
        for j in range(D_GROUP):
            even = o[j * tq:(j + 1) * tq]
            odd = pltpu.roll(o[(D_GROUP + j) * tq:(D_GROUP + j + 1) * tq], HEAD_DIM, 1)
            o_ref[:, j * LANE:(j + 1) * LANE] = jnp.where(low, even, odd).astype(BF16)


def _gqa_attention(ud3, ua3, *, tq=512, tk=2048):
    B, L, _ = ud3.shape
    tq = min(tq, L)
    tk = min(tk, L)
    gw = D_GROUP * LANE
    R = 2 * D_GROUP * tq
    qb, kb, vb = UD_Q // gw, UD_K // LANE, UA_DV // LANE
    return pl.pallas_call(
        functools.partial(_gqa_body, tq=tq),
        out_shape=jax.ShapeDtypeStruct((B, L, D_Q_HEADS * HEAD_DIM), BF16),
        grid=(B, D_KV_HEADS // 2, L // tq, L // tk),
        in_specs=[
            pl.BlockSpec((None, tq, gw), lambda b, gp, qi, ki: (b, qi, qb + gp)),
            pl.BlockSpec((None, tk, LANE), lambda b, gp, qi, ki: (b, ki, kb + gp)),
            pl.BlockSpec((None, tk, LANE), lambda b, gp, qi, ki: (b, ki, vb + 2 * gp)),
            pl.BlockSpec((None, tk, LANE), lambda b, gp, qi, ki: (b, ki, vb + 2 * gp + 1)),
        ],
        out_specs=pl.BlockSpec((None, tq, gw), lambda b, gp, qi, ki: (b, qi, gp)),
        scratch_shapes=[
            pltpu.VMEM((R, LANE), BF16),
            pltpu.VMEM((R, LANE), F32),
            pltpu.VMEM((R, LANE), F32),
        ],
        compiler_params=_cparams(("parallel", "parallel", "parallel", "arbitrary")),
        name="gqa_attn",
    )(ud3, ud3, ua3, ua3)


def _dil_body(q_ref, k_ref, kp_ref, kn_ref, v_ref, vp_ref, vn_ref, o_ref, kbuf, vbuf, m_ref, l_ref, acc_ref,
              *, TT, HALO, L):
    i = pl.program_id(2)
    H = B_SIDE
    kbuf[0:HALO, :] = kp_ref[...]
    kbuf[HALO:HALO + TT, :] = k_ref[...]
    kbuf[HALO + TT:HALO + TT + HALO, :] = kn_ref[...]
    vbuf[0:HALO, :] = vp_ref[...]
    vbuf[HALO:HALO + TT, :] = v_ref[...]
    vbuf[HALO + TT:HALO + TT + HALO, :] = vn_ref[...]

    tq = LANE
    tw = tq + 2 * H
    head0 = lax.broadcasted_iota(jnp.int32, (tq, LANE), 1) < HEAD_DIM
    row = lax.broadcasted_iota(jnp.int32, (2 * tq, tw), 0)
    col = lax.broadcasted_iota(jnp.int32, (2 * tq, tw), 1)
    band_bias = jnp.where(jnp.abs(col - H - (row % tq)) <= H, 0.0, NEG)
    col1 = lax.broadcasted_iota(jnp.int32, (1, tw), 1)

    def per_head(x):
        return jnp.where(head0, jnp.broadcast_to(x[0:tq], (tq, LANE)), jnp.broadcast_to(x[tq:2 * tq], (tq, LANE)))

    for p, (_, d) in enumerate(reversed(B_PATTERNS)):
        n_sub = L // d

        def tile(idx, carry, d=d, p=p, n_sub=n_sub):
            r = idx % d
            j = idx // d
            q_start = r + d * (j * tq)
            k_start = HALO + r + d * (j * tq - H)
            if d == 1:
                q_rows, k_rows = pl.ds(q_start, tq), pl.ds(k_start, tw)
            else:
                q_rows, k_rows = pl.ds(q_start, tq, stride=d), pl.ds(k_start, tw, stride=d)
            q = q_ref[q_rows, :].astype(BF16)
            zero = jnp.zeros_like(q)
            qq = jnp.concatenate([jnp.where(head0, q, zero), jnp.where(head0, zero, q)], axis=0)
            kw = kbuf[k_rows, :].astype(BF16)
            vw = vbuf[k_rows, :].astype(BF16)
            n0 = (i * TT) // d + j * tq - H
            in_seq = jnp.where((col1 >= -n0) & (col1 < n_sub - n0), 0.0, NEG)
            s = _dot_nt(qq, kw) + band_bias + in_seq
            m = jnp.max(s, axis=-1, keepdims=True)
            e = jnp.exp2(s - m)
            m_t = per_head(m)
            l_t = per_head(jnp.sum(e, axis=-1, keepdims=True))
            pv = _dot(e.astype(BF16), vw)
            pv_t = jnp.where(head0, pv[0:tq], pv[tq:2 * tq])
            if p == 0:
                m_ref[q_rows, :] = m_t
                l_ref[q_rows, :] = l_t
                acc_ref[q_rows, :] = pv_t
            else:
                m_old = m_ref[q_rows, :]
                m_new = jnp.maximum(m_old, m_t)
                a_old = jnp.exp2(m_old - m_new)
                a_t = jnp.exp2(m_t - m_new)
                m_ref[q_rows, :] = m_new
                l_ref[q_rows, :] = a_old * l_ref[q_rows, :] + a_t * l_t
                acc_ref[q_rows, :] = a_old * acc_ref[q_rows, :] + a_t * pv_t
            return carry

        lax.fori_loop(0, TT // tq, tile, 0, unroll=True)

    o_ref[...] = (acc_ref[...] * (1.0 / l_ref[...])).astype(BF16)


B_TILE = 2048
B_HALO = 1024


def _dilated_attention(ub3):
    B, L, _ = ub3.shape
    TT, HALO = B_TILE, B_HALO
    assert L % TT == 0 and TT % HALO == 0 and HALO >= B_SIDE * max(d for _, d in B_PATTERNS)
    pairs = B_HEADS // 2
    qb, kb, vb = UB_Q // LANE, UB_K // LANE, UB_V // LANE
    th = TT // HALO
    nh = L // HALO

    def main(base):
        return pl.BlockSpec((None, TT, LANE), lambda b, hp, i: (b, i, base + hp))

    def prev(base):
        return pl.BlockSpec((None, HALO, LANE), lambda b, hp, i: (b, jnp.maximum(i * th - 1, 0), base + hp))

    def nxt(base):
        return pl.BlockSpec((None, HALO, LANE), lambda b, hp, i: (b, jnp.minimum((i + 1) * th, nh - 1), base + hp))

    return pl.pallas_call(
        functools.partial(_dil_body, TT=TT, HALO=HALO, L=L),
        out_shape=jax.ShapeDtypeStruct((B, L, pairs * LANE), BF16),
        grid=(B, pairs, L // TT),
        in_specs=[main(qb), main(kb), prev(kb), nxt(kb), main(vb), prev(vb), nxt(vb)],
        out_specs=pl.BlockSpec((None, TT, LANE), lambda b, hp, i: (b, i, hp)),
        scratch_shapes=[pltpu.VMEM((TT + 2 * HALO, LANE), F32), pltpu.VMEM((TT + 2 * HALO, LANE), F32),
                        pltpu.VMEM((TT, LANE), F32), pltpu.VMEM((TT, LANE), F32), pltpu.VMEM((TT, LANE), F32)],
        compiler_params=_cparams(("parallel", "parallel", "parallel")),
        name="dilated_attn",
    )(ub3, ub3, ub3, ub3, ub3, ub3, ub3)


def _gla_body(*refs, R, backward):
    if backward:
        (q_ref, k_ref, v_ref, low_ref, wup_ref, bias_ref, fwd_ref, cg_ref, gn_ref, o_ref, s_ref) = refs
    else:
        (q_ref, k_ref, v_ref, low_ref, wup_ref, bias_ref, o_ref, s_ref) = refs
    C = C_CHUNK
    n = R // C

    @pl.when(pl.program_id(2) == 0)
    def _():
        s_ref[...] = jnp.zeros(s_ref.shape, F32)

    x = _dot(low_ref[...], wup_ref[...]) + bias_ref[...]
    gl = (jnp.minimum(x, 0.0) - jnp.log(1.0 + jnp.exp(-jnp.abs(x)))) * (1.0 / C_GATE_NORM)

    lane = lax.broadcasted_iota(jnp.int32, (C, LANE), 1)
    qrow = lax.broadcasted_iota(jnp.int32, (C, LANE), 0)
    key = lane % C
    tri_wide = (key >= qrow) if backward else (key <= qrow)
    head0 = lane < HEAD_DIM
    lane2 = lax.broadcasted_iota(jnp.int32, (C, 2 * LANE), 1)
    vhead0 = lane2 < LANE
    rr = lax.broadcasted_iota(jnp.int32, (LANE, 2 * LANE), 0)
    cc = lax.broadcasted_iota(jnp.int32, (LANE, 2 * LANE), 1)
    blockdiag = (rr < HEAD_DIM) == (cc < LANE)
    eye = (lax.broadcasted_iota(jnp.int32, (LANE, LANE), 0) == lax.broadcasted_iota(jnp.int32, (LANE, LANE), 1))
    zpad = jnp.zeros((C, LANE), F32)
    zpad_v = jnp.zeros((C, 2 * LANE), BF16)

    chunk_rows = [slice(c * C, (c + 1) * C) for c in range(n)]
    pos = lax.broadcasted_iota(jnp.int32, (R, LANE), 0) % C
    cum_all = gl
    for sft in (1, 2, 4, 8, 16, 32):
        if backward:
            cum_all = cum_all + jnp.where(pos < C - sft, pltpu.roll(cum_all, R - sft, 0), 0.0)
        else:
            cum_all = cum_all + jnp.where(pos >= sft, pltpu.roll(cum_all, sft, 0), 0.0)
    cums = [cum_all[rows] for rows in chunk_rows]
    lasts = [(cum[0:1] if backward else cum[C - 1:C]) for cum in cums]
    qds, atts, intra, upds, dec_cols = [], [], [], [], []
    for rows, cum, last in zip(chunk_rows, cums, lasts):
        kc = k_ref[rows, :].astype(F32)
        qds.append((q_ref[rows, :].astype(F32) * jnp.exp(cum)).astype(BF16))
        kinv = kc * jnp.exp(-cum)
        zero = jnp.zeros_like(kinv)
        kbd = jnp.concatenate([jnp.where(head0, kinv, zero), jnp.where(head0, zero, kinv)], axis=0).astype(BF16)
        atts.append(jnp.where(tri_wide, _dot_nt(qds[-1], kbd), 0.0).astype(BF16))
        kdec = kc * jnp.exp(last - cum)
        kdec_t = jnp.concatenate([kdec, zpad], axis=0).T.astype(BF16)
        vc = v_ref[rows, :]
        upd = _dot(kdec_t, jnp.concatenate([vc, zpad_v], axis=0))
        upds.append(jnp.where(blockdiag, upd, 0.0))
        dec_row = jnp.broadcast_to(jnp.exp(last), (LANE, LANE))
        dec_cols.append(jnp.sum(jnp.where(eye, dec_row, 0.0), axis=-1, keepdims=True))
    for rows, att in zip(chunk_rows, atts):
        vc = v_ref[rows, :]
        zv = jnp.zeros_like(vc)
        vbd = jnp.concatenate([jnp.where(vhead0, vc, zv), jnp.where(vhead0, zv, vc)], axis=0)
        intra.append(_dot(att, vbd))

    state = s_ref[...]
    entering = [None] * n
    for c in (range(n - 1, -1, -1) if backward else range(n)):
        entering[c] = state.astype(BF16)
        state = dec_cols[c] * state + upds[c]
    s_ref[...] = state

    if backward:
        gate = cg_ref[...].astype(F32)
        gate = gate * (1.0 / (1.0 + jnp.exp(-gate)))
    for c, rows in enumerate(chunk_rows):
        o = intra[c] + _dot(qds[c], entering[c])
        if backward:
            tot = o + fwd_ref[rows, :]
            for h in range(2):
                hs = slice(h * LANE, (h + 1) * LANE)
                o_ref[rows, hs] = (_rms(tot[:, hs], gn_ref[...]) * gate[rows, hs]).astype(BF16)
        else:
            o_ref[rows, :] = o


def _gla(u3, wup_f, bias_f, wup_b, bias_b, gn, *, R=2048):
    B, L, _ = u3.shape
    R = min(R, L)
    nb = L // R
    pairs = C_HEADS // 2
    W = C_HEADS * LANE
    qb, kb = UA_CQ // LANE, UA_CK // LANE
    vb, gb, lb = UA_CV // (2 * LANE), UA_CG // (2 * LANE), UA_CLOW // (2 * LANE)

    def specs(rev):
        def ri(i):
            return nb - 1 - i if rev else i
        return dict(
            q=pl.BlockSpec((None, R, LANE), lambda b, p, i: (b, ri(i), qb + p)),
            k=pl.BlockSpec((None, R, LANE), lambda b, p, i: (b, ri(i), kb + p)),
            v=pl.BlockSpec((None, R, 2 * LANE), lambda b, p, i: (b, ri(i), vb + p)),
            low=pl.BlockSpec((None, R, 2 * LANE), lambda b, p, i: (b, ri(i), lb)),
            wup=pl.BlockSpec((2 * LANE, LANE), lambda b, p, i: (0, p)),
            bias=pl.BlockSpec((1, LANE), lambda b, p, i: (0, p)),
            out=pl.BlockSpec((None, R, 2 * LANE), lambda b, p, i: (b, ri(i), p)),
            cg=pl.BlockSpec((None, R, 2 * LANE), lambda b, p, i: (b, ri(i), gb + p)),
            gn=pl.BlockSpec((1, LANE), lambda b, p, i: (0, 0)),
        )

    sem = _cparams(("parallel", "parallel", "arbitrary"))
    sf = specs(False)
    fwd = pl.pallas_call(
        functools.partial(_gla_body, R=R, backward=False),
        out_shape=jax.ShapeDtypeStruct((B, L, W), F32),
        grid=(B, pairs, nb),
        in_specs=[sf["q"], sf["k"], sf["v"], sf["low"], sf["wup"], sf["bias"]],
        out_specs=sf["out"],
        scratch_shapes=[pltpu.VMEM((LANE, 2 * LANE), F32)],
        compiler_params=sem,
        name="gla_fwd",
    )(u3, u3, u3, u3, wup_f, bias_f)
    sb = specs(True)
    return pl.pallas_call(
        functools.partial(_gla_body, R=R, backward=True),
        out_shape=jax.ShapeDtypeStruct((B, L, W), BF16),
        grid=(B, pairs, nb),
        in_specs=[sb["q"], sb["k"], sb["v"], sb["low"], sb["wup"], sb["bias"], sb["out"], sb["cg"], sb["gn"]],
        out_specs=sb["out"],
        scratch_shapes=[pltpu.VMEM((LANE, 2 * LANE), F32)],
        compiler_params=sem,
        name="gla_bwd",
    )(u3, u3, u3, u3, wup_b, bias_b, fwd, u3, gn)


def _outproj_body(x_ref, a_ref, b_ref, c_ref, d_ref, wa_ref, wb_ref, wc_ref, wd_ref, ng_ref, o_ref, hn_ref):
    acc = _dot(a_ref[...], wa_ref[...]) + _dot(b_ref[...], wb_ref[...])
    acc = acc + _dot(c_ref[...], wc_ref[...]) + _dot(d_ref[...], wd_ref[...])
    y = x_ref[...] + acc
    o_ref[...] = y
    hn_ref[...] = _rms(y, ng_ref[...]).astype(BF16)


def _outproj(x, oa, ob, oc, od, wa, wb, wc, wd, ng, *, tm=512):
    T, D = x.shape
    tm = min(tm, T)
    row_spec = pl.BlockSpec((tm, D), lambda i: (i, 0))

    def act(w):
        return pl.BlockSpec((tm, w), lambda i: (i, 0))

    def wt(w):
        return pl.BlockSpec((w, D), lambda i: (0, 0), pipeline_mode=pl.Buffered(1))

    widths = [oa.shape[1], ob.shape[1], oc.shape[1], od.shape[1]]
    return pl.pallas_call(
        _outproj_body,
        out_shape=[jax.ShapeDtypeStruct((T, D), F32), jax.ShapeDtypeStruct((T, D), BF16)],
        grid=(T // tm,),
        in_specs=[row_spec] + [act(w) for w in widths] + [wt(w) for w in widths]
        + [pl.BlockSpec((1, D), lambda i: (0, 0))],
        out_specs=[row_spec, row_spec],
        compiler_params=_cparams(("parallel",)),
        name="outproj",
    )(x, oa, ob, oc, od, wa, wb, wc, wd, ng)


def _pad_heads(w, n):
    k = w.shape[0]
    return jnp.pad(w.reshape(k, n, HEAD_DIM), ((0, 0), (0, 0), (0, LANE - HEAD_DIM))).reshape(k, n * LANE)


def _prep_w_in(w):
    sizes = (768, 768, 768, 768, 768, 768, 384, 384, 768, 768, 32, 768, 256, 256)
    offs = [0]
    for s in sizes:
        offs.append(offs[-1] + s)
    (a_q, a_k, a_v, b_q, b_k, b_v, c_q, c_k, c_v, c_g, c_low, d_q, d_k, d_v) = [
        w[:, offs[i]:offs[i + 1]] for i in range(len(sizes))]
    low = jnp.pad(c_low, ((0, 0), (0, 2 * LANE - 2 * C_RANK)))
    wb = jnp.concatenate([b_q, b_k, b_v], axis=1).astype(BF16)
    wa = jnp.concatenate([a_q, a_k, a_v, _pad_heads(d_v, D_KV_HEADS),
                          c_q * QK_SCALE, c_k, c_v, c_g, low], axis=1).astype(BF16)
    order = jnp.asarray(D_HEAD_ORDER)
    d_q = jnp.take(d_q.reshape(-1, D_Q_HEADS, HEAD_DIM), order, axis=1).reshape(-1, D_Q_HEADS * HEAD_DIM)
    wd = jnp.concatenate([d_q, d_k], axis=1).astype(BF16)
    assert wb.shape[1] == UB_W and wa.shape[1] == UA_W and wd.shape[1] == UD_W
    return wb, wa, wd


def _prep_w_out(w):
    wa = w[0:768]
    wb = w[768:1536]
    wc = w[1536:2304]
    wd = w[2304:3072]
    n = w.shape[1]
    wd = jnp.take(wd.reshape(D_Q_HEADS, HEAD_DIM, n), jnp.asarray(D_HEAD_ORDER), axis=0)
    wd = wd.reshape(D_Q_HEADS * HEAD_DIM, n)
    return [t.astype(BF16) for t in (wa, wb, wc, wd)]


def _q_col_scale(width, q_lo, q_hi):
    col = jnp.arange(width)
    return jnp.where((col >= q_lo) & (col < q_hi), QK_SCALE * LOG2E, 1.0).astype(F32).reshape(1, -1)


def _rope_tables(L):
    t = jnp.arange(L, dtype=F32)
    lane = jnp.arange(LANE)
    l64 = lane % HEAD_DIM
    half = ROPE_DIMS // 2
    inv = ROPE_THETA ** (-jnp.arange(0, ROPE_DIMS, 2, dtype=F32) / ROPE_DIMS)
    ang = t[:, None] * inv[None, :]
    ang_l = ang[:, l64 % half]
    in_rot = (l64 < ROPE_DIMS)[None, :]
    c8 = jnp.where(in_rot, jnp.cos(ang_l), 1.0)
    sa8 = jnp.where(((l64 >= half) & (l64 < ROPE_DIMS))[None, :], jnp.sin(ang_l), 0.0)
    sb8 = jnp.where((l64 < half)[None, :], -jnp.sin(ang_l), 0.0)

    q = HEAD_DIM // 4
    inv2 = AXIAL_THETA ** (-jnp.arange(0, HEAD_DIM // 2, 2, dtype=F32) / (HEAD_DIM // 2))
    rows = L // GRID_W
    row_pos = jnp.repeat(jnp.arange(rows, dtype=F32), GRID_W)
    col_pos = jnp.tile(jnp.arange(GRID_W, dtype=F32), rows)
    ang_r = row_pos[:, None] * inv2[None, :]
    ang_c = col_pos[:, None] * inv2[None, :]
    ang_x = jnp.where((l64 < 2 * q)[None, :], ang_r[:, l64 % q], ang_c[:, l64 % q])
    cx = jnp.cos(ang_x)
    upper = ((l64 % (2 * q)) >= q)[None, :]
    sax = jnp.where(upper, jnp.sin(ang_x), 0.0)
    sbx = jnp.where(upper, 0.0, -jnp.sin(ang_x))
    one, zero = jnp.ones_like(c8), jnp.zeros_like(c8)
    rope8 = [jnp.stack([c8, one]), jnp.stack([sa8, zero]), jnp.stack([sb8, zero])]
    axial = [jnp.stack([cx, one]), jnp.stack([sax, zero]), jnp.stack([sbx, zero])]
    return rope8, axial


def _prep_layer(l, ffn1_norm, ffn1_w_gate, ffn1_w_up, ffn1_w_down, mix_norm, w_in, w_out,
                diff_lambda_q1, diff_lambda_k1, diff_lambda_q2, diff_lambda_k2, diff_out_norm,
                gla_gate_up_f, gla_gate_bias_f, gla_gate_up_b, gla_gate_bias_b, gla_out_norm,
                gqa_q_norm, gqa_k_norm, ffn2_norm, ffn2_w_gate, ffn2_w_up, ffn2_w_down):
    def row(v):
        return v.astype(F32).reshape(1, -1)

    gq = jnp.tile(gqa_q_norm[l].astype(F32) * (QK_SCALE * LOG2E), D_Q_HEADS)
    gk = jnp.tile(gqa_k_norm[l].astype(F32), D_KV_HEADS)
    wup_f = jnp.zeros((2 * LANE, C_HEADS * HEAD_DIM), F32).at[0:C_RANK].set(gla_gate_up_f[l])
    wup_b = jnp.zeros((2 * LANE, C_HEADS * HEAD_DIM), F32).at[C_RANK:2 * C_RANK].set(gla_gate_up_b[l])
    return dict(
        n1=row(ffn1_norm[l]), wg1=ffn1_w_gate[l].astype(BF16), wu1=ffn1_w_up[l].astype(BF16),
        wd1=ffn1_w_down[l].astype(BF16),
        nmix=row(mix_norm[l]), w_in=_prep_w_in(w_in[l]), w_out=_prep_w_out(w_out[l]),
        gd=jnp.concatenate([gq, gk]).reshape(1, -1),
        lam=jnp.stack([diff_lambda_q1[l], diff_lambda_k1[l], diff_lambda_q2[l], diff_lambda_k2[l]]).astype(F32),
        lam_init=0.8 - 0.6 * math.exp(-0.3 * l),
        gdiff=row(diff_out_norm[l]),
        wup_f=wup_f.astype(BF16), bias_f=row(gla_gate_bias_f[l]),
        wup_b=wup_b.astype(BF16), bias_b=row(gla_gate_bias_b[l]),
        ggla=row(gla_out_norm[l]),
        n2=row(ffn2_norm[l]), wg2=ffn2_w_gate[l].astype(BF16), wu2=ffn2_w_up[l].astype(BF16),
        wd2=ffn2_w_down[l].astype(BF16),
    )


def _in_projections(h, p, tabs, L):
    rope8, axial = tabs
    w_b, w_a, w_d = p["w_in"]
    half = ROPE_DIMS // 2
    ub = _proj(h, w_b, _q_col_scale(UB_W, UB_Q, UB_K), rope8, L, out_dtype=F32,
               n_rot_tiles=UB_ROPE_TILES, shift=half, head_norm=False, tn=PROJ_TN, name="inproj_b")
    ua = _proj(h, w_a, _q_col_scale(UA_W, UA_AQ, UA_AK), rope8, L, out_dtype=BF16,
               n_rot_tiles=UA_ROPE_TILES, shift=half, head_norm=False, tn=PROJ_TN, name="inproj_a")
    ud = _proj(h, w_d, p["gd"], axial, L, out_dtype=BF16, n_rot_tiles=UD_W // 1024,
               shift=HEAD_DIM // 4, head_norm=True, tn=1024, tm=1024, name="inproj_d")
    return ub, ua, ud


def _trunk(x, layers, final_g):
    B, L, D = x.shape
    xt = x.reshape(B * L, D)
    tabs = _rope_tables(L)
    h_ffn1 = None
    for l, p in enumerate(layers):
        xt, h = _ffn(xt, p["n1"] if h_ffn1 is None else h_ffn1, p["wg1"], p["wu1"], p["wd1"], p["nmix"],
                     tail="next_norm")
        ub, ua, ud = _in_projections(h, p, tabs, L)
        ua3 = ua.reshape(B, L, UA_W)
        oa = _diff_attention(ua3, p["lam"], p["gdiff"], p["lam_init"]).reshape(B * L, -1)
        ob = _dilated_attention(ub.reshape(B, L, UB_W)).reshape(B * L, -1)
        oc = _gla(ua3, p["wup_f"], p["bias_f"], p["wup_b"], p["bias_b"], p["ggla"]).reshape(B * L, -1)
        od = _gqa_attention(ud.reshape(B, L, UD_W), ua3).reshape(B * L, -1)
        xt, h2 = _outproj(xt, oa, ob, oc, od, *p["w_out"], p["n2"])
        if l == len(layers) - 1:
            xt = _ffn(xt, h2, p["wg2"], p["wu2"], p["wd2"], final_g, tail="final_norm")
        else:
            xt, h_ffn1 = _ffn(xt, h2, p["wg2"], p["wu2"], p["wd2"], layers[l + 1]["n1"], tail="next_norm")
    return xt.reshape(B, L, D)


def kernel(x_prompt, x_sample, ffn1_norm, ffn1_w_gate, ffn1_w_up, ffn1_w_down, mix_norm, w_in, w_out, diff_lambda_q1, diff_lambda_k1, diff_lambda_q2, diff_lambda_k2, diff_out_norm, gla_gate_up_f, gla_gate_bias_f, gla_gate_up_b, gla_gate_bias_b, gla_out_norm, gqa_q_norm, gqa_k_norm, ffn2_norm, ffn2_w_gate, ffn2_w_up, ffn2_w_down, final_norm):
    depth = w_in.shape[0]
    layers = [_prep_layer(l, ffn1_norm, ffn1_w_gate, ffn1_w_up, ffn1_w_down, mix_norm, w_in, w_out,
                          diff_lambda_q1, diff_lambda_k1, diff_lambda_q2, diff_lambda_k2, diff_out_norm,
                          gla_gate_up_f, gla_gate_bias_f, gla_gate_up_b, gla_gate_bias_b, gla_out_norm,
                          gqa_q_norm, gqa_k_norm, ffn2_norm, ffn2_w_gate, ffn2_w_up, ffn2_w_down)
              for l in range(depth)]
    final_g = final_norm.astype(F32).reshape(1, -1)
    return (_trunk(x_prompt, layers, final_g), _trunk(x_sample, layers, final_g))
```

```python
import functools
import math

import jax
import jax.numpy as jnp
from jax import lax
from jax.experimental import pallas as pl
from jax.experimental.pallas import tpu as pltpu

F32 = jnp.float32
BF16 = jnp.bfloat16

HEAD_DIM = 64
EPS = 1e-6
ROPE_THETA = 500000.0
ROPE_DIMS = HEAD_DIM // 4
AXIAL_THETA = 10000.0
GRID_W = 64
A_HEADS = 6
B_HEADS = 12
B_PATTERNS = ((128, 1), (512, 4), (2048, 16))
B_SIDE = 64
C_HEADS = 6
C_RANK = 16
C_CHUNK = 64
C_GATE_NORM = 16.0
D_Q_HEADS = 12
D_KV_HEADS = 4
D_GROUP = D_Q_HEADS // D_KV_HEADS
QK_SCALE = HEAD_DIM ** -0.5
LOG2E = math.log2(math.e)

LANE = 128
NEG = -1e30

PROJ_TN = 768
UB_Q = 0
UB_K = 768
UB_V = 1536
UB_W = 2304
UB_ROPE_TILES = 2
UA_AQ = 0
UA_AK = 768
UA_AV = 1536
UA_DV = 2304
UA_CQ = 2816
UA_CK = 3200
UA_CV = 3584
UA_CG = 4352
UA_CLOW = 5120
UA_W = 5376
UA_ROPE_TILES = 2
UD_Q = 0
UD_K = 768
UD_W = 1024
D_HEAD_ORDER = tuple(D_GROUP * (2 * gp + half) + j
                     for gp in range(D_KV_HEADS // 2) for j in range(D_GROUP) for half in range(2))


V7X_VMEM_MIB = 64
VMEM_LIMIT_MIB = V7X_VMEM_MIB - 8


def _cparams(sem):
    return pltpu.CompilerParams(dimension_semantics=sem, vmem_limit_bytes=VMEM_LIMIT_MIB * 1024 * 1024)


def _dot(a, b):
    return jnp.dot(a, b, preferred_element_type=F32)


def _dot_nt(a, b):
    return lax.dot_general(a, b, (((1,), (1,)), ((), ())), preferred_element_type=F32)


def _rms(x, g):
    ms = jnp.mean(x * x, axis=-1, keepdims=True)
    return x * lax.rsqrt(ms + EPS) * g


def _ffn_body(x_ref, hg_ref, wg_ref, wu_ref, wd_ref, ng_ref, *rest, tail, h_given):
    rest = list(rest)
    o_ref = rest.pop(0)
    hn_ref = rest.pop(0) if tail == "next_norm" else None
    h_ref = hg_ref if h_given else rest.pop(0)
    acc_ref = rest.pop(0)
    j = pl.program_id(1)
    nj = pl.num_programs(1)

    @pl.when(j == 0)
    def _():
        if not h_given:
            h_ref[...] = _rms(x_ref[...], hg_ref[...]).astype(BF16)
        acc_ref[...] = jnp.zeros(acc_ref.shape, F32)

    h = h_ref[...]
    gate = _dot(h, wg_ref[...])
    up = _dot(h, wu_ref[...])
    act = (gate * (1.0 / (1.0 + jnp.exp(-gate))) * up).astype(BF16)
    acc_ref[...] += _dot(act, wd_ref[...])

    @pl.when(j == nj - 1)
    def _():
        y = x_ref[...] + 0.5 * acc_ref[...]
        if tail == "final_norm":
            y = _rms(y, ng_ref[...])
        o_ref[...] = y
        if tail == "next_norm":
            hn_ref[...] = _rms(y, ng_ref[...]).astype(BF16)


FFN_TF = 512


def _ffn(x, hg, wg, wu, wd, ng, *, tail, tm=512):
    T, D = x.shape
    FF = wd.shape[0]
    tm = min(tm, T)
    tf = min(FFN_TF, FF)
    assert T % tm == 0 and FF % tf == 0
    h_given = hg.shape[0] == T
    row_spec = pl.BlockSpec((tm, D), lambda i, j: (i, 0))
    scratch = [] if h_given else [pltpu.VMEM((tm, D), BF16)]
    if tail == "next_norm":
        out_shape = [jax.ShapeDtypeStruct((T, D), F32), jax.ShapeDtypeStruct((T, D), BF16)]
        out_specs = [row_spec, row_spec]
    else:
        out_shape = jax.ShapeDtypeStruct((T, D), F32)
        out_specs = row_spec
    return pl.pallas_call(
        functools.partial(_ffn_body, tail=tail, h_given=h_given),
        out_shape=out_shape,
        grid=(T // tm, FF // tf),
        in_specs=[
            row_spec,
            row_spec if h_given else pl.BlockSpec((1, D), lambda i, j: (0, 0)),
            pl.BlockSpec((D, tf), lambda i, j: (0, j)),
            pl.BlockSpec((D, tf), lambda i, j: (0, j)),
            pl.BlockSpec((tf, D), lambda i, j: (j, 0)),
            pl.BlockSpec((1, D), lambda i, j: (0, 0)),
        ],
        out_specs=out_specs,
        scratch_shapes=scratch + [pltpu.VMEM((tm, D), F32)],
        compiler_params=_cparams(("parallel", "arbitrary")),
        name="ffn",
    )(x, hg, wg, wu, wd, ng)


def _rot(x, c, sa, sb, shift):
    return x * c + pltpu.roll(x, shift, 1) * sa + pltpu.roll(x, LANE - shift, 1) * sb


def _proj_body(h_ref, w_ref, cs_ref, c_ref, sa_ref, sb_ref, o_ref, *, tr, shift, head_norm):
    tm, tn = o_ref.shape
    for rc in range(tm // tr):
        rows = slice(rc * tr, (rc + 1) * tr)
        acc = _dot(h_ref[rows, :], w_ref[...])
        c_t, sa_t, sb_t = c_ref[rows, :], sa_ref[rows, :], sb_ref[rows, :]
        for c in range(tn // LANE):
            sl = slice(c * LANE, (c + 1) * LANE)
            y = acc[:, sl]
            if head_norm:
                low = lax.broadcasted_iota(jnp.int32, y.shape, 1) < HEAD_DIM
                sq = y * y
                ss_low = jnp.sum(jnp.where(low, sq, 0.0), axis=-1, keepdims=True)
                ss_high = jnp.sum(jnp.where(low, 0.0, sq), axis=-1, keepdims=True)
                ms = jnp.where(low, ss_low, ss_high) * (1.0 / HEAD_DIM)
                y = y * lax.rsqrt(ms + EPS) * cs_ref[:, sl]
                y = _rot(y, c_t, sa_t, sb_t, shift)
            else:
                y = _rot(y, c_t, sa_t, sb_t, shift) * cs_ref[:, sl]
            o_ref[rows, sl] = y.astype(o_ref.dtype)


def _proj(h, w, cs, tabs, L, *, out_dtype, n_rot_tiles, shift, head_norm, tn, name, tm=2048, tr=256):
    T, D = h.shape
    W = w.shape[1]
    tm = min(tm, L)
    tr = min(tr, tm)
    assert T % tm == 0 and L % tm == 0 and W % tn == 0 and tm % tr == 0
    lt = L // tm
    tab_spec = pl.BlockSpec((None, tm, LANE), lambda i, j: (jnp.where(j < n_rot_tiles, 0, 1), i % lt, 0))
    return pl.pallas_call(
        functools.partial(_proj_body, tr=tr, shift=shift, head_norm=head_norm),
        out_shape=jax.ShapeDtypeStruct((T, W), out_dtype),
        grid=(T // tm, W // tn),
        in_specs=[
            pl.BlockSpec((tm, D), lambda i, j: (i, 0)),
            pl.BlockSpec((D, tn), lambda i, j: (0, j)),
            pl.BlockSpec((1, tn), lambda i, j: (0, j)),
            tab_spec, tab_spec, tab_spec,
        ],
        out_specs=pl.BlockSpec((tm, tn), lambda i, j: (i, j)),
        compiler_params=_cparams(("parallel", "parallel")),
        name=name,
    )(h, w, cs, *tabs)


def _flash_init(m_ref, acc_ref):
    m_ref[...] = jnp.full(m_ref.shape, NEG, F32)
    acc_ref[...] = jnp.zeros(acc_ref.shape, F32)


def _flash_step_grouped(qq_ref, k_ref, v, m_ref, acc_ref, l_ref, groups):
    gr = qq_ref.shape[0] // groups
    row_groups = [slice(g * gr, (g + 1) * gr) for g in range(groups)]
    scores = [_dot_nt(qq_ref[rows, :], k_ref[...]) for rows in row_groups]
    for rows, s in zip(row_groups, scores):
        chunks = [s[:, c * LANE:(c + 1) * LANE] for c in range(s.shape[1] // LANE)]
        mx = chunks[0]
        for sc in chunks[1:]:
            mx = jnp.maximum(mx, sc)
        m_prev = m_ref[rows, :]
        m_new = jnp.maximum(m_prev, jnp.max(mx, axis=-1, keepdims=True))
        alpha = jnp.exp2(m_prev - m_new)
        ps = [jnp.exp2(sc - m_new) for sc in chunks]
        lsum = ps[0]
        for pc in ps[1:]:
            lsum = lsum + pc
        l_ref[rows, :] = alpha * l_ref[rows, :] + lsum
        p = jnp.concatenate([pc.astype(BF16) for pc in ps], axis=1)
        acc_ref[rows, :] = alpha * acc_ref[rows, :] + _dot(p, v)
        m_ref[rows, :] = m_new


def _flash_step(qq_ref, k_ref, vs, m_ref, acc_ref, l_ref=None):
    s = _dot_nt(qq_ref[...], k_ref[...])
    chunks = [s[:, c * LANE:(c + 1) * LANE] for c in range(s.shape[1] // LANE)]
    mx = chunks[0]
    for sc in chunks[1:]:
        mx = jnp.maximum(mx, sc)
    m_prev = m_ref[...]
    m_new = jnp.maximum(m_prev, jnp.max(mx, axis=-1, keepdims=True))
    alpha = jnp.exp2(m_prev - m_new)
    ps = [jnp.exp2(sc - m_new) for sc in chunks]
    if l_ref is not None:
        lsum = ps[0]
        for pc in ps[1:]:
            lsum = lsum + pc
        l_ref[...] = alpha * l_ref[...] + lsum
    p = jnp.concatenate([pc.astype(BF16) for pc in ps], axis=1)
    share = p.shape[0] // len(vs)
    for t, v in enumerate(vs):
        rows = slice(t * share, (t + 1) * share)
        acc_ref[rows, :] = alpha[rows] * acc_ref[rows, :] + _dot(p[rows], v)
    m_ref[...] = m_new


def _flash_out(acc_ref, l_ref=None, ones_lane=None):
    acc = acc_ref[...]
    if l_ref is not None:
        den = jnp.sum(l_ref[...], axis=-1, keepdims=True)
    else:
        lane = lax.broadcasted_iota(jnp.int32, acc.shape, 1)
        den = jnp.sum(jnp.where(lane == ones_lane, acc, 0.0), axis=-1, keepdims=True)
    return acc * (1.0 / den)


def _diff_body(q_ref, k_ref, v_ref, lam_ref, gn_ref, o_ref, qq_ref, m_ref, l_ref, acc_ref, *, tq, lam_init,
               single_tile):
    ki = pl.program_id(3)

    @pl.when(ki == 0)
    def _():
        q = q_ref[...]
        lane = lax.broadcasted_iota(jnp.int32, q.shape, 1)
        zero = jnp.zeros_like(q)
        qq_ref[0:tq, :] = jnp.where(lane < HEAD_DIM, q, zero)
        qq_ref[tq:2 * tq, :] = jnp.where(lane >= HEAD_DIM, q, zero)
        _flash_init(m_ref, acc_ref)
        l_ref[...] = jnp.zeros(l_ref.shape, F32)

    if single_tile:
        _flash_step_grouped(qq_ref, k_ref, v_ref[...], m_ref, acc_ref, l_ref, groups=2)
    else:
        _flash_step(qq_ref, k_ref, [v_ref[...]], m_ref, acc_ref, l_ref)

    @pl.when(ki == pl.num_programs(3) - 1)
    def _():
        o = _flash_out(acc_ref, l_ref=l_ref)
        lv = lam_ref[...]
        lam = (jnp.exp(jnp.sum(lv[0:1] * lv[1:2], axis=-1, keepdims=True))
               - jnp.exp(jnp.sum(lv[2:3] * lv[3:4], axis=-1, keepdims=True)) + lam_init)
        a = o[0:tq] - lam * o[tq:2 * tq]
        o_ref[...] = (_rms(a, gn_ref[...]) * (1.0 - lam_init)).astype(BF16)


def _diff_attention(u3, lam_vecs, gn, lam_init, *, tq=1024, tk=2048):
    B, L, _ = u3.shape
    tq = min(tq, L)
    tk = min(tk, L)
    qb, kb, vb = UA_AQ // LANE, UA_AK // LANE, UA_AV // LANE
    return pl.pallas_call(
        functools.partial(_diff_body, tq=tq, lam_init=lam_init, single_tile=(L == tk)),
        out_shape=jax.ShapeDtypeStruct((B, L, A_HEADS * LANE), BF16),
        grid=(B, A_HEADS, L // tq, L // tk),
        in_specs=[
            pl.BlockSpec((None, tq, LANE), lambda b, h, qi, ki: (b, qi, qb + h)),
            pl.BlockSpec((None, tk, LANE), lambda b, h, qi, ki: (b, ki, kb + h)),
            pl.BlockSpec((None, tk, LANE), lambda b, h, qi, ki: (b, ki, vb + h)),
            pl.BlockSpec((4, HEAD_DIM), lambda b, h, qi, ki: (0, 0)),
            pl.BlockSpec((1, LANE), lambda b, h, qi, ki: (0, 0)),
        ],
        out_specs=pl.BlockSpec((None, tq, LANE), lambda b, h, qi, ki: (b, qi, h)),
        scratch_shapes=[
            pltpu.VMEM((2 * tq, LANE), BF16),
            pltpu.VMEM((2 * tq, LANE), F32),
            pltpu.VMEM((2 * tq, LANE), F32),
            pltpu.VMEM((2 * tq, LANE), F32),
        ],
        compiler_params=_cparams(("parallel", "parallel", "parallel", "arbitrary")),
        name="diff_attn",
    )(u3, u3, u3, lam_vecs, gn)


def _gqa_body(q_ref, k_ref, v0_ref, v1_ref, o_ref, qq_ref, m_ref, acc_ref, *, tq):
    ki = pl.program_id(3)

    @pl.when(ki == 0)
    def _():
        for half in range(2):
            for j in range(D_GROUP):
                q = q_ref[:, j * LANE:(j + 1) * LANE]
                lane = lax.broadcasted_iota(jnp.int32, q.shape, 1)
                own = (lane >= HEAD_DIM) if half == 1 else (lane < HEAD_DIM)
                r0 = (half * D_GROUP + j) * tq
                qq_ref[r0:r0 + tq, :] = jnp.where(own, q, jnp.zeros_like(q))
        _flash_init(m_ref, acc_ref)

    def with_ones(v_ref):
        v = v_ref[...]
        lane_v = lax.broadcasted_iota(jnp.int32, v.shape, 1)
        return jnp.where(lane_v == HEAD_DIM, jnp.ones_like(v), v)

    _flash_step(qq_ref, k_ref, [with_ones(v0_ref), with_ones(v1_ref)], m_ref, acc_ref)

    @pl.when(ki == pl.num_programs(3) - 1)
    def _():
        o = _flash_out(acc_ref, ones_lane=HEAD_DIM)
        low = lax.broadcasted_iota(jnp.int32, (tq, LANE), 1) < HEAD_DIM
        for j in range(D_GROUP):
            even = o[j * tq:(j + 1) * tq]
            odd = pltpu.roll(o[(D_GROUP + j) * tq:(D_GROUP + j + 1) * tq], HEAD_DIM, 1)
            o_ref[:, j * LANE:(j + 1) * LANE] = jnp.where(low, even, odd).astype(BF16)


def _gqa_attention(ud3, ua3, *, tq=512, tk=2048):
    B, L, _ = ud3.shape
    tq = min(tq, L)
    tk = min(tk, L)
    gw = D_GROUP * LANE
    R = 2 * D_GROUP * tq
    qb, kb, vb = UD_Q // gw, UD_K // LANE, UA_DV // LANE
    return pl.pallas_call(
        functools.partial(_gqa_body, tq=tq),
        out_shape=jax.ShapeDtypeStruct((B, L, D_Q_HEADS * HEAD_DIM), BF16),
        grid=(B, D_KV_HEADS // 2, L // tq, L // tk),
        in_specs=[
            pl.BlockSpec((None, tq, gw), lambda b, gp, qi, ki: (b, qi, qb + gp)),
            pl.BlockSpec((None, tk, LANE), lambda b, gp, qi, ki: (b, ki, kb + gp)),
            pl.BlockSpec((None, tk, LANE), lambda b, gp, qi, ki: (b, ki, vb + 2 * gp)),
            pl.BlockSpec((None, tk, LANE), lambda b, gp, qi, ki: (b, ki, vb + 2 * gp + 1)),
        ],
        out_specs=pl.BlockSpec((None, tq, gw), lambda b, gp, qi, ki: (b, qi, gp)),
        scratch_shapes=[
            pltpu.VMEM((R, LANE), BF16),
            pltpu.VMEM((R, LANE), F32),
            pltpu.VMEM((R, LANE), F32),
        ],
        compiler_params=_cparams(("parallel", "parallel", "parallel", "arbitrary")),
        name="gqa_attn",
    )(ud3, ud3, ua3, ua3)


def _dil_body(q_ref, k_ref, kp_ref, kn_ref, v_ref, vp_ref, vn_ref, o_ref, kbuf, vbuf, m_ref, l_ref, acc_ref,
              *, TT, HALO, L):
    i = pl.program_id(2)
    H = B_SIDE
    kbuf[0:HALO, :] = kp_ref[...]
    kbuf[HALO:HALO + TT, :] = k_ref[...]
    kbuf[HALO + TT:HALO + TT + HALO, :] = kn_ref[...]
    vbuf[0:HALO, :] = vp_ref[...]
    vbuf[HALO:HALO + TT, :] = v_ref[...]
    vbuf[HALO + TT:HALO + TT + HALO, :] = vn_ref[...]

    tq = LANE
    tw = tq + 2 * H
    head0 = lax.broadcasted_iota(jnp.int32, (tq, LANE), 1) < HEAD_DIM
    row = lax.broadcasted_iota(jnp.int32, (2 * tq, tw), 0)
    col = lax.broadcasted_iota(jnp.int32, (2 * tq, tw), 1)
    band_bias = jnp.where(jnp.abs(col - H - (row % tq)) <= H, 0.0, NEG)
    col1 = lax.broadcasted_iota(jnp.int32, (1, tw), 1)

    def per_head(x):
        return jnp.where(head0, jnp.broadcast_to(x[0:tq], (tq, LANE)), jnp.broadcast_to(x[tq:2 * tq], (tq, LANE)))

    for p, (_, d) in enumerate(reversed(B_PATTERNS)):
        n_sub = L // d

        def tile(idx, carry, d=d, p=p, n_sub=n_sub):
            r = idx % d
            j = idx // d
            q_start = r + d * (j * tq)
            k_start = HALO + r + d * (j * tq - H)
            if d == 1:
                q_rows, k_rows = pl.ds(q_start, tq), pl.ds(k_start, tw)
            else:
                q_rows, k_rows = pl.ds(q_start, tq, stride=d), pl.ds(k_start, tw, stride=d)
            q = q_ref[q_rows, :].astype(BF16)
            zero = jnp.zeros_like(q)
            qq = jnp.concatenate([jnp.where(head0, q, zero), jnp.where(head0, zero, q)], axis=0)
            kw = kbuf[k_rows, :].astype(BF16)
            vw = vbuf[k_rows, :].astype(BF16)
            n0 = (i * TT) // d + j * tq - H
            in_seq = jnp.where((col1 >= -n0) & (col1 < n_sub - n0), 0.0, NEG)
            s = _dot_nt(qq, kw) + band_bias + in_seq
            m = jnp.max(s, axis=-1, keepdims=True)
            e = jnp.exp2(s - m)
            m_t = per_head(m)
            l_t = per_head(jnp.sum(e, axis=-1, keepdims=True))
            pv = _dot(e.astype(BF16), vw)
            pv_t = jnp.where(head0, pv[0:tq], pv[tq:2 * tq])
            if p == 0:
                m_ref[q_rows, :] = m_t
                l_ref[q_rows, :] = l_t
                acc_ref[q_rows, :] = pv_t
            else:
                m_old = m_ref[q_rows, :]
                m_new = jnp.maximum(m_old, m_t)
                a_old = jnp.exp2(m_old - m_new)
                a_t = jnp.exp2(m_t - m_new)
                m_ref[q_rows, :] = m_new
                l_ref[q_rows, :] = a_old * l_ref[q_rows, :] + a_t * l_t
                acc_ref[q_rows, :] = a_old * acc_ref[q_rows, :] + a_t * pv_t
            return carry

        lax.fori_loop(0, TT // tq, tile, 0, unroll=True)

    o_ref[...] = (acc_ref[...] * (1.0 / l_ref[...])).astype(BF16)


B_TILE = 2048
B_HALO = 1024


def _dilated_attention(ub3):
    B, L, _ = ub3.shape
    TT, HALO = B_TILE, B_HALO
    assert L % TT == 0 and TT % HALO == 0 and HALO >= B_SIDE * max(d for _, d in B_PATTERNS)
    pairs = B_HEADS // 2
    qb, kb, vb = UB_Q // LANE, UB_K // LANE, UB_V // LANE
    th = TT // HALO
    nh = L // HALO

    def main(base):
        return pl.BlockSpec((None, TT, LANE), lambda b, hp, i: (b, i, base + hp))

    def prev(base):
        return pl.BlockSpec((None, HALO, LANE), lambda b, hp, i: (b, jnp.maximum(i * th - 1, 0), base + hp))

    def nxt(base):
        return pl.BlockSpec((None, HALO, LANE), lambda b, hp, i: (b, jnp.minimum((i + 1) * th, nh - 1), base + hp))

    return pl.pallas_call(
        functools.partial(_dil_body, TT=TT, HALO=HALO, L=L),
        out_shape=jax.ShapeDtypeStruct((B, L, pairs * LANE), BF16),
        grid=(B, pairs, L // TT),
        in_specs=[main(qb), main(kb), prev(kb), nxt(kb), main(vb), prev(vb), nxt(vb)],
        out_specs=pl.BlockSpec((None, TT, LANE), lambda b, hp, i: (b, i, hp)),
        scratch_shapes=[pltpu.VMEM((TT + 2 * HALO, LANE), F32), pltpu.VMEM((TT + 2 * HALO, LANE), F32),
                        pltpu.VMEM((TT, LANE), F32), pltpu.VMEM((TT, LANE), F32), pltpu.VMEM((TT, LANE), F32)],
        compiler_params=_cparams(("parallel", "parallel", "parallel")),
        name="dilated_attn",
    )(ub3, ub3, ub3, ub3, ub3, ub3, ub3)


def _gla_body(*refs, R, backward):
    if backward:
        (q_ref, k_ref, v_ref, low_ref, wup_ref, bias_ref, fwd_ref, cg_ref, gn_ref, o_ref, s_ref) = refs
    else:
        (q_ref, k_ref, v_ref, low_ref, wup_ref, bias_ref, o_ref, s_ref) = refs
    C = C_CHUNK
    n = R // C

    @pl.when(pl.program_id(2) == 0)
    def _():
        s_ref[...] = jnp.zeros(s_ref.shape, F32)

    x = _dot(low_ref[...], wup_ref[...]) + bias_ref[...]
    gl = (jnp.minimum(x, 0.0) - jnp.log(1.0 + jnp.exp(-jnp.abs(x)))) * (1.0 / C_GATE_NORM)

    lane = lax.broadcasted_iota(jnp.int32, (C, LANE), 1)
    qrow = lax.broadcasted_iota(jnp.int32, (C, LANE), 0)
    key = lane % C
    tri_wide = (key >= qrow) if backward else (key <= qrow)
    head0 = lane < HEAD_DIM
    lane2 = lax.broadcasted_iota(jnp.int32, (C, 2 * LANE), 1)
    vhead0 = lane2 < LANE
    rr = lax.broadcasted_iota(jnp.int32, (LANE, 2 * LANE), 0)
    cc = lax.broadcasted_iota(jnp.int32, (LANE, 2 * LANE), 1)
    blockdiag = (rr < HEAD_DIM) == (cc < LANE)
    eye = (lax.broadcasted_iota(jnp.int32, (LANE, LANE), 0) == lax.broadcasted_iota(jnp.int32, (LANE, LANE), 1))
    zpad = jnp.zeros((C, LANE), F32)
    zpad_v = jnp.zeros((C, 2 * LANE), BF16)

    chunk_rows = [slice(c * C, (c + 1) * C) for c in range(n)]
    pos = lax.broadcasted_iota(jnp.int32, (R, LANE), 0) % C
    cum_all = gl
    for sft in (1, 2, 4, 8, 16, 32):
        if backward:
            cum_all = cum_all + jnp.where(pos < C - sft, pltpu.roll(cum_all, R - sft, 0), 0.0)
        else:
            cum_all = cum_all + jnp.where(pos >= sft, pltpu.roll(cum_all, sft, 0), 0.0)
    cums = [cum_all[rows] for rows in chunk_rows]
    lasts = [(cum[0:1] if backward else cum[C - 1:C]) for cum in cums]
    qds, atts, intra, upds, dec_cols = [], [], [], [], []
    for rows, cum, last in zip(chunk_rows, cums, lasts):
        kc = k_ref[rows, :].astype(F32)
        qds.append((q_ref[rows, :].astype(F32) * jnp.exp(cum)).astype(BF16))
        kinv = kc * jnp.exp(-cum)
        zero = jnp.zeros_like(kinv)
        kbd = jnp.concatenate([jnp.where(head0, kinv, zero), jnp.where(head0, zero, kinv)], axis=0).astype(BF16)
        atts.append(jnp.where(tri_wide, _dot_nt(qds[-1], kbd), 0.0).astype(BF16))
        kdec = kc * jnp.exp(last - cum)
        kdec_t = jnp.concatenate([kdec, zpad], axis=0).T.astype(BF16)
        vc = v_ref[rows, :]
        upd = _dot(kdec_t, jnp.concatenate([vc, zpad_v], axis=0))
        upds.append(jnp.where(blockdiag, upd, 0.0))
        dec_row = jnp.broadcast_to(jnp.exp(last), (LANE, LANE))
        dec_cols.append(jnp.sum(jnp.where(eye, dec_row, 0.0), axis=-1, keepdims=True))
    for rows, att in zip(chunk_rows, atts):
        vc = v_ref[rows, :]
        zv = jnp.zeros_like(vc)
        vbd = jnp.concatenate([jnp.where(vhead0, vc, zv), jnp.where(vhead0, zv, vc)], axis=0)
        intra.append(_dot(att, vbd))

    state = s_ref[...]
    entering = [None] * n
    for c in (range(n - 1, -1, -1) if backward else range(n)):
        entering[c] = state.astype(BF16)
        state = dec_cols[c] * state + upds[c]
    s_ref[...] = state

    if backward:
        gate = cg_ref[...].astype(F32)
        gate = gate * (1.0 / (1.0 + jnp.exp(-gate)))
    for c, rows in enumerate(chunk_rows):
        o = intra[c] + _dot(qds[c], entering[c])
        if backward:
            tot = o + fwd_ref[rows, :]
            for h in range(2):
                hs = slice(h * LANE, (h + 1) * LANE)
                o_ref[rows, hs] = (_rms(tot[:, hs], gn_ref[...]) * gate[rows, hs]).astype(BF16)
        else:
            o_ref[rows, :] = o


def _gla(u3, wup_f, bias_f, wup_b, bias_b, gn, *, R=2048):
    B, L, _ = u3.shape
    R = min(R, L)
    nb = L // R
    pairs = C_HEADS // 2
    W = C_HEADS * LANE
    qb, kb = UA_CQ // LANE, UA_CK // LANE
    vb, gb, lb = UA_CV // (2 * LANE), UA_CG // (2 * LANE), UA_CLOW // (2 * LANE)

    def specs(rev):
        def ri(i):
            return nb - 1 - i if rev else i
        return dict(
            q=pl.BlockSpec((None, R, LANE), lambda b, p, i: (b, ri(i), qb + p)),
            k=pl.BlockSpec((None, R, LANE), lambda b, p, i: (b, ri(i), kb + p)),
            v=pl.BlockSpec((None, R, 2 * LANE), lambda b, p, i: (b, ri(i), vb + p)),
            low=pl.BlockSpec((None, R, 2 * LANE), lambda b, p, i: (b, ri(i), lb)),
            wup=pl.BlockSpec((2 * LANE, LANE), lambda b, p, i: (0, p)),
            bias=pl.BlockSpec((1, LANE), lambda b, p, i: (0, p)),
            out=pl.BlockSpec((None, R, 2 * LANE), lambda b, p, i: (b, ri(i), p)),
            cg=pl.BlockSpec((None, R, 2 * LANE), lambda b, p, i: (b, ri(i), gb + p)),
            gn=pl.BlockSpec((1, LANE), lambda b, p, i: (0, 0)),
        )

    sem = _cparams(("parallel", "parallel", "arbitrary"))
    sf = specs(False)
    fwd = pl.pallas_call(
        functools.partial(_gla_body, R=R, backward=False),
        out_shape=jax.ShapeDtypeStruct((B, L, W), F32),
        grid=(B, pairs, nb),
        in_specs=[sf["q"], sf["k"], sf["v"], sf["low"], sf["wup"], sf["bias"]],
        out_specs=sf["out"],
        scratch_shapes=[pltpu.VMEM((LANE, 2 * LANE), F32)],
        compiler_params=sem,
        name="gla_fwd",
    )(u3, u3, u3, u3, wup_f, bias_f)
    sb = specs(True)
    return pl.pallas_call(
        functools.partial(_gla_body, R=R, backward=True),
        out_shape=jax.ShapeDtypeStruct((B, L, W), BF16),
        grid=(B, pairs, nb),
        in_specs=[sb["q"], sb["k"], sb["v"], sb["low"], sb["wup"], sb["bias"], sb["out"], sb["cg"], sb["gn"]],
        out_specs=sb["out"],
        scratch_shapes=[pltpu.VMEM((LANE, 2 * LANE), F32)],
        compiler_params=sem,
        name="gla_bwd",
    )(u3, u3, u3, u3, wup_b, bias_b, fwd, u3, gn)


def _outproj_body(x_ref, a_ref, b_ref, c_ref, d_ref, wa_ref, wb_ref, wc_ref, wd_ref, ng_ref, o_ref, hn_ref):
    acc = _dot(a_ref[...], wa_ref[...]) + _dot(b_ref[...], wb_ref[...])
    acc = acc + _dot(c_ref[...], wc_ref[...]) + _dot(d_ref[...], wd_ref[...])
    y = x_ref[...] + acc
    o_ref[...] = y
    hn_ref[...] = _rms(y, ng_ref[...]).astype(BF16)


def _outproj(x, oa, ob, oc, od, wa, wb, wc, wd, ng, *, tm=512):
    T, D = x.shape
    tm = min(tm, T)
    row_spec = pl.BlockSpec((tm, D), lambda i: (i, 0))

    def act(w):
        return pl.BlockSpec((tm, w), lambda i: (i, 0))

    def wt(w):
        return pl.BlockSpec((w, D), lambda i: (0, 0), pipeline_mode=pl.Buffered(1))

    widths = [oa.shape[1], ob.shape[1], oc.shape[1], od.shape[1]]
    return pl.pallas_call(
        _outproj_body,
        out_shape=[jax.ShapeDtypeStruct((T, D), F32), jax.ShapeDtypeStruct((T, D), BF16)],
        grid=(T // tm,),
        in_specs=[row_spec] + [act(w) for w in widths] + [wt(w) for w in widths]
        + [pl.BlockSpec((1, D), lambda i: (0, 0))],
        out_specs=[row_spec, row_spec],
        compiler_params=_cparams(("parallel",)),
        name="outproj",
    )(x, oa, ob, oc, od, wa, wb, wc, wd, ng)


def _pad_heads(w, n):
    k = w.shape[0]
    return jnp.pad(w.reshape(k, n, HEAD_DIM), ((0, 0), (0, 0), (0, LANE - HEAD_DIM))).reshape(k, n * LANE)


def _prep_w_in(w):
    sizes = (768, 768, 768, 768, 768, 768, 384, 384, 768, 768, 32, 768, 256, 256)
    offs = [0]
    for s in sizes:
        offs.append(offs[-1] + s)
    (a_q, a_k, a_v, b_q, b_k, b_v, c_q, c_k, c_v, c_g, c_low, d_q, d_k, d_v) = [
        w[:, offs[i]:offs[i + 1]] for i in range(len(sizes))]
    low = jnp.pad(c_low, ((0, 0), (0, 2 * LANE - 2 * C_RANK)))
    wb = jnp.concatenate([b_q, b_k, b_v], axis=1).astype(BF16)
    wa = jnp.concatenate([a_q, a_k, a_v, _pad_heads(d_v, D_KV_HEADS),
                          c_q * QK_SCALE, c_k, c_v, c_g, low], axis=1).astype(BF16)
    order = jnp.asarray(D_HEAD_ORDER)
    d_q = jnp.take(d_q.reshape(-1, D_Q_HEADS, HEAD_DIM), order, axis=1).reshape(-1, D_Q_HEADS * HEAD_DIM)
    wd = jnp.concatenate([d_q, d_k], axis=1).astype(BF16)
    assert wb.shape[1] == UB_W and wa.shape[1] == UA_W and wd.shape[1] == UD_W
    return wb, wa, wd


def _prep_w_out(w):
    wa = w[0:768]
    wb = w[768:1536]
    wc = w[1536:2304]
    wd = w[2304:3072]
    n = w.shape[1]
    wd = jnp.take(wd.reshape(D_Q_HEADS, HEAD_DIM, n), jnp.asarray(D_HEAD_ORDER), axis=0)
    wd = wd.reshape(D_Q_HEADS * HEAD_DIM, n)
    return [t.astype(BF16) for t in (wa, wb, wc, wd)]


def _q_col_scale(width, q_lo, q_hi):
    col = jnp.arange(width)
    return jnp.where((col >= q_lo) & (col < q_hi), QK_SCALE * LOG2E, 1.0).astype(F32).reshape(1, -1)


def _rope_tables(L):
    t = jnp.arange(L, dtype=F32)
    lane = jnp.arange(LANE)
    l64 = lane % HEAD_DIM
    half = ROPE_DIMS // 2
    inv = ROPE_THETA ** (-jnp.arange(0, ROPE_DIMS, 2, dtype=F32) / ROPE_DIMS)
    ang = t[:, None] * inv[None, :]
    ang_l = ang[:, l64 % half]
    in_rot = (l64 < ROPE_DIMS)[None, :]
    c8 = jnp.where(in_rot, jnp.cos(ang_l), 1.0)
    sa8 = jnp.where(((l64 >= half) & (l64 < ROPE_DIMS))[None, :], jnp.sin(ang_l), 0.0)
    sb8 = jnp.where((l64 < half)[None, :], -jnp.sin(ang_l), 0.0)

    q = HEAD_DIM // 4
    inv2 = AXIAL_THETA ** (-jnp.arange(0, HEAD_DIM // 2, 2, dtype=F32) / (HEAD_DIM // 2))
    rows = L // GRID_W
    row_pos = jnp.repeat(jnp.arange(rows, dtype=F32), GRID_W)
    col_pos = jnp.tile(jnp.arange(GRID_W, dtype=F32), rows)
    ang_r = row_pos[:, None] * inv2[None, :]
    ang_c = col_pos[:, None] * inv2[None, :]
    ang_x = jnp.where((l64 < 2 * q)[None, :], ang_r[:, l64 % q], ang_c[:, l64 % q])
    cx = jnp.cos(ang_x)
    upper = ((l64 % (2 * q)) >= q)[None, :]
    sax = jnp.where(upper, jnp.sin(ang_x), 0.0)
    sbx = jnp.where(upper, 0.0, -jnp.sin(ang_x))
    one, zero = jnp.ones_like(c8), jnp.zeros_like(c8)
    rope8 = [jnp.stack([c8, one]), jnp.stack([sa8, zero]), jnp.stack([sb8, zero])]
    axial = [jnp.stack([cx, one]), jnp.stack([sax, zero]), jnp.stack([sbx, zero])]
    return rope8, axial


def _prep_layer(l, ffn1_norm, ffn1_w_gate, ffn1_w_up, ffn1_w_down, mix_norm, w_in, w_out,
                diff_lambda_q1, diff_lambda_k1, diff_lambda_q2, diff_lambda_k2, diff_out_norm,
                gla_gate_up_f, gla_gate_bias_f, gla_gate_up_b, gla_gate_bias_b, gla_out_norm,
                gqa_q_norm, gqa_k_norm, ffn2_norm, ffn2_w_gate, ffn2_w_up, ffn2_w_down):
    def row(v):
        return v.astype(F32).reshape(1, -1)

    gq = jnp.tile(gqa_q_norm[l].astype(F32) * (QK_SCALE * LOG2E), D_Q_HEADS)
    gk = jnp.tile(gqa_k_norm[l].astype(F32), D_KV_HEADS)
    wup_f = jnp.zeros((2 * LANE, C_HEADS * HEAD_DIM), F32).at[0:C_RANK].set(gla_gate_up_f[l])
    wup_b = jnp.zeros((2 * LANE, C_HEADS * HEAD_DIM), F32).at[C_RANK:2 * C_RANK].set(gla_gate_up_b[l])
    return dict(
        n1=row(ffn1_norm[l]), wg1=ffn1_w_gate[l].astype(BF16), wu1=ffn1_w_up[l].astype(BF16),
        wd1=ffn1_w_down[l].astype(BF16),
        nmix=row(mix_norm[l]), w_in=_prep_w_in(w_in[l]), w_out=_prep_w_out(w_out[l]),
        gd=jnp.concatenate([gq, gk]).reshape(1, -1),
        lam=jnp.stack([diff_lambda_q1[l], diff_lambda_k1[l], diff_lambda_q2[l], diff_lambda_k2[l]]).astype(F32),
        lam_init=0.8 - 0.6 * math.exp(-0.3 * l),
        gdiff=row(diff_out_norm[l]),
        wup_f=wup_f.astype(BF16), bias_f=row(gla_gate_bias_f[l]),
        wup_b=wup_b.astype(BF16), bias_b=row(gla_gate_bias_b[l]),
        ggla=row(gla_out_norm[l]),
        n2=row(ffn2_norm[l]), wg2=ffn2_w_gate[l].astype(BF16), wu2=ffn2_w_up[l].astype(BF16),
        wd2=ffn2_w_down[l].astype(BF16),
    )


def _in_projections(h, p, tabs, L):
    rope8, axial = tabs
    w_b, w_a, w_d = p["w_in"]
    half = ROPE_DIMS // 2
    ub = _proj(h, w_b, _q_col_scale(UB_W, UB_Q, UB_K), rope8, L, out_dtype=F32,
               n_rot_tiles=UB_ROPE_TILES, shift=half, head_norm=False, tn=PROJ_TN, name="inproj_b")
    ua = _proj(h, w_a, _q_col_scale(UA_W, UA_AQ, UA_AK), rope8, L, out_dtype=BF16,
               n_rot_tiles=UA_ROPE_TILES, shift=half, head_norm=False, tn=PROJ_TN, name="inproj_a")
    ud = _proj(h, w_d, p["gd"], axial, L, out_dtype=BF16, n_rot_tiles=UD_W // 1024,
               shift=HEAD_DIM // 4, head_norm=True, tn=1024, tm=1024, name="inproj_d")
    return ub, ua, ud


def _trunk(x, layers, final_g):
    B, L, D = x.shape
    xt = x.reshape(B * L, D)
    tabs = _rope_tables(L)
    h_ffn1 = None
    for l, p in enumerate(layers):
        xt, h = _ffn(xt, p["n1"] if h_ffn1 is None else h_ffn1, p["wg1"], p["wu1"], p["wd1"], p["nmix"],
                     tail="next_norm")
        ub, ua, ud = _in_projections(h, p, tabs, L)
        ua3 = ua.reshape(B, L, UA_W)
        oa = _diff_attention(ua3, p["lam"], p["gdiff"], p["lam_init"]).reshape(B * L, -1)
        ob = _dilated_attention(ub.reshape(B, L, UB_W)).reshape(B * L, -1)
        oc = _gla(ua3, p["wup_f"], p["bias_f"], p["wup_b"], p["bias_b"], p["ggla"]).reshape(B * L, -1)
        od = _gqa_attention(ud.reshape(B, L, UD_W), ua3).reshape(B * L, -1)
        xt, h2 = _outproj(xt, oa, ob, oc, od, *p["w_out"], p["n2"])
        if l == len(layers) - 1:
            xt = _ffn(xt, h2, p["wg2"], p["wu2"], p["wd2"], final_g, tail="final_norm")
        else:
            xt, h_ffn1 = _ffn(xt, h2, p["wg2"], p["wu2"], p["wd2"], layers[l + 1]["n1"], tail="next_norm")
    return xt.reshape(B, L, D)


def kernel(x_prompt, x_sample, ffn1_norm, ffn1_w_gate, ffn1_w_up, ffn1_w_down, mix_norm, w_in, w_out, diff_lambda_q1, diff_lambda_k1, diff_lambda_q2, diff_lambda_k2, diff_out_norm, gla_gate_up_f, gla_gate_bias_f, gla_gate_up_b, gla_gate_bias_b, gla_out_norm, gqa_q_norm, gqa_k_norm, ffn2_norm, ffn2_w_gate, ffn2_w_up, ffn2_w_down, final_norm):
    depth = w_in.shape[0]
    layers = [_prep_layer(l, ffn1_norm, ffn1_w_gate, ffn1_w_up, ffn1_w_down, mix_norm, w_in, w_out,
                          diff_lambda_q1, diff_lambda_k1, diff_lambda_q2, diff_lambda_k2, diff_out_norm,
                          gla_gate_up_f, gla_gate_bias_f, gla_gate_up_b, gla_gate_bias_b, gla_out_norm,
                          gqa_q_norm, gqa_k_norm, ffn2_norm, ffn2_w_gate, ffn2_w_up, ffn2_w_down)
              for l in range(depth)]
    final_g = final_norm.astype(F32).reshape(1, -1)
    return (_trunk(x_prompt, layers, final_g), _trunk(x_sample, layers, final_g))
```
